```python
import math
import jax, jax.numpy as jnp
from jax import lax
import numpy as np

D_MODEL = 1024
BATCH = 8
SEQ = 4096
DEPTH = 1

CTX_LEN = 256
GRID_W = 64

GDN_HEADS = 4
GDN_HEAD_DIM = 128
GDN_WIDTH = GDN_HEADS * GDN_HEAD_DIM
CONV_W = 5
CHUNK = 64
ROPE_BASE = 10000.0

NA_HEADS = 8
NA_HEAD_DIM = 64
NA_WIDTH = NA_HEADS * NA_HEAD_DIM
NA_ROWS = 8
NA_COLS = 16

MIX_WIDTH = GDN_WIDTH + NA_WIDTH
IN_COLS = 4 * GDN_WIDTH + 4 * GDN_HEADS + 3 * NA_WIDTH

N_EXPERTS = 16
CAPACITY_FACTOR = 2
D_EXPERT = 1024

EPS = 1e-6

kernel_name = 'hybrid_gdn_natten_ec_dit_block'


def rms_norm(x, w):
    xf = x.astype(jnp.float32)
    y = xf * lax.rsqrt(jnp.mean(xf * xf, axis=-1, keepdims=True) + EPS)
    return y.astype(x.dtype) * w


def l2_normalize(t):
    tf = t.astype(jnp.float32)
    return (tf * lax.rsqrt(jnp.sum(tf * tf, axis=-1, keepdims=True) + EPS)).astype(t.dtype)


def modulate(h, shift, scale):
    return h * (1.0 + scale) + shift


def depthwise_conv(x, w):
    C = x.shape[-1]
    return lax.conv_general_dilated(
        x, w[:, None, :].astype(x.dtype), window_strides=(1,),
        padding=((CONV_W // 2, CONV_W // 2),),
        dimension_numbers=('NWC', 'WIO', 'NWC'), feature_group_count=C)


def axial_rope(x):
    T, D = x.shape[1], x.shape[-1]
    half = D // 2
    pairs = half // 2
    t = jnp.arange(T)
    inv_freq = ROPE_BASE ** (-jnp.arange(pairs, dtype=jnp.float32) / pairs)

    def rot(xa, pos):
        ang = pos.astype(jnp.float32)[:, None] * inv_freq[None, :]
        cos = jnp.cos(ang)[None, :, None, :].astype(x.dtype)
        sin = jnp.sin(ang)[None, :, None, :].astype(x.dtype)
        x1, x2 = xa[..., :pairs], xa[..., pairs:]
        return jnp.concatenate([x1 * cos - x2 * sin, x2 * cos + x1 * sin], axis=-1)

    return jnp.concatenate([rot(x[..., :half], t // GRID_W), rot(x[..., half:], t % GRID_W)], axis=-1)


def gdn_inputs(p_qkv, p_a, p_b, conv_w, a_log, dt_bias, rotary):
    B, T, _ = p_qkv.shape
    qkv = jax.nn.silu(depthwise_conv(p_qkv, conv_w))
    q, k, v = jnp.split(qkv, 3, axis=-1)
    heads = lambda t: t.reshape(B, T, GDN_HEADS, GDN_HEAD_DIM)
    q, k, v = l2_normalize(heads(q)), l2_normalize(heads(k)), heads(v)
    if rotary:
        q, k = axial_rope(q), axial_rope(k)
    a = p_a.astype(jnp.float32).reshape(B, T, 2, GDN_HEADS)
    b = p_b.astype(jnp.float32).reshape(B, T, 2, GDN_HEADS)
    g = -jnp.exp(a_log.astype(jnp.float32)) * jax.nn.softplus(a + dt_bias.astype(jnp.float32))
    beta = jax.nn.sigmoid(b)
    return q, k, v, g, beta


def chunk_gated_delta(q, k, v, g, beta, s0):
    B, H, T, dk = k.shape
    dv = v.shape[-1]
    n = T // CHUNK
    q = q * (dk ** -0.5)
    q, k, v = [t.reshape(B, H, n, CHUNK, t.shape[-1]) for t in (q, k, v)]
    g = g.reshape(B, H, n, CHUNK)
    beta = beta.reshape(B, H, n, CHUNK)
    gcum = jnp.cumsum(g, axis=-1)
    incl = jnp.tril(jnp.ones((CHUNK, CHUNK), dtype=bool))
    strict = jnp.tril(jnp.ones((CHUNK, CHUNK), dtype=bool), -1)
    decay = jnp.exp(jnp.where(incl, gcum[..., :, None] - gcum[..., None, :], -jnp.inf))
    kb = k * beta[..., None]
    lower = jnp.where(strict, jnp.einsum('bhnid,bhnjd->bhnij', kb, k) * decay, 0.0)
    eye = jnp.eye(CHUNK, dtype=jnp.float32)
    tinv = lax.linalg.triangular_solve(eye + lower, jnp.broadcast_to(eye, lower.shape),
                                       left_side=True, lower=True, unit_diagonal=True)
    u = tinv @ (v * beta[..., None])
    w = tinv @ (kb * jnp.exp(gcum)[..., None])
    attn = jnp.where(incl, jnp.einsum('bhnid,bhnjd->bhnij', q, k) * decay, 0.0)
    q_dec = q * jnp.exp(gcum)[..., None]
    k_dec = k * jnp.exp(gcum[..., -1:] - gcum)[..., None]
    g_last = jnp.exp(gcum[..., -1])
    xs = tuple(jnp.moveaxis(t, 2, 0) for t in (q_dec, k_dec, u, w, attn, g_last))

    def step(S, inp):
        qd, kd, ui, wi, ai, gl = inp
        v_new = ui - wi @ S
        o = qd @ S + ai @ v_new
        S = S * gl[..., None, None] + jnp.einsum('bhcd,bhce->bhde', kd, v_new)
        return S, o

    S, o = lax.scan(step, s0, xs)
    return jnp.moveaxis(o, 0, 2).reshape(B, H, T, dv), S


def bidirectional_gated_delta(lat, cin):
    q, k, v, g, beta = lat
    qc, kc, vc, gc, bc = cin
    B = q.shape[0]
    bhtd = lambda t: jnp.swapaxes(t, 1, 2).astype(jnp.float32)
    bht = lambda t, d: jnp.swapaxes(t[:, :, d, :], 1, 2).astype(jnp.float32)
    s0 = jnp.zeros((B, GDN_HEADS, GDN_HEAD_DIM, GDN_HEAD_DIM), jnp.float32)
    outs_l, outs_c = [], []
    for d in range(2):
        order = (lambda t: jnp.flip(t, axis=2)) if d == 1 else (lambda t: t)
        oc, s_ctx = chunk_gated_delta(order(bhtd(qc)), order(bhtd(kc)), order(bhtd(vc)),
                                      order(bht(gc, d)), order(bht(bc, d)), s0)
        ol, _ = chunk_gated_delta(order(bhtd(q)), order(bhtd(k)), order(bhtd(v)),
                                  order(bht(g, d)), order(bht(beta, d)), s_ctx)
        outs_l.append(order(ol))
        outs_c.append(order(oc))
    o_l = jnp.swapaxes(outs_l[0] + outs_l[1], 1, 2).astype(q.dtype)
    o_c = jnp.swapaxes(outs_c[0] + outs_c[1], 1, 2).astype(q.dtype)
    return o_l, o_c


def gated_rms_norm(o, gate, w):
    B, T, H, dv = o.shape
    y = rms_norm(o, w) * jax.nn.silu(gate.reshape(B, T, H, dv))
    return y.reshape(B, T, H * dv)


def neighbourhood_attention(q, k, v, kc, vc, rpb):
    B, T, H, dh = q.shape
    rows = T // GRID_W
    win_rows = min(NA_ROWS, rows)
    grid = lambda t: t.reshape(B, rows, GRID_W, H, dh)
    qg, kg, vg = grid(q * (dh ** -0.5)), grid(k), grid(v)
    col = jnp.arange(GRID_W)
    col_start = jnp.clip(col - NA_COLS // 2, 0, GRID_W - NA_COLS)
    col_in = (col[None, :] >= col_start[:, None]) & (col[None, :] < col_start[:, None] + NA_COLS)
    col_idx = jnp.clip(col[None, :] - col[:, None], -(NA_COLS - 1), NA_COLS - 1) + (NA_COLS - 1)
    rpb_f = rpb.astype(jnp.float32)
    n_lat = win_rows * GRID_W

    def row_block(r):
        start = jnp.clip(r - win_rows // 2, 0, rows - win_rows)
        kr = lax.dynamic_slice_in_dim(kg, start, win_rows, axis=1)
        vr = lax.dynamic_slice_in_dim(vg, start, win_rows, axis=1)
        qr = lax.dynamic_index_in_dim(qg, r, axis=1, keepdims=False)
        row_idx = start + jnp.arange(win_rows) - r + (NA_ROWS - 1)
        bias = rpb_f[:, row_idx][:, :, col_idx].transpose(0, 2, 1, 3)
        s_lat = jnp.einsum('bqhd,brkhd->bhqrk', qr, kr).astype(jnp.float32) + bias[None]
        s_lat = jnp.where(col_in[:, None, :], s_lat, -jnp.inf)
        s_ctx = jnp.einsum('bqhd,bchd->bhqc', qr, kc).astype(jnp.float32)
        s = jnp.concatenate([s_lat.reshape(B, H, GRID_W, n_lat), s_ctx], axis=-1)
        p = jax.nn.softmax(s, axis=-1).astype(q.dtype)
        p_lat = p[..., :n_lat].reshape(B, H, GRID_W, win_rows, GRID_W)
        p_ctx = p[..., n_lat:]
        return (jnp.einsum('bhqrk,brkhd->bqhd', p_lat, vr)
                + jnp.einsum('bhqc,bchd->bqhd', p_ctx, vc))

    out = lax.map(row_block, jnp.arange(rows))
    return jnp.moveaxis(out, 0, 1).reshape(B, T, H * dh)


def context_attention(qc, kc, vc):
    B, Tc, H, dh = qc.shape
    s = jnp.einsum('bqhd,bkhd->bhqk', qc * (dh ** -0.5), kc).astype(jnp.float32)
    p = jax.nn.softmax(s, axis=-1).astype(qc.dtype)
    return jnp.einsum('bhqk,bkhd->bqhd', p, vc).reshape(B, Tc, H * dh)


def token_mixer(h, hc, w_in, conv_qkv, a_log, dt_bias, gdn_norm, na_rpb, w_out, need_ctx):
    B, T, _ = h.shape
    Tc = hc.shape[1]
    cuts = (3 * GDN_WIDTH, 4 * GDN_WIDTH, 4 * GDN_WIDTH + 2 * GDN_HEADS, 4 * GDN_WIDTH + 4 * GDN_HEADS)
    qkv, gate, a, b, na = jnp.split(h @ w_in, cuts, axis=-1)
    qkv_c, gate_c, a_c, b_c, na_c = jnp.split(hc @ w_in, cuts, axis=-1)
    lat = gdn_inputs(qkv, a, b, conv_qkv, a_log, dt_bias, True)
    cin = gdn_inputs(qkv_c, a_c, b_c, conv_qkv, a_log, dt_bias, False)
    o_gdn, o_gdn_c = bidirectional_gated_delta(lat, cin)
    y_gdn = gated_rms_norm(o_gdn, gate, gdn_norm)
    na_heads = lambda t, n: t.reshape(B, n, 3, NA_HEADS, NA_HEAD_DIM)
    nl = na_heads(na, T)
    ncx = na_heads(na_c, Tc)
    y_na = neighbourhood_attention(nl[:, :, 0], nl[:, :, 1], nl[:, :, 2], ncx[:, :, 1], ncx[:, :, 2], na_rpb)
    y = jnp.concatenate([y_gdn, y_na], axis=-1) @ w_out
    if not need_ctx:
        return y, None
    y_c_gdn = gated_rms_norm(o_gdn_c, gate_c, gdn_norm)
    y_c_na = context_attention(ncx[:, :, 0], ncx[:, :, 1], ncx[:, :, 2])
    y_c = jnp.concatenate([y_c_gdn, y_c_na], axis=-1) @ w_out
    return y, y_c


def expert_choice_ffn(h, w_router, w_gate, w_up, w_down):
    B, T, D = h.shape
    cap = CAPACITY_FACTOR * T // N_EXPERTS
    aff = jax.nn.softmax((h @ w_router).astype(jnp.float32), axis=-1)
    gval, idx = lax.top_k(jnp.swapaxes(aff, 1, 2), cap)
    xe = jax.vmap(lambda hb, ib: hb[ib])(h, idx)
    hid = jax.nn.silu(jnp.einsum('becd,edf->becf', xe, w_gate)) * jnp.einsum('becd,edf->becf', xe, w_up)
    ye = jnp.einsum('becf,efd->becd', hid, w_down) * gval[..., None].astype(h.dtype)
    return jax.vmap(lambda yb, ib: jnp.zeros((T, D), yb.dtype).at[ib.reshape(-1)].add(yb.reshape(-1, D)))(ye, idx)


def setup_inputs(seed: int = 0) -> dict:
    key = jax.random.key(seed)
    ks = jax.random.split(key, 20)
    f32 = jnp.float32
    nrm = lambda k, shape, s: jax.random.normal(k, shape, f32) * s
    D = D_MODEL
    dt = jnp.exp(jax.random.uniform(ks[11], (DEPTH, 2, GDN_HEADS), f32, math.log(1e-3), math.log(1e-1)))
    return {
        'x': nrm(ks[0], (BATCH, SEQ, D), 1.0),
        'c': nrm(ks[1], (BATCH, D), 1.0),
        'ctx': nrm(ks[2], (BATCH, CTX_LEN, D), 1.0),
        'c_ctx': nrm(ks[3], (D,), 1.0),
        'w_mod': nrm(ks[4], (DEPTH, D, 6 * D), 0.5 * D ** -0.5),
        'b_mod': nrm(ks[5], (DEPTH, 6 * D), 0.02),
        'norm_mix': 1.0 + nrm(ks[6], (DEPTH, D), 0.1),
        'norm_ffn': 1.0 + nrm(ks[7], (DEPTH, D), 0.1),
        'w_in': nrm(ks[8], (DEPTH, D, IN_COLS), D ** -0.5),
        'conv_qkv': nrm(ks[9], (DEPTH, CONV_W, 3 * GDN_WIDTH), CONV_W ** -0.5),
        'a_log': jnp.log(jax.random.uniform(ks[10], (DEPTH, 2, GDN_HEADS), f32, 1.0, 16.0)),
        'dt_bias': dt + jnp.log(-jnp.expm1(-dt)),
        'gdn_norm': 1.0 + nrm(ks[12], (DEPTH, GDN_HEAD_DIM), 0.1),
        'na_rpb': nrm(ks[13], (DEPTH, NA_HEADS, 2 * NA_ROWS - 1, 2 * NA_COLS - 1), 0.2),
        'w_out': nrm(ks[14], (DEPTH, MIX_WIDTH, D), MIX_WIDTH ** -0.5),
        'w_router': nrm(ks[15], (DEPTH, D, N_EXPERTS), D ** -0.5),
        'w_gate': nrm(ks[16], (DEPTH, N_EXPERTS, D, D_EXPERT), D ** -0.5),
        'w_up': nrm(ks[17], (DEPTH, N_EXPERTS, D, D_EXPERT), D ** -0.5),
        'w_down': nrm(ks[18], (DEPTH, N_EXPERTS, D_EXPERT, D), D_EXPERT ** -0.5),
        'final_norm': 1.0 + nrm(ks[19], (D,), 0.1),
    }


def reference(x, c, ctx, c_ctx, w_mod, b_mod, norm_mix, norm_ffn, w_in, conv_qkv, a_log, dt_bias,
              gdn_norm, na_rpb, w_out, w_router, w_gate, w_up, w_down, final_norm):
    for li in range(DEPTH):
        need_ctx = li + 1 < DEPTH
        mod = jax.nn.silu(c) @ w_mod[li] + b_mod[li]
        mod_c = jax.nn.silu(c_ctx) @ w_mod[li] + b_mod[li]
        sh1, sc1, gt1, sh2, sc2, gt2 = jnp.split(mod[:, None, :], 6, axis=-1)
        sh1c, sc1c, gt1c, sh2c, sc2c, gt2c = jnp.split(mod_c, 6, axis=-1)
        h = modulate(rms_norm(x, norm_mix[li]), sh1, sc1)
        hc = modulate(rms_norm(ctx, norm_mix[li]), sh1c, sc1c)
        y, y_c = token_mixer(h, hc, w_in[li], conv_qkv[li], a_log[li], dt_bias[li], gdn_norm[li],
                             na_rpb[li], w_out[li], need_ctx)
        x = x + gt1 * y
        h2 = modulate(rms_norm(x, norm_ffn[li]), sh2, sc2)
        x = x + gt2 * expert_choice_ffn(h2, w_router[li], w_gate[li], w_up[li], w_down[li])
        if need_ctx:
            ctx = ctx + gt1c * y_c
            hc2 = modulate(rms_norm(ctx, norm_ffn[li]), sh2c, sc2c)
            ctx = ctx + gt2c * expert_choice_ffn(hc2, w_router[li], w_gate[li], w_up[li], w_down[li])
    return rms_norm(x, final_norm)
```

```python
import functools
import math

import numpy as np
import jax
import jax.numpy as jnp
from jax import lax
from jax.experimental import pallas as pl
from jax.experimental.pallas import tpu as pltpu

F32 = jnp.float32
BF16 = jnp.bfloat16

D_MODEL = 1024
SEQ = 4096
CTX_LEN = 256
GRID_W = 64
GRID_ROWS = SEQ // GRID_W
GDN_HEADS = 4
GDN_HEAD_DIM = 128
GDN_WIDTH = GDN_HEADS * GDN_HEAD_DIM
CONV_W = 5
CHUNK = 64
ROPE_BASE = 10000.0
NA_HEADS = 8
NA_HEAD_DIM = 64
NA_WIDTH = NA_HEADS * NA_HEAD_DIM
NA_ROWS = 8
NA_COLS = 16
N_EXPERTS = 16
CAPACITY = 2 * SEQ // N_EXPERTS
EPS = 1e-6

TILE = 256
T_ALL = CTX_LEN + SEQ
N_TILES = T_ALL // TILE
N_CHUNKS = T_ALL // CHUNK
CTX_CHUNKS = CTX_LEN // CHUNK
LANES = 128
SUBLANES = 8
VMEM_LIMIT = 56 * 1024 * 1024


def _params(*sem):
    return pltpu.CompilerParams(dimension_semantics=sem, vmem_limit_bytes=VMEM_LIMIT)


def _dot(a, b):
    return jnp.dot(a, b, preferred_element_type=F32)


def _dot_nt(a, b):
    return lax.dot_general(a, b, (((1,), (1,)), ((), ())), preferred_element_type=F32)


def _dot_tn(a, b):
    return lax.dot_general(a, b, (((0,), (0,)), ((), ())), preferred_element_type=F32)


def _split2(a):
    hi = a.astype(BF16)
    lo = (a - hi.astype(F32)).astype(BF16)
    return hi, lo


def _split3(a):
    hi = a.astype(BF16)
    r = a - hi.astype(F32)
    mid = r.astype(BF16)
    lo = (r - mid.astype(F32)).astype(BF16)
    return hi, mid, lo


def _dot3(a, b):
    ah, al = _split2(a)
    bh, bl = _split2(b)
    return _dot(ah, bh) + (_dot(ah, bl) + _dot(al, bh))


def _silu(x):
    return x * jax.nn.sigmoid(x)


def _mod_kernel(c_ref, w_ref, b_ref, o_ref):
    s = _silu(c_ref[...])
    o_ref[...] = _dot3(s, w_ref[...]) + b_ref[...]


def _modulation(cc, w_mod, b_mod):
    rows, d = cc.shape
    n = w_mod.shape[1]
    bn = 1024
    return pl.pallas_call(
        _mod_kernel,
        grid=(n // bn,),
        in_specs=[pl.BlockSpec((rows, d), lambda j: (0, 0)),
                  pl.BlockSpec((d, bn), lambda j: (0, j)),
                  pl.BlockSpec((1, bn), lambda j: (0, j))],
        out_specs=pl.BlockSpec((rows, bn), lambda j: (0, j)),
        out_shape=jax.ShapeDtypeStruct((rows, n), F32),
        compiler_params=_params("parallel"),
    )(cc, w_mod, b_mod.reshape(1, n))


def _inproj_kernel(x_ref, ctx_ref, sh_ref, sc_ref, nw_ref, wq_ref, wg_ref, wab_ref, wnq_ref, wnkv_ref,
                   gpar_ref, pq_ref, gate_ref, gb_ref, naq_ref, nakv_ref):
    i = pl.program_id(1)
    xt = jnp.where(i == 0, ctx_ref[...], x_ref[...])
    ms = jnp.mean(xt * xt, axis=-1, keepdims=True)
    h = xt * lax.rsqrt(ms + EPS) * nw_ref[...]
    h = h * (1.0 + sc_ref[...]) + sh_ref[...]
    hb = h.astype(BF16)
    pq_ref[...] = _dot(hb, wq_ref[...])
    gate_ref[...] = _dot(hb, wg_ref[...])
    ab = _dot(hb, wab_ref[...])
    z = ab + gpar_ref[1:2, :]
    softplus = jnp.maximum(z, 0.0) + jnp.log1p(jnp.exp(-jnp.abs(z)))
    lane = lax.broadcasted_iota(jnp.int32, ab.shape, 1)
    gb_ref[...] = jnp.where(lane < 2 * GDN_HEADS, gpar_ref[0:1, :] * softplus, jax.nn.sigmoid(ab))
    naq_ref[...] = (_dot(hb, wnq_ref[...]) * (NA_HEAD_DIM ** -0.5)).astype(BF16)
    nakv_ref[...] = _dot(hb, wnkv_ref[...]).astype(BF16)


def _in_projection(x, ctx, sh, sc, norm_w, wq, wg, wab, wnq, wnkv, gpar):
    B = x.shape[0]
    D = D_MODEL
    tok = lambda n: pl.BlockSpec((None, TILE, n), lambda b, i: (b, i, 0))
    full = lambda a: pl.BlockSpec(a.shape, lambda b, i: (0,) * a.ndim)
    modrow = pl.BlockSpec((None, 1, D), lambda b, i: (jnp.where(i == 0, B, b), 0, 0))
    outs = [(3 * GDN_WIDTH, F32), (GDN_WIDTH, F32), (LANES, F32), (NA_WIDTH, BF16), (2 * NA_WIDTH, BF16)]
    return pl.pallas_call(
        _inproj_kernel,
        grid=(B, N_TILES),
        in_specs=[pl.BlockSpec((None, TILE, D), lambda b, i: (b, jnp.maximum(i - 1, 0), 0)),
                  pl.BlockSpec((None, TILE, D), lambda b, i: (b, 0, 0)),
                  modrow, modrow, full(norm_w), full(wq), full(wg), full(wab), full(wnq), full(wnkv), full(gpar)],
        out_specs=[tok(n) for n, _ in outs],
        out_shape=[jax.ShapeDtypeStruct((B, T_ALL, n), dt) for n, dt in outs],
        compiler_params=_params("parallel", "arbitrary"),
    )(x, ctx, sh, sc, norm_w, wq, wg, wab, wnq, wnkv, gpar)


HALO = SUBLANES


def _gdn_prep_kernel(cur_ref, prev_ref, next_ref, cw_ref, cos_ref, sin_ref, out_ref, ext_ref):
    i = pl.program_id(1)
    has_prev = i >= 2
    has_next = jnp.logical_and(i >= 1, i < N_TILES - 1)
    ext_ref[0:HALO, :] = jnp.where(has_prev, prev_ref[...], 0.0)
    ext_ref[HALO:HALO + TILE, :] = cur_ref[...]
    ext_ref[HALO + TILE:, :] = jnp.where(has_next, next_ref[...], 0.0)
    first = HALO - CONV_W // 2
    acc = ext_ref[first:first + TILE, :] * cw_ref[0:1, :]
    for k in range(1, CONV_W):
        acc = acc + ext_ref[first + k:first + k + TILE, :] * cw_ref[k:k + 1, :]
    y = _silu(acc)
    cos = cos_ref[...]
    sin = sin_ref[...]
    lane = lax.broadcasted_iota(jnp.int32, (TILE, LANES), 1)
    take_upper = (lane % (GDN_HEAD_DIM // 2)) < (GDN_HEAD_DIM // 4)
    for j in range(2 * GDN_HEADS):
        t = y[:, j * LANES:(j + 1) * LANES]
        t = t * lax.rsqrt(jnp.sum(t * t, axis=-1, keepdims=True) + EPS)
        partner = jnp.where(take_upper, pltpu.roll(t, LANES - GDN_HEAD_DIM // 4, 1),
                            pltpu.roll(t, GDN_HEAD_DIM // 4, 1))
        t = t * cos + partner * sin
        if j < GDN_HEADS:
            t = t * (GDN_HEAD_DIM ** -0.5)
        out_ref[:, j * LANES:(j + 1) * LANES] = t
    out_ref[:, 2 * GDN_WIDTH:] = y[:, 2 * GDN_WIDTH:]


def _gdn_prep(pq, conv_w, cos_tab, sin_tab):
    B = pq.shape[0]
    W = 3 * GDN_WIDTH
    per = TILE // HALO
    return pl.pallas_call(
        _gdn_prep_kernel,
        grid=(B, N_TILES),
        in_specs=[pl.BlockSpec((None, TILE, W), lambda b, i: (b, i, 0)),
                  pl.BlockSpec((None, HALO, W), lambda b, i: (b, jnp.maximum(i * per - 1, 0), 0)),
                  pl.BlockSpec((None, HALO, W), lambda b, i: (b, jnp.minimum((i + 1) * per, T_ALL // HALO - 1), 0)),
                  pl.BlockSpec((CONV_W, W), lambda b, i: (0, 0)),
                  pl.BlockSpec((TILE, LANES), lambda b, i: (i, 0)),
                  pl.BlockSpec((TILE, LANES), lambda b, i: (i, 0))],
        out_specs=pl.BlockSpec((None, TILE, W), lambda b, i: (b, i, 0)),
        out_shape=jax.ShapeDtypeStruct((B, T_ALL, W), F32),
        scratch_shapes=[pltpu.VMEM((TILE + 2 * HALO, W), F32)],
        compiler_params=_params("parallel", "arbitrary"),
    )(pq, pq, pq, conv_w, cos_tab, sin_tab)


def _rope_tables():
    half = GDN_HEAD_DIM // 2
    pairs = half // 2
    t = np.arange(SEQ)
    inv_freq = jnp.asarray(ROPE_BASE, F32) ** (-jnp.arange(pairs, dtype=F32) / pairs)

    def tab(pos):
        ang = jnp.asarray(pos, F32)[:, None] * inv_freq[None, :]
        c, s = jnp.cos(ang), jnp.sin(ang)
        return jnp.concatenate([c, c], axis=-1), jnp.concatenate([-s, s], axis=-1)

    cr, sr = tab(t // GRID_W)
    cc, sc = tab(t % GRID_W)
    cos = jnp.concatenate([cr, cc], axis=-1)
    sin = jnp.concatenate([sr, sc], axis=-1)
    cos = jnp.concatenate([jnp.ones((CTX_LEN, LANES), F32), cos], axis=0)
    sin = jnp.concatenate([jnp.zeros((CTX_LEN, LANES), F32), sin], axis=0)
    return cos, sin


def _gdn_scan_kernel(qf_ref, qb_ref, gf_ref, gb_ref, of_ref, ob_ref, s_ref):
    n = pl.program_id(1)

    @pl.when(n == 0)
    def _():
        s_ref[...] = jnp.zeros_like(s_ref)

    row = lax.broadcasted_iota(jnp.int32, (CHUNK, CHUNK), 0)
    col = lax.broadcasted_iota(jnp.int32, (CHUNK, CHUNK), 1)
    eye = jnp.where(row == col, 1.0, 0.0).astype(F32)
    chains = range(2 * GDN_HEADS)
    bf = lambda t: t.astype(BF16)
    each = lambda f, *cols: [f(*args) for args in zip(*cols)]
    q, k, v, gcol, grow, gtot, bcol, incl, strict = [], [], [], [], [], [], [], [], []
    for d in range(2):
        src = qf_ref if d == 0 else qb_ref
        gbt = (gf_ref if d == 0 else gb_ref)[...]
        inc = (row >= col) if d == 0 else (row <= col)
        cm = jnp.where(inc, 1.0, 0.0).astype(BF16)
        g1, g2, g3 = _split3(gbt)
        gc = _dot(cm, g1) + (_dot(cm, g2) + _dot(cm, g3))
        gct = gc.T
        tot = gc[CHUNK - 1:CHUNK, :] if d == 0 else gc[0:1, :]
        for h in range(GDN_HEADS):
            c = d * GDN_HEADS + h
            gcol.append(gc[:, c:c + 1])
            grow.append(gct[c:c + 1, :])
            gtot.append(tot[:, c:c + 1])
            bcol.append(gbt[:, 2 * GDN_HEADS + c:2 * GDN_HEADS + c + 1])
            q.append(src[:, h * LANES:(h + 1) * LANES])
            k.append(src[:, GDN_WIDTH + h * LANES:GDN_WIDTH + (h + 1) * LANES])
            v.append(src[:, 2 * GDN_WIDTH + h * LANES:2 * GDN_WIDTH + (h + 1) * LANES])
            incl.append(inc)
            strict.append((row > col) if d == 0 else (row < col))
    s_old = [s_ref[c] for c in chains]
    decay = each(lambda m, gc_, gr_: jnp.exp(jnp.where(m, gc_ - gr_, -jnp.inf)), incl, gcol, grow)
    eg = each(jnp.exp, gcol)
    kb = each(lambda k_, b_: k_ * b_, k, bcol)
    a = each(lambda kb_, q_, k_: _dot_nt(bf(jnp.concatenate([kb_, q_], axis=0)), bf(k_)), kb, q, k)
    lower = each(lambda m, a_, dc: jnp.where(m, a_[:CHUNK] * dc, 0.0), strict, a, decay)
    attn = each(lambda a_, dc: bf(a_[CHUNK:] * dc), a, decay)
    tinv = each(lambda l_: eye - l_, lower)
    m = lower
    for _ in range(int(math.log2(CHUNK)) - 1):
        m = each(lambda m_: _dot(bf(m_), bf(m_)), m)
        tinv = each(lambda t_, m_: t_ + _dot(bf(t_), bf(m_)), tinv, m)
    resid = each(lambda t_, l_: (eye - t_) - _dot3(l_, t_), tinv, lower)
    tinv = each(lambda t_, r_: t_ + _dot(bf(t_), bf(r_)), tinv, resid)
    uw = each(lambda t_, v_, b_, kb_, eg_: _dot(bf(t_), bf(jnp.concatenate([v_ * b_, kb_ * eg_], axis=1))),
              tinv, v, bcol, kb, eg)
    ws = each(lambda uw_, q_, eg_, s_: _dot(bf(jnp.concatenate([uw_[:, LANES:], q_ * eg_], axis=0)), bf(s_)),
              uw, q, eg, s_old)
    vb = each(lambda uw_, ws_: bf(uw_[:, :LANES] - ws_[:CHUNK]), uw, ws)
    o = each(lambda ws_, at_, vb_: ws_[CHUNK:] + _dot(at_, vb_), ws, attn, vb)
    s_new = each(lambda s_, gt_, k_, gc_, vb_: s_ * jnp.exp(gt_) + _dot_tn(bf(k_ * jnp.exp(gt_ - gc_)), vb_),
                 s_old, gtot, k, gcol, vb)
    for c in chains:
        o_ref = of_ref if c < GDN_HEADS else ob_ref
        h = c % GDN_HEADS
        o_ref[:, h * LANES:(h + 1) * LANES] = o[c]
        s_ref[c] = s_new[c]


def _gdn_scan(qkv, gb):
    B = qkv.shape[0]
    fwd = lambda b, n: (b, n, 0)
    bwd = lambda b, n: (b, jnp.where(n < CTX_CHUNKS, CTX_CHUNKS - 1 - n, N_CHUNKS + CTX_CHUNKS - 1 - n), 0)
    out = jax.ShapeDtypeStruct((B, T_ALL, GDN_WIDTH), F32)
    return pl.pallas_call(
        _gdn_scan_kernel,
        grid=(B, N_CHUNKS),
        in_specs=[pl.BlockSpec((None, CHUNK, 3 * GDN_WIDTH), fwd),
                  pl.BlockSpec((None, CHUNK, 3 * GDN_WIDTH), bwd),
                  pl.BlockSpec((None, CHUNK, LANES), fwd),
                  pl.BlockSpec((None, CHUNK, LANES), bwd)],
        out_specs=[pl.BlockSpec((None, CHUNK, GDN_WIDTH), fwd),
                   pl.BlockSpec((None, CHUNK, GDN_WIDTH), bwd)],
        out_shape=[out, out],
        scratch_shapes=[pltpu.VMEM((2 * GDN_HEADS, GDN_HEAD_DIM, GDN_HEAD_DIM), F32)],
        compiler_params=_params("parallel", "arbitrary"),
    )(qkv, qkv, gb, gb)


WIN_TOKENS = NA_ROWS * GRID_W
NA_GROUP = 8


def _na_kernel(q_ref, kv_ref, bias_ref, o_ref):
    r = pl.program_id(1)
    start = jnp.clip(r - NA_ROWS // 2, 0, GRID_ROWS - NA_ROWS)
    off = pl.multiple_of(CTX_LEN + start * GRID_W, GRID_W)
    lane = lax.broadcasted_iota(jnp.int32, (GRID_W, LANES), 1)
    low = lane < NA_HEAD_DIM
    zero = jnp.zeros((GRID_W, LANES), BF16)
    each = lambda f, *cols: [f(*args) for args in zip(*cols)]
    rowmax = lambda t: jnp.max(t, axis=-1, keepdims=True)
    rowsum = lambda t: jnp.sum(t, axis=-1, keepdims=True)
    for g in range(NA_HEADS // NA_GROUP):
        heads = range(g * NA_GROUP, (g + 1) * NA_GROUP)
        klanes = [slice((hd // 2) * LANES, (hd // 2 + 1) * LANES) for hd in heads]
        vlanes = [slice(NA_WIDTH + s.start, NA_WIDTH + s.stop) for s in klanes]
        mine = [low if hd % 2 == 0 else jnp.logical_not(low) for hd in heads]
        qm = each(lambda s, m: jnp.where(m, q_ref[:, s], zero), klanes, mine)
        s_lat = each(lambda q_, s, hd: _dot_nt(q_, kv_ref[pl.ds(off, WIN_TOKENS), s]) + bias_ref[hd],
                     qm, klanes, heads)
        s_ctx = each(lambda q_, s: _dot_nt(q_, kv_ref[0:CTX_LEN, s]), qm, klanes)
        mx = each(lambda a, b: jnp.maximum(rowmax(a), rowmax(b)), s_lat, s_ctx)
        p_lat = each(lambda a, m: jnp.exp(a - m), s_lat, mx)
        p_ctx = each(lambda a, m: jnp.exp(a - m), s_ctx, mx)
        den = each(lambda a, b: rowsum(a) + rowsum(b), p_lat, p_ctx)
        o = each(lambda a, b, s: _dot(a.astype(BF16), kv_ref[pl.ds(off, WIN_TOKENS), s])
                 + _dot(b.astype(BF16), kv_ref[0:CTX_LEN, s]), p_lat, p_ctx, vlanes)
        o = each(lambda o_, d_: o_ / d_, o, den)
        for j in range(0, NA_GROUP, 2):
            o_ref[:, klanes[j]] = jnp.where(low, o[j], o[j + 1]).astype(BF16)


def _neighbourhood_attention(naq, nakv, bias_tab):
    B = naq.shape[0]
    q_blocks_before = CTX_LEN // GRID_W

    def bias_map(b, r):
        start = jnp.clip(r - NA_ROWS // 2, 0, GRID_ROWS - NA_ROWS)
        return (r - start, 0, 0, 0)

    return pl.pallas_call(
        _na_kernel,
        grid=(B, GRID_ROWS),
        in_specs=[pl.BlockSpec((None, GRID_W, NA_WIDTH), lambda b, r: (b, r + q_blocks_before, 0)),
                  pl.BlockSpec((None, T_ALL, 2 * NA_WIDTH), lambda b, r: (b, 0, 0)),
                  pl.BlockSpec((None, NA_HEADS, GRID_W, WIN_TOKENS), bias_map)],
        out_specs=pl.BlockSpec((None, GRID_W, NA_WIDTH), lambda b, r: (b, r, 0)),
        out_shape=jax.ShapeDtypeStruct((B, SEQ, NA_WIDTH), BF16),
        compiler_params=_params("parallel", "arbitrary"),
    )(naq, nakv, bias_tab)


def _na_bias_table(rpb):
    col = np.arange(GRID_W)
    col_start = np.clip(col - NA_COLS // 2, 0, GRID_W - NA_COLS)
    col_in = (col[None, :] >= col_start[:, None]) & (col[None, :] < col_start[:, None] + NA_COLS)
    col_idx = np.clip(col[None, :] - col[:, None], -(NA_COLS - 1), NA_COLS - 1) + (NA_COLS - 1)
    row_idx = np.arange(NA_ROWS)[None, :] - np.arange(NA_ROWS)[:, None] + (NA_ROWS - 1)
    t = rpb.astype(F32)[:, row_idx]
    t = t[:, :, :, col_idx]
    t = jnp.where(col_in[None, None, None], t, -jnp.inf)
    t = jnp.transpose(t, (1, 0, 3, 2, 4))
    return t.reshape(NA_ROWS, NA_HEADS, GRID_W, WIN_TOKENS)


def _outproj_kernel(of_ref, ob_ref, gate_ref, yna_ref, x_ref, gt1_ref, sh2_ref, sc2_ref, gn_ref, nf_ref,
                    wog_ref, won_ref, wr_ref, x1_ref, h2_ref, aff_ref):
    o = of_ref[...] + ob_ref[...]
    ys = []
    for h in range(GDN_HEADS):
        lanes = slice(h * LANES, (h + 1) * LANES)
        oh = o[:, lanes]
        yh = oh * lax.rsqrt(jnp.mean(oh * oh, axis=-1, keepdims=True) + EPS) * gn_ref[...]
        ys.append(yh * _silu(gate_ref[:, lanes]))
    yg = jnp.concatenate(ys, axis=1).astype(BF16)
    y = _dot(yg, wog_ref[...]) + _dot(yna_ref[...], won_ref[...])
    x1 = x_ref[...] + gt1_ref[...] * y
    x1_ref[...] = x1
    h2 = x1 * lax.rsqrt(jnp.mean(x1 * x1, axis=-1, keepdims=True) + EPS) * nf_ref[...]
    h2 = h2 * (1.0 + sc2_ref[...]) + sh2_ref[...]
    h2_ref[...] = h2
    hb = h2.astype(BF16)
    h_lo = (h2 - hb.astype(F32)).astype(BF16)
    wr_hi, wr_lo = _split2(wr_ref[...])
    logits = _dot_nt(wr_hi, hb) + (_dot_nt(wr_hi, h_lo) + _dot_nt(wr_lo, hb))
    e = jnp.exp(logits - jnp.max(logits, axis=0, keepdims=True))
    aff_ref[...] = e / jnp.sum(e, axis=0, keepdims=True)


def _out_projection(o_f, o_b, gate, y_na, x, gt1, sh2, sc2, gdn_norm, norm_ffn, wog, won, wrt):
    B = x.shape[0]
    D = D_MODEL
    lat = lambda n: pl.BlockSpec((None, TILE, n), lambda b, i: (b, i + 1, 0))
    tok = lambda n: pl.BlockSpec((None, TILE, n), lambda b, i: (b, i, 0))
    full = lambda a: pl.BlockSpec(a.shape, lambda b, i: (0,) * a.ndim)
    modrow = pl.BlockSpec((None, 1, D), lambda b, i: (b, 0, 0))
    return pl.pallas_call(
        _outproj_kernel,
        grid=(B, SEQ // TILE),
        in_specs=[lat(GDN_WIDTH), lat(GDN_WIDTH), lat(GDN_WIDTH), tok(NA_WIDTH), tok(D),
                  modrow, modrow, modrow, full(gdn_norm), full(norm_ffn), full(wog), full(won), full(wrt)],
        out_specs=[tok(D), tok(D),
                   pl.BlockSpec((None, N_EXPERTS, TILE), lambda b, i: (b, 0, i))],
        out_shape=[jax.ShapeDtypeStruct((B, SEQ, D), F32),
                   jax.ShapeDtypeStruct((B, SEQ, D), F32),
                   jax.ShapeDtypeStruct((B, N_EXPERTS, SEQ), F32)],
        compiler_params=_params("parallel", "arbitrary"),
    )(o_f, o_b, gate, y_na, x, gt1, sh2, sc2, gdn_norm, norm_ffn, wog, won, wrt)


ROW_BLOCK = 64


def _lane_cumsum(m):
    r = lax.broadcasted_iota(jnp.int32, (LANES, LANES), 0)
    c = lax.broadcasted_iota(jnp.int32, (LANES, LANES), 1)
    tri = jnp.where(r <= c, 1.0, 0.0).astype(BF16)
    carry = jnp.zeros((m.shape[0], 1), F32)
    parts = []
    for j in range(m.shape[1] // LANES):
        blk = _dot(m[:, j * LANES:(j + 1) * LANES].astype(BF16), tri) + carry
        parts.append(blk)
        carry = blk[:, LANES - 1:LANES]
    return jnp.concatenate(parts, axis=1)


def _topk_kernel(aff_ref, idx_ref, gval_ref, cnt_ref, sel_ref):
    a = aff_ref[...]

    def bisect(it, prefix):
        cand = prefix | (jnp.int32(1) << (30 - it))
        cnt = jnp.sum(jnp.where(a >= pltpu.bitcast(cand, F32), 1.0, 0.0), axis=-1, keepdims=True)
        return jnp.where(cnt >= CAPACITY, cand, prefix)

    thr = lax.fori_loop(0, 31, bisect, jnp.zeros((N_EXPERTS, 1), jnp.int32))
    thr_f = pltpu.bitcast(thr, F32)
    gt = a >= pltpu.bitcast(thr + 1, F32)
    eq = jnp.logical_and(a >= thr_f, jnp.logical_not(gt))
    need = CAPACITY - jnp.sum(jnp.where(gt, 1.0, 0.0), axis=-1, keepdims=True)
    eq_f = jnp.where(eq, 1.0, 0.0)
    eq_before = _lane_cumsum(eq_f) - eq_f
    sel = jnp.logical_or(gt, jnp.logical_and(eq, eq_before < need))
    sel_f = jnp.where(sel, 1.0, 0.0)
    cnt = _lane_cumsum(sel_f)
    cnt_ref[...] = cnt
    sel_ref[...] = jnp.where(sel, cnt, 0.0)
    for e in range(N_EXPERTS):
        c_row = cnt_ref[e:e + 1, :]
        s_row = sel_ref[e:e + 1, :]
        a_row = aff_ref[e:e + 1, :]

        def slots(j, _):
            r0 = pl.multiple_of(j * ROW_BLOCK, ROW_BLOCK)
            slot = (lax.broadcasted_iota(jnp.int32, (ROW_BLOCK, 1), 0) + r0).astype(F32)
            idx = jnp.sum(jnp.where(c_row <= slot, 1.0, 0.0), axis=-1, keepdims=True)
            val = jnp.sum(jnp.where(s_row == slot + 1.0, a_row, 0.0), axis=-1, keepdims=True)
            idx_ref[e, pl.ds(r0, ROW_BLOCK), :] = idx.astype(jnp.int32)
            gval_ref[e, pl.ds(r0, ROW_BLOCK), :] = val
            return 0

        lax.fori_loop(0, CAPACITY // ROW_BLOCK, slots, 0)


def _route(aff_t):
    B = aff_t.shape[0]
    out = lambda dt: jax.ShapeDtypeStruct((B, N_EXPERTS, CAPACITY, 1), dt)
    spec = pl.BlockSpec((None, N_EXPERTS, CAPACITY, 1), lambda b: (b, 0, 0, 0))
    return pl.pallas_call(
        _topk_kernel,
        grid=(B,),
        in_specs=[pl.BlockSpec((None, N_EXPERTS, SEQ), lambda b: (b, 0, 0))],
        out_specs=[spec, spec],
        out_shape=[out(jnp.int32), out(F32)],
        scratch_shapes=[pltpu.VMEM((N_EXPERTS, SEQ), F32), pltpu.VMEM((N_EXPERTS, SEQ), F32)],
        compiler_params=_params("parallel"),
    )(aff_t)


GATHER_UNROLL = 8


def _dispatch_kernel(idx_ref, h_ref, xe_ref, rows_ref):
    base = (pl.program_id(0) * N_EXPERTS + pl.program_id(1)) * CAPACITY

    def body(r, _):
        t = idx_ref[base + r]
        rows_ref[pl.ds(r, 1), :] = h_ref[pl.ds(t, 1), :]
        return 0

    lax.fori_loop(0, CAPACITY, body, 0, unroll=GATHER_UNROLL)
    xe_ref[...] = rows_ref[...].astype(BF16)


def _dispatch(idx_flat, h2):
    B = h2.shape[0]
    D = D_MODEL
    return pl.pallas_call(
        _dispatch_kernel,
        grid_spec=pltpu.PrefetchScalarGridSpec(
            num_scalar_prefetch=1,
            grid=(B, N_EXPERTS),
            in_specs=[pl.BlockSpec((None, SEQ, D), lambda b, e, idx: (b, 0, 0))],
            out_specs=pl.BlockSpec((None, None, CAPACITY, D), lambda b, e, idx: (b, e, 0, 0)),
            scratch_shapes=[pltpu.VMEM((CAPACITY, D), F32)]),
        out_shape=jax.ShapeDtypeStruct((B, N_EXPERTS, CAPACITY, D), BF16),
        compiler_params=_params("parallel", "arbitrary"),
    )(idx_flat, h2)


def _ffn_kernel(xe_ref, gv_ref, wg_ref, wu_ref, wd_ref, ye_ref, wgb_ref, wub_ref, wdb_ref):
    @pl.when(pl.program_id(1) == 0)
    def _():
        wgb_ref[...] = wg_ref[...].astype(BF16)
        wub_ref[...] = wu_ref[...].astype(BF16)
        wdb_ref[...] = wd_ref[...].astype(BF16)

    xe = xe_ref[...]
    hid = (_silu(_dot(xe, wgb_ref[...])) * _dot(xe, wub_ref[...])).astype(BF16)
    ye_ref[...] = _dot(hid, wdb_ref[...]) * gv_ref[...]


def _expert_ffn(xe, gval, w_gate, w_up, w_down):
    B = xe.shape[0]
    D = D_MODEL
    F = w_gate.shape[-1]
    return pl.pallas_call(
        _ffn_kernel,
        grid=(N_EXPERTS, B),
        in_specs=[pl.BlockSpec((None, None, CAPACITY, D), lambda e, b: (b, e, 0, 0)),
                  pl.BlockSpec((None, None, CAPACITY, 1), lambda e, b: (b, e, 0, 0)),
                  pl.BlockSpec((None, D, F), lambda e, b: (e, 0, 0)),
                  pl.BlockSpec((None, D, F), lambda e, b: (e, 0, 0)),
                  pl.BlockSpec((None, F, D), lambda e, b: (e, 0, 0))],
        out_specs=pl.BlockSpec((None, None, CAPACITY, D), lambda e, b: (b, e, 0, 0)),
        out_shape=jax.ShapeDtypeStruct((B, N_EXPERTS, CAPACITY, D), F32),
        scratch_shapes=[pltpu.VMEM((D, F), BF16), pltpu.VMEM((D, F), BF16), pltpu.VMEM((F, D), BF16)],
        compiler_params=_params("arbitrary", "arbitrary"),
    )(xe, gval, w_gate, w_up, w_down)


def _combine_kernel(idx_ref, ye_ref, o_ref):
    e = pl.program_id(1)

    @pl.when(e == 0)
    def _():
        o_ref[...] = jnp.zeros_like(o_ref)

    base = (pl.program_id(0) * N_EXPERTS + e) * CAPACITY

    def body(r, _):
        t = idx_ref[base + r]
        o_ref[pl.ds(t, 1), :] = o_ref[pl.ds(t, 1), :] + ye_ref[pl.ds(r, 1), :]
        return 0

    lax.fori_loop(0, CAPACITY, body, 0, unroll=GATHER_UNROLL)


def _combine(idx_flat, ye):
    B = ye.shape[0]
    D = D_MODEL
    return pl.pallas_call(
        _combine_kernel,
        grid_spec=pltpu.PrefetchScalarGridSpec(
            num_scalar_prefetch=1,
            grid=(B, N_EXPERTS),
            in_specs=[pl.BlockSpec((None, None, CAPACITY, D), lambda b, e, idx: (b, e, 0, 0))],
            out_specs=pl.BlockSpec((None, SEQ, D), lambda b, e, idx: (b, 0, 0))),
        out_shape=jax.ShapeDtypeStruct((B, SEQ, D), F32),
        compiler_params=_params("parallel", "arbitrary"),
    )(idx_flat, ye)


def _final_kernel(x1_ref, moe_ref, gt2_ref, fw_ref, o_ref):
    x2 = x1_ref[...] + gt2_ref[...] * moe_ref[...]
    o_ref[...] = x2 * lax.rsqrt(jnp.mean(x2 * x2, axis=-1, keepdims=True) + EPS) * fw_ref[...]


def _final(x1, moe, gt2, final_norm):
    B = x1.shape[0]
    D = D_MODEL
    tok = pl.BlockSpec((None, TILE, D), lambda b, i: (b, i, 0))
    return pl.pallas_call(
        _final_kernel,
        grid=(B, SEQ // TILE),
        in_specs=[tok, tok, pl.BlockSpec((None, 1, D), lambda b, i: (b, 0, 0)),
                  pl.BlockSpec((1, D), lambda b, i: (0, 0))],
        out_specs=tok,
        out_shape=jax.ShapeDtypeStruct((B, SEQ, D), F32),
        compiler_params=_params("parallel", "arbitrary"),
    )(x1, moe, gt2, final_norm)


def kernel(x, c, ctx, c_ctx, w_mod, b_mod, norm_mix, norm_ffn, w_in, conv_qkv, a_log, dt_bias, gdn_norm, na_rpb,
           w_out, w_router, w_gate, w_up, w_down, final_norm):
    B, T, D = x.shape
    assert (T, D) == (SEQ, D_MODEL) and ctx.shape == (B, CTX_LEN, D) and w_mod.shape[0] == 1
    li = 0
    mod_rows = -(-(B + 1) // SUBLANES) * SUBLANES
    cc = jnp.zeros((mod_rows, D), F32).at[:B].set(c).at[B].set(c_ctx)
    mod = _modulation(cc, w_mod[li], b_mod[li])
    part = lambda j, rows: mod[:rows, j * D:(j + 1) * D].reshape(rows, 1, D)
    sh1, sc1 = part(0, B + 1), part(1, B + 1)
    gt1, sh2, sc2, gt2 = part(2, B), part(3, B), part(4, B), part(5, B)

    wi = w_in[li]
    q_end, g_end = 3 * GDN_WIDTH, 4 * GDN_WIDTH
    ab_end = g_end + 4 * GDN_HEADS
    wq = wi[:, :q_end].astype(BF16)
    wg = wi[:, q_end:g_end].astype(BF16)
    wab = jnp.zeros((D, LANES), F32).at[:, :4 * GDN_HEADS].set(wi[:, g_end:ab_end]).astype(BF16)
    wnq = wi[:, ab_end:ab_end + NA_WIDTH].astype(BF16)
    wnkv = wi[:, ab_end + NA_WIDTH:].astype(BF16)
    gpar = jnp.zeros((2, LANES), F32)
    gpar = gpar.at[0, :2 * GDN_HEADS].set(-jnp.exp(a_log[li].astype(F32)).reshape(-1))
    gpar = gpar.at[1, :2 * GDN_HEADS].set(dt_bias[li].astype(F32).reshape(-1))

    pq, gate, gb, naq, nakv = _in_projection(x, ctx, sh1, sc1, norm_mix[li].reshape(1, D),
                                             wq, wg, wab, wnq, wnkv, gpar)
    cos_tab, sin_tab = _rope_tables()
    qkv = _gdn_prep(pq, conv_qkv[li], cos_tab, sin_tab)
    o_f, o_b = _gdn_scan(qkv, gb)
    y_na = _neighbourhood_attention(naq, nakv, _na_bias_table(na_rpb[li]))

    wo = w_out[li].astype(BF16)
    x1, h2, aff_t = _out_projection(o_f, o_b, gate, y_na, x, gt1, sh2, sc2,
                                     gdn_norm[li].reshape(1, GDN_HEAD_DIM), norm_ffn[li].reshape(1, D),
                                     wo[:GDN_WIDTH], wo[GDN_WIDTH:], w_router[li].T)
    idx, gval = _route(aff_t)
    idx_flat = idx.reshape(-1)
    xe = _dispatch(idx_flat, h2)
    ye = _expert_ffn(xe, gval, w_gate[li], w_up[li], w_down[li])
    moe = _combine(idx_flat, ye)
    return _final(x1, moe, gt2, final_norm.reshape(1, D))
```

```python
import functools
import math

import numpy as np
import jax
import jax.numpy as jnp
from jax import lax
from jax.experimental import pallas as pl
from jax.experimental.pallas import tpu as pltpu

F32 = jnp.float32
BF16 = jnp.bfloat16

D_MODEL = 1024
SEQ = 4096
CTX_LEN = 256
GRID_W = 64
GRID_ROWS = SEQ // GRID_W
GDN_HEADS = 4
GDN_HEAD_DIM = 128
GDN_WIDTH = GDN_HEADS * GDN_HEAD_DIM
CONV_W = 5
CHUNK = 64
ROPE_BASE = 10000.0
NA_HEADS = 8
NA_HEAD_DIM = 64
NA_WIDTH = NA_HEADS * NA_HEAD_DIM
NA_ROWS = 8
NA_COLS = 16
N_EXPERTS = 16
CAPACITY = 2 * SEQ // N_EXPERTS
EPS = 1e-6

TILE = 256
T_ALL = CTX_LEN + SEQ
N_TILES = T_ALL // TILE
N_CHUNKS = T_ALL // CHUNK
CTX_CHUNKS = CTX_LEN // CHUNK
LANES = 128
SUBLANES = 8
VMEM_LIMIT = 56 * 1024 * 1024


def _params(*sem):
    return pltpu.CompilerParams(dimension_semantics=sem, vmem_limit_bytes=VMEM_LIMIT)


def _dot(a, b):
    return jnp.dot(a, b, preferred_element_type=F32)


def _dot_nt(a, b):
    return lax.dot_general(a, b, (((1,), (1,)), ((), ())), preferred_element_type=F32)


def _dot_tn(a, b):
    return lax.dot_general(a, b, (((0,), (0,)), ((), ())), preferred_element_type=F32)


def _split2(a):
    hi = a.astype(BF16)
    lo = (a - hi.astype(F32)).astype(BF16)
    return hi, lo


def _split3(a):
    hi = a.astype(BF16)
    r = a - hi.astype(F32)
    mid = r.astype(BF16)
    lo = (r - mid.astype(F32)).astype(BF16)
    return hi, mid, lo


def _dot3(a, b):
    ah, al = _split2(a)
    bh, bl = _split2(b)
    return _dot(ah, bh) + (_dot(ah, bl) + _dot(al, bh))


def _silu(x):
    return x * jax.nn.sigmoid(x)


def _mod_kernel(c_ref, w_ref, b_ref, o_ref):
    s = _silu(c_ref[...])
    o_ref[...] = _dot3(s, w_ref[...]) + b_ref[...]


def _modulation(cc, w_mod, b_mod):
    rows, d = cc.shape
    n = w_mod.shape[1]
    bn = 1024
    return pl.pallas_call(
        _mod_kernel,
        grid=(n // bn,),
        in_specs=[pl.BlockSpec((rows, d), lambda j: (0, 0)),
                  pl.BlockSpec((d, bn), lambda j: (0, j)),
                  pl.BlockSpec((1, bn), lambda j: (0, j))],
        out_specs=pl.BlockSpec((rows, bn), lambda j: (0, j)),
        out_shape=jax.ShapeDtypeStruct((rows, n), F32),
        compiler_params=_params("parallel"),
    )(cc, w_mod, b_mod.reshape(1, n))


def _inproj_kernel(x_ref, ctx_ref, sh_ref, sc_ref, nw_ref, wq_ref, wg_ref, wab_ref, wnq_ref, wnkv_ref,
                   gpar_ref, pq_ref, gate_ref, gb_ref, naq_ref, nakv_ref):
    i = pl.program_id(1)
    xt = jnp.where(i == 0, ctx_ref[...], x_ref[...])
    ms = jnp.mean(xt * xt, axis=-1, keepdims=True)
    h = xt * lax.rsqrt(ms + EPS) * nw_ref[...]
    h = h * (1.0 + sc_ref[...]) + sh_ref[...]
    hb = h.astype(BF16)
    pq_ref[...] = _dot(hb, wq_ref[...])
    gate_ref[...] = _dot(hb, wg_ref[...])
    ab = _dot(hb, wab_ref[...])
    z = ab + gpar_ref[1:2, :]
    softplus = jnp.maximum(z, 0.0) + jnp.log1p(jnp.exp(-jnp.abs(z)))
    lane = lax.broadcasted_iota(jnp.int32, ab.shape, 1)
    gb_ref[...] = jnp.where(lane < 2 * GDN_HEADS, gpar_ref[0:1, :] * softplus, jax.nn.sigmoid(ab))
    naq_ref[...] = (_dot(hb, wnq_ref[...]) * (NA_HEAD_DIM ** -0.5)).astype(BF16)
    nakv_ref[...] = _dot(hb, wnkv_ref[...]).astype(BF16)


def _in_projection(x, ctx, sh, sc, norm_w, wq, wg, wab, wnq, wnkv, gpar):
    B = x.shape[0]
    D = D_MODEL
    tok = lambda n: pl.BlockSpec((None, TILE, n), lambda b, i: (b, i, 0))
    full = lambda a: pl.BlockSpec(a.shape, lambda b, i: (0,) * a.ndim)
    modrow = pl.BlockSpec((None, 1, D), lambda b, i: (jnp.where(i == 0, B, b), 0, 0))
    outs = [(3 * GDN_WIDTH, F32), (GDN_WIDTH, F32), (LANES, F32), (NA_WIDTH, BF16), (2 * NA_WIDTH, BF16)]
    return pl.pallas_call(
        _inproj_kernel,
        grid=(B, N_TILES),
        in_specs=[pl.BlockSpec((None, TILE, D), lambda b, i: (b, jnp.maximum(i - 1, 0), 0)),
                  pl.BlockSpec((None, TILE, D), lambda b, i: (b, 0, 0)),
                  modrow, modrow, full(norm_w), full(wq), full(wg), full(wab), full(wnq), full(wnkv), full(gpar)],
        out_specs=[tok(n) for n, _ in outs],
        out_shape=[jax.ShapeDtypeStruct((B, T_ALL, n), dt) for n, dt in outs],
        compiler_params=_params("parallel", "arbitrary"),
    )(x, ctx, sh, sc, norm_w, wq, wg, wab, wnq, wnkv, gpar)


HALO = SUBLANES


def _gdn_prep_kernel(cur_ref, prev_ref, next_ref, cw_ref, cos_ref, sin_ref, out_ref, ext_ref):
    i = pl.program_id(1)
    has_prev = i >= 2
    has_next = jnp.logical_and(i >= 1, i < N_TILES - 1)
    ext_ref[0:HALO, :] = jnp.where(has_prev, prev_ref[...], 0.0)
    ext_ref[HALO:HALO + TILE, :] = cur_ref[...]
    ext_ref[HALO + TILE:, :] = jnp.where(has_next, next_ref[...], 0.0)
    first = HALO - CONV_W // 2
    acc = ext_ref[first:first + TILE, :] * cw_ref[0:1, :]
    for k in range(1, CONV_W):
        acc = acc + ext_ref[first + k:first + k + TILE, :] * cw_ref[k:k + 1, :]
    y = _silu(acc)
    cos = cos_ref[...]
    sin = sin_ref[...]
    lane = lax.broadcasted_iota(jnp.int32, (TILE, LANES), 1)
    take_upper = (lane % (GDN_HEAD_DIM // 2)) < (GDN_HEAD_DIM // 4)
    for j in range(2 * GDN_HEADS):
        t = y[:, j * LANES:(j + 1) * LANES]
        t = t * lax.rsqrt(jnp.sum(t * t, axis=-1, keepdims=True) + EPS)
        partner = jnp.where(take_upper, pltpu.roll(t, LANES - GDN_HEAD_DIM // 4, 1),
                            pltpu.roll(t, GDN_HEAD_DIM // 4, 1))
        t = t * cos + partner * sin
        if j < GDN_HEADS:
            t = t * (GDN_HEAD_DIM ** -0.5)
        out_ref[:, j * LANES:(j + 1) * LANES] = t
    out_ref[:, 2 * GDN_WIDTH:] = y[:, 2 * GDN_WIDTH:]


def _gdn_prep(pq, conv_w, cos_tab, sin_tab):
    B = pq.shape[0]
    W = 3 * GDN_WIDTH
    per = TILE // HALO
    return pl.pallas_call(
        _gdn_prep_kernel,
        grid=(B, N_TILES),
        in_specs=[pl.BlockSpec((None, TILE, W), lambda b, i: (b, i, 0)),
                  pl.BlockSpec((None, HALO, W), lambda b, i: (b, jnp.maximum(i * per - 1, 0), 0)),
                  pl.BlockSpec((None, HALO, W), lambda b, i: (b, jnp.minimum((i + 1) * per, T_ALL // HALO - 1), 0)),
                  pl.BlockSpec((CONV_W, W), lambda b, i: (0, 0)),
                  pl.BlockSpec((TILE, LANES), lambda b, i: (i, 0)),
                  pl.BlockSpec((TILE, LANES), lambda b, i: (i, 0))],
        out_specs=pl.BlockSpec((None, TILE, W), lambda b, i: (b, i, 0)),
        out_shape=jax.ShapeDtypeStruct((B, T_ALL, W), F32),
        scratch_shapes=[pltpu.VMEM((TILE + 2 * HALO, W), F32)],
        compiler_params=_params("parallel", "arbitrary"),
    )(pq, pq, pq, conv_w, cos_tab, sin_tab)


def _rope_tables():
    half = GDN_HEAD_DIM // 2
    pairs = half // 2
    t = np.arange(SEQ)
    inv_freq = jnp.asarray(ROPE_BASE, F32) ** (-jnp.arange(pairs, dtype=F32) / pairs)

    def tab(pos):
        ang = jnp.asarray(pos, F32)[:, None] * inv_freq[None, :]
        c, s = jnp.cos(ang), jnp.sin(ang)
        return jnp.concatenate([c, c], axis=-1), jnp.concatenate([-s, s], axis=-1)

    cr, sr = tab(t // GRID_W)
    cc, sc = tab(t % GRID_W)
    cos = jnp.concatenate([cr, cc], axis=-1)
    sin = jnp.concatenate([sr, sc], axis=-1)
    cos = jnp.concatenate([jnp.ones((CTX_LEN, LANES), F32), cos], axis=0)
    sin = jnp.concatenate([jnp.zeros((CTX_LEN, LANES), F32), sin], axis=0)
    return cos, sin


SCAN_SAMPLES = 4


def _gdn_scan_kernel(qf_ref, qb_ref, gf_ref, gb_ref, of_ref, ob_ref, s_ref):
    n = pl.program_id(1)

    @pl.when(n == 0)
    def _():
        s_ref[...] = jnp.zeros_like(s_ref)

    row = lax.broadcasted_iota(jnp.int32, (CHUNK, CHUNK), 0)
    col = lax.broadcasted_iota(jnp.int32, (CHUNK, CHUNK), 1)
    eye = jnp.where(row == col, 1.0, 0.0).astype(F32)
    chains = range(SCAN_SAMPLES * 2 * GDN_HEADS)
    bf = lambda t: t.astype(BF16)
    each = lambda f, *cols: [f(*args) for args in zip(*cols)]
    q, k, v, gcol, grow, gtot, bcol, incl, strict = [], [], [], [], [], [], [], [], []
    for s in range(SCAN_SAMPLES):
        for d in range(2):
            src = qf_ref if d == 0 else qb_ref
            gbt = (gf_ref if d == 0 else gb_ref)[s]
            inc = (row >= col) if d == 0 else (row <= col)
            cm = jnp.where(inc, 1.0, 0.0).astype(BF16)
            g1, g2, g3 = _split3(gbt)
            gc = _dot(cm, g1) + (_dot(cm, g2) + _dot(cm, g3))
            gct = gc.T
            tot = gc[CHUNK - 1:CHUNK, :] if d == 0 else gc[0:1, :]
            for h in range(GDN_HEADS):
                c = d * GDN_HEADS + h
                gcol.append(gc[:, c:c + 1])
                grow.append(gct[c:c + 1, :])
                gtot.append(tot[:, c:c + 1])
                bcol.append(gbt[:, 2 * GDN_HEADS + c:2 * GDN_HEADS + c + 1])
                q.append(src[s, :, h * LANES:(h + 1) * LANES])
                k.append(src[s, :, GDN_WIDTH + h * LANES:GDN_WIDTH + (h + 1) * LANES])
                v.append(src[s, :, 2 * GDN_WIDTH + h * LANES:2 * GDN_WIDTH + (h + 1) * LANES])
                incl.append(inc)
                strict.append((row > col) if d == 0 else (row < col))
    s_old = [s_ref[c] for c in chains]
    decay = each(lambda m, gc_, gr_: jnp.exp(jnp.where(m, gc_ - gr_, -jnp.inf)), incl, gcol, grow)
    eg = each(jnp.exp, gcol)
    kb = each(lambda k_, b_: k_ * b_, k, bcol)
    a = each(lambda kb_, q_, k_: _dot_nt(bf(jnp.concatenate([kb_, q_], axis=0)), bf(k_)), kb, q, k)
    lower = each(lambda m, a_, dc: jnp.where(m, a_[:CHUNK] * dc, 0.0), strict, a, decay)
    attn = each(lambda a_, dc: bf(a_[CHUNK:] * dc), a, decay)
    tinv = each(lambda l_: eye - l_, lower)
    m = lower
    for _ in range(int(math.log2(CHUNK)) - 1):
        m = each(lambda m_: _dot(bf(m_), bf(m_)), m)
        tinv = each(lambda t_, m_: t_ + _dot(bf(t_), bf(m_)), tinv, m)
    resid = each(lambda t_, l_: (eye - t_) - _dot3(l_, t_), tinv, lower)
    tinv = each(lambda t_, r_: t_ + _dot(bf(t_), bf(r_)), tinv, resid)
    uw = each(lambda t_, v_, b_, kb_, eg_: _dot(bf(t_), bf(jnp.concatenate([v_ * b_, kb_ * eg_], axis=1))),
              tinv, v, bcol, kb, eg)
    ws = each(lambda uw_, q_, eg_, s_: _dot(bf(jnp.concatenate([uw_[:, LANES:], q_ * eg_], axis=0)), bf(s_)),
              uw, q, eg, s_old)
    vb = each(lambda uw_, ws_: bf(uw_[:, :LANES] - ws_[:CHUNK]), uw, ws)
    o = each(lambda ws_, at_, vb_: ws_[CHUNK:] + _dot(at_, vb_), ws, attn, vb)
    s_new = each(lambda s_, gt_, k_, gc_, vb_: s_ * jnp.exp(gt_) + _dot_tn(bf(k_ * jnp.exp(gt_ - gc_)), vb_),
                 s_old, gtot, k, gcol, vb)
    for c in chains:
        s, d, h = c // (2 * GDN_HEADS), (c // GDN_HEADS) % 2, c % GDN_HEADS
        o_ref = of_ref if d == 0 else ob_ref
        o_ref[s, :, h * LANES:(h + 1) * LANES] = o[c]
        s_ref[c] = s_new[c]


def _gdn_scan(qkv, gb):
    B = qkv.shape[0]
    S = SCAN_SAMPLES
    assert B % S == 0
    fwd = lambda b, n: (b, n, 0)
    bwd = lambda b, n: (b, jnp.where(n < CTX_CHUNKS, CTX_CHUNKS - 1 - n, N_CHUNKS + CTX_CHUNKS - 1 - n), 0)
    out = jax.ShapeDtypeStruct((B, T_ALL, GDN_WIDTH), F32)
    return pl.pallas_call(
        _gdn_scan_kernel,
        grid=(B // S, N_CHUNKS),
        in_specs=[pl.BlockSpec((S, CHUNK, 3 * GDN_WIDTH), fwd),
                  pl.BlockSpec((S, CHUNK, 3 * GDN_WIDTH), bwd),
                  pl.BlockSpec((S, CHUNK, LANES), fwd),
                  pl.BlockSpec((S, CHUNK, LANES), bwd)],
        out_specs=[pl.BlockSpec((S, CHUNK, GDN_WIDTH), fwd),
                   pl.BlockSpec((S, CHUNK, GDN_WIDTH), bwd)],
        out_shape=[out, out],
        scratch_shapes=[pltpu.VMEM((S * 2 * GDN_HEADS, GDN_HEAD_DIM, GDN_HEAD_DIM), F32)],
        compiler_params=_params("parallel", "arbitrary"),
    )(qkv, qkv, gb, gb)


WIN_TOKENS = NA_ROWS * GRID_W
NA_GROUP = 8


def _na_kernel(q_ref, kv_ref, bias_ref, o_ref):
    r = pl.program_id(1)
    start = jnp.clip(r - NA_ROWS // 2, 0, GRID_ROWS - NA_ROWS)
    off = pl.multiple_of(CTX_LEN + start * GRID_W, GRID_W)
    lane = lax.broadcasted_iota(jnp.int32, (GRID_W, LANES), 1)
    low = lane < NA_HEAD_DIM
    zero = jnp.zeros((GRID_W, LANES), BF16)
    each = lambda f, *cols: [f(*args) for args in zip(*cols)]
    rowmax = lambda t: jnp.max(t, axis=-1, keepdims=True)
    rowsum = lambda t: jnp.sum(t, axis=-1, keepdims=True)
    for g in range(NA_HEADS // NA_GROUP):
        heads = range(g * NA_GROUP, (g + 1) * NA_GROUP)
        klanes = [slice((hd // 2) * LANES, (hd // 2 + 1) * LANES) for hd in heads]
        vlanes = [slice(NA_WIDTH + s.start, NA_WIDTH + s.stop) for s in klanes]
        mine = [low if hd % 2 == 0 else jnp.logical_not(low) for hd in heads]
        qm = each(lambda s, m: jnp.where(m, q_ref[:, s], zero), klanes, mine)
        s_lat = each(lambda q_, s, hd: _dot_nt(q_, kv_ref[pl.ds(off, WIN_TOKENS), s]) + bias_ref[hd],
                     qm, klanes, heads)
        s_ctx = each(lambda q_, s: _dot_nt(q_, kv_ref[0:CTX_LEN, s]), qm, klanes)
        mx = each(lambda a, b: jnp.maximum(rowmax(a), rowmax(b)), s_lat, s_ctx)
        p_lat = each(lambda a, m: jnp.exp(a - m), s_lat, mx)
        p_ctx = each(lambda a, m: jnp.exp(a - m), s_ctx, mx)
        den = each(lambda a, b: rowsum(a) + rowsum(b), p_lat, p_ctx)
        o = each(lambda a, b, s: _dot(a.astype(BF16), kv_ref[pl.ds(off, WIN_TOKENS), s])
                 + _dot(b.astype(BF16), kv_ref[0:CTX_LEN, s]), p_lat, p_ctx, vlanes)
        o = each(lambda o_, d_: o_ / d_, o, den)
        for j in range(0, NA_GROUP, 2):
            o_ref[:, klanes[j]] = jnp.where(low, o[j], o[j + 1]).astype(BF16)


def _neighbourhood_attention(naq, nakv, bias_tab):
    B = naq.shape[0]
    q_blocks_before = CTX_LEN // GRID_W

    def bias_map(b, r):
        start = jnp.clip(r - NA_ROWS // 2, 0, GRID_ROWS - NA_ROWS)
        return (r - start, 0, 0, 0)

    return pl.pallas_call(
        _na_kernel,
        grid=(B, GRID_ROWS),
        in_specs=[pl.BlockSpec((None, GRID_W, NA_WIDTH), lambda b, r: (b, r + q_blocks_before, 0)),
                  pl.BlockSpec((None, T_ALL, 2 * NA_WIDTH), lambda b, r: (b, 0, 0)),
                  pl.BlockSpec((None, NA_HEADS, GRID_W, WIN_TOKENS), bias_map)],
        out_specs=pl.BlockSpec((None, GRID_W, NA_WIDTH), lambda b, r: (b, r, 0)),
        out_shape=jax.ShapeDtypeStruct((B, SEQ, NA_WIDTH), BF16),
        compiler_params=_params("parallel", "arbitrary"),
    )(naq, nakv, bias_tab)


def _na_bias_table(rpb):
    col = np.arange(GRID_W)
    col_start = np.clip(col - NA_COLS // 2, 0, GRID_W - NA_COLS)
    col_in = (col[None, :] >= col_start[:, None]) & (col[None, :] < col_start[:, None] + NA_COLS)
    col_idx = np.clip(col[None, :] - col[:, None], -(NA_COLS - 1), NA_COLS - 1) + (NA_COLS - 1)
    row_idx = np.arange(NA_ROWS)[None, :] - np.arange(NA_ROWS)[:, None] + (NA_ROWS - 1)
    row_sel = (row_idx[..., None] == np.arange(2 * NA_ROWS - 1)).astype(np.float32)
    col_sel = (col_idx[..., None] == np.arange(2 * NA_COLS - 1)).astype(np.float32)
    t = jnp.einsum('hrc,vjr,qkc->vhqjk', rpb.astype(F32), row_sel, col_sel, precision=lax.Precision.HIGHEST)
    t = jnp.where(col_in[None, None, :, None, :], t, -jnp.inf)
    return t.reshape(NA_ROWS, NA_HEADS, GRID_W, WIN_TOKENS)


def _outproj_kernel(of_ref, ob_ref, gate_ref, yna_ref, x_ref, gt1_ref, sh2_ref, sc2_ref, gn_ref, nf_ref,
                    wog_ref, won_ref, wr_ref, x1_ref, h2_ref, aff_ref):
    o = of_ref[...] + ob_ref[...]
    ys = []
    for h in range(GDN_HEADS):
        lanes = slice(h * LANES, (h + 1) * LANES)
        oh = o[:, lanes]
        yh = oh * lax.rsqrt(jnp.mean(oh * oh, axis=-1, keepdims=True) + EPS) * gn_ref[...]
        ys.append(yh * _silu(gate_ref[:, lanes]))
    yg = jnp.concatenate(ys, axis=1).astype(BF16)
    y = _dot(yg, wog_ref[...]) + _dot(yna_ref[...], won_ref[...])
    x1 = x_ref[...] + gt1_ref[...] * y
    x1_ref[...] = x1
    h2 = x1 * lax.rsqrt(jnp.mean(x1 * x1, axis=-1, keepdims=True) + EPS) * nf_ref[...]
    h2 = h2 * (1.0 + sc2_ref[...]) + sh2_ref[...]
    h2_ref[...] = h2
    hb = h2.astype(BF16)
    h_lo = (h2 - hb.astype(F32)).astype(BF16)
    wr_hi, wr_lo = _split2(wr_ref[...])
    logits = _dot_nt(wr_hi, hb) + (_dot_nt(wr_hi, h_lo) + _dot_nt(wr_lo, hb))
    e = jnp.exp(logits - jnp.max(logits, axis=0, keepdims=True))
    aff_ref[...] = e / jnp.sum(e, axis=0, keepdims=True)


def _out_projection(o_f, o_b, gate, y_na, x, gt1, sh2, sc2, gdn_norm, norm_ffn, wog, won, wrt):
    B = x.shape[0]
    D = D_MODEL
    lat = lambda n: pl.BlockSpec((None, TILE, n), lambda b, i: (b, i + 1, 0))
    tok = lambda n: pl.BlockSpec((None, TILE, n), lambda b, i: (b, i, 0))
    full = lambda a: pl.BlockSpec(a.shape, lambda b, i: (0,) * a.ndim)
    modrow = pl.BlockSpec((None, 1, D), lambda b, i: (b, 0, 0))
    return pl.pallas_call(
        _outproj_kernel,
        grid=(B, SEQ // TILE),
        in_specs=[lat(GDN_WIDTH), lat(GDN_WIDTH), lat(GDN_WIDTH), tok(NA_WIDTH), tok(D),
                  modrow, modrow, modrow, full(gdn_norm), full(norm_ffn), full(wog), full(won), full(wrt)],
        out_specs=[tok(D), tok(D),
                   pl.BlockSpec((None, N_EXPERTS, TILE), lambda b, i: (b, 0, i))],
        out_shape=[jax.ShapeDtypeStruct((B, SEQ, D), F32),
                   jax.ShapeDtypeStruct((B, SEQ, D), F32),
                   jax.ShapeDtypeStruct((B, N_EXPERTS, SEQ), F32)],
        compiler_params=_params("parallel", "arbitrary"),
    )(o_f, o_b, gate, y_na, x, gt1, sh2, sc2, gdn_norm, norm_ffn, wog, won, wrt)


ROW_BLOCK = 64


def _lane_cumsum(m):
    r = lax.broadcasted_iota(jnp.int32, (LANES, LANES), 0)
    c = lax.broadcasted_iota(jnp.int32, (LANES, LANES), 1)
    tri = jnp.where(r <= c, 1.0, 0.0).astype(BF16)
    carry = jnp.zeros((m.shape[0], 1), F32)
    parts = []
    for j in range(m.shape[1] // LANES):
        blk = _dot(m[:, j * LANES:(j + 1) * LANES].astype(BF16), tri) + carry
        parts.append(blk)
        carry = blk[:, LANES - 1:LANES]
    return jnp.concatenate(parts, axis=1)


def _topk_kernel(aff_ref, idx_ref, gval_ref, cnt_ref, sel_ref):
    a = aff_ref[...]

    def bisect(it, prefix):
        cand = prefix | (jnp.int32(1) << (30 - it))
        cnt = jnp.sum(jnp.where(a >= pltpu.bitcast(cand, F32), 1.0, 0.0), axis=-1, keepdims=True)
        return jnp.where(cnt >= CAPACITY, cand, prefix)

    thr = lax.fori_loop(0, 31, bisect, jnp.zeros((N_EXPERTS, 1), jnp.int32))
    thr_f = pltpu.bitcast(thr, F32)
    gt = a >= pltpu.bitcast(thr + 1, F32)
    eq = jnp.logical_and(a >= thr_f, jnp.logical_not(gt))
    need = CAPACITY - jnp.sum(jnp.where(gt, 1.0, 0.0), axis=-1, keepdims=True)
    eq_f = jnp.where(eq, 1.0, 0.0)
    eq_before = _lane_cumsum(eq_f) - eq_f
    sel = jnp.logical_or(gt, jnp.logical_and(eq, eq_before < need))
    sel_f = jnp.where(sel, 1.0, 0.0)
    cnt = _lane_cumsum(sel_f)
    cnt_ref[...] = cnt
    sel_ref[...] = jnp.where(sel, cnt, 0.0)
    for e in range(N_EXPERTS):
        c_row = cnt_ref[e:e + 1, :]
        s_row = sel_ref[e:e + 1, :]
        a_row = aff_ref[e:e + 1, :]

        def slots(j, _):
            r0 = pl.multiple_of(j * ROW_BLOCK, ROW_BLOCK)
            slot = (lax.broadcasted_iota(jnp.int32, (ROW_BLOCK, 1), 0) + r0).astype(F32)
            idx = jnp.sum(jnp.where(c_row <= slot, 1.0, 0.0), axis=-1, keepdims=True)
            val = jnp.sum(jnp.where(s_row == slot + 1.0, a_row, 0.0), axis=-1, keepdims=True)
            idx_ref[e, pl.ds(r0, ROW_BLOCK), :] = idx.astype(jnp.int32)
            gval_ref[e, pl.ds(r0, ROW_BLOCK), :] = val
            return 0

        lax.fori_loop(0, CAPACITY // ROW_BLOCK, slots, 0)


def _route(aff_t):
    B = aff_t.shape[0]
    out = lambda dt: jax.ShapeDtypeStruct((B, N_EXPERTS, CAPACITY, 1), dt)
    spec = pl.BlockSpec((None, N_EXPERTS, CAPACITY, 1), lambda b: (b, 0, 0, 0))
    return pl.pallas_call(
        _topk_kernel,
        grid=(B,),
        in_specs=[pl.BlockSpec((None, N_EXPERTS, SEQ), lambda b: (b, 0, 0))],
        out_specs=[spec, spec],
        out_shape=[out(jnp.int32), out(F32)],
        scratch_shapes=[pltpu.VMEM((N_EXPERTS, SEQ), F32), pltpu.VMEM((N_EXPERTS, SEQ), F32)],
        compiler_params=_params("parallel"),
    )(aff_t)


GATHER_UNROLL = 8


def _dispatch_kernel(idx_ref, h_ref, xe_ref, rows_ref):
    base = (pl.program_id(0) * N_EXPERTS + pl.program_id(1)) * CAPACITY

    def body(r, _):
        t = idx_ref[base + r]
        rows_ref[pl.ds(r, 1), :] = h_ref[pl.ds(t, 1), :]
        return 0

    lax.fori_loop(0, CAPACITY, body, 0, unroll=GATHER_UNROLL)
    xe_ref[...] = rows_ref[...].astype(BF16)


def _dispatch(idx_flat, h2):
    B = h2.shape[0]
    D = D_MODEL
    return pl.pallas_call(
        _dispatch_kernel,
        grid_spec=pltpu.PrefetchScalarGridSpec(
            num_scalar_prefetch=1,
            grid=(B, N_EXPERTS),
            in_specs=[pl.BlockSpec((None, SEQ, D), lambda b, e, idx: (b, 0, 0))],
            out_specs=pl.BlockSpec((None, None, CAPACITY, D), lambda b, e, idx: (b, e, 0, 0)),
            scratch_shapes=[pltpu.VMEM((CAPACITY, D), F32)]),
        out_shape=jax.ShapeDtypeStruct((B, N_EXPERTS, CAPACITY, D), BF16),
        compiler_params=_params("parallel", "arbitrary"),
    )(idx_flat, h2)


def _ffn_kernel(xe_ref, gv_ref, wg_ref, wu_ref, wd_ref, ye_ref, wgb_ref, wub_ref, wdb_ref):
    @pl.when(pl.program_id(1) == 0)
    def _():
        wgb_ref[...] = wg_ref[...].astype(BF16)
        wub_ref[...] = wu_ref[...].astype(BF16)
        wdb_ref[...] = wd_ref[...].astype(BF16)

    xe = xe_ref[...]
    hid = (_silu(_dot(xe, wgb_ref[...])) * _dot(xe, wub_ref[...])).astype(BF16)
    ye_ref[...] = _dot(hid, wdb_ref[...]) * gv_ref[...]


def _expert_ffn(xe, gval, w_gate, w_up, w_down):
    B = xe.shape[0]
    D = D_MODEL
    F = w_gate.shape[-1]
    return pl.pallas_call(
        _ffn_kernel,
        grid=(N_EXPERTS, B),
        in_specs=[pl.BlockSpec((None, None, CAPACITY, D), lambda e, b: (b, e, 0, 0)),
                  pl.BlockSpec((None, None, CAPACITY, 1), lambda e, b: (b, e, 0, 0)),
                  pl.BlockSpec((None, D, F), lambda e, b: (e, 0, 0)),
                  pl.BlockSpec((None, D, F), lambda e, b: (e, 0, 0)),
                  pl.BlockSpec((None, F, D), lambda e, b: (e, 0, 0))],
        out_specs=pl.BlockSpec((None, None, CAPACITY, D), lambda e, b: (b, e, 0, 0)),
        out_shape=jax.ShapeDtypeStruct((B, N_EXPERTS, CAPACITY, D), F32),
        scratch_shapes=[pltpu.VMEM((D, F), BF16), pltpu.VMEM((D, F), BF16), pltpu.VMEM((F, D), BF16)],
        compiler_params=_params("arbitrary", "arbitrary"),
    )(xe, gval, w_gate, w_up, w_down)


def _combine_kernel(idx_ref, ye_ref, o_ref):
    e = pl.program_id(1)

    @pl.when(e == 0)
    def _():
        o_ref[...] = jnp.zeros_like(o_ref)

    base = (pl.program_id(0) * N_EXPERTS + e) * CAPACITY

    def body(r, _):
        t = idx_ref[base + r]
        o_ref[pl.ds(t, 1), :] = o_ref[pl.ds(t, 1), :] + ye_ref[pl.ds(r, 1), :]
        return 0

    lax.fori_loop(0, CAPACITY, body, 0, unroll=GATHER_UNROLL)


def _combine(idx_flat, ye):
    B = ye.shape[0]
    D = D_MODEL
    return pl.pallas_call(
        _combine_kernel,
        grid_spec=pltpu.PrefetchScalarGridSpec(
            num_scalar_prefetch=1,
            grid=(B, N_EXPERTS),
            in_specs=[pl.BlockSpec((None, None, CAPACITY, D), lambda b, e, idx: (b, e, 0, 0))],
            out_specs=pl.BlockSpec((None, SEQ, D), lambda b, e, idx: (b, 0, 0))),
        out_shape=jax.ShapeDtypeStruct((B, SEQ, D), F32),
        compiler_params=_params("parallel", "arbitrary"),
    )(idx_flat, ye)


def _final_kernel(x1_ref, moe_ref, gt2_ref, fw_ref, o_ref):
    x2 = x1_ref[...] + gt2_ref[...] * moe_ref[...]
    o_ref[...] = x2 * lax.rsqrt(jnp.mean(x2 * x2, axis=-1, keepdims=True) + EPS) * fw_ref[...]


def _final(x1, moe, gt2, final_norm):
    B = x1.shape[0]
    D = D_MODEL
    tok = pl.BlockSpec((None, TILE, D), lambda b, i: (b, i, 0))
    return pl.pallas_call(
        _final_kernel,
        grid=(B, SEQ // TILE),
        in_specs=[tok, tok, pl.BlockSpec((None, 1, D), lambda b, i: (b, 0, 0)),
                  pl.BlockSpec((1, D), lambda b, i: (0, 0))],
        out_specs=tok,
        out_shape=jax.ShapeDtypeStruct((B, SEQ, D), F32),
        compiler_params=_params("parallel", "arbitrary"),
    )(x1, moe, gt2, final_norm)


def kernel(x, c, ctx, c_ctx, w_mod, b_mod, norm_mix, norm_ffn, w_in, conv_qkv, a_log, dt_bias, gdn_norm, na_rpb,
           w_out, w_router, w_gate, w_up, w_down, final_norm):
    B, T, D = x.shape
    assert (T, D) == (SEQ, D_MODEL) and ctx.shape == (B, CTX_LEN, D) and w_mod.shape[0] == 1
    li = 0
    mod_rows = -(-(B + 1) // SUBLANES) * SUBLANES
    cc = jnp.zeros((mod_rows, D), F32).at[:B].set(c).at[B].set(c_ctx)
    mod = _modulation(cc, w_mod[li], b_mod[li])
    part = lambda j, rows: mod[:rows, j * D:(j + 1) * D].reshape(rows, 1, D)
    sh1, sc1 = part(0, B + 1), part(1, B + 1)
    gt1, sh2, sc2, gt2 = part(2, B), part(3, B), part(4, B), part(5, B)

    wi = w_in[li]
    q_end, g_end = 3 * GDN_WIDTH, 4 * GDN_WIDTH
    ab_end = g_end + 4 * GDN_HEADS
    wq = wi[:, :q_end].astype(BF16)
    wg = wi[:, q_end:g_end].astype(BF16)
    wab = jnp.zeros((D, LANES), F32).at[:, :4 * GDN_HEADS].set(wi[:, g_end:ab_end]).astype(BF16)
    wnq = wi[:, ab_end:ab_end + NA_WIDTH].astype(BF16)
    wnkv = wi[:, ab_end + NA_WIDTH:].astype(BF16)
    gpar = jnp.zeros((2, LANES), F32)
    gpar = gpar.at[0, :2 * GDN_HEADS].set(-jnp.exp(a_log[li].astype(F32)).reshape(-1))
    gpar = gpar.at[1, :2 * GDN_HEADS].set(dt_bias[li].astype(F32).reshape(-1))

    pq, gate, gb, naq, nakv = _in_projection(x, ctx, sh1, sc1, norm_mix[li].reshape(1, D),
                                             wq, wg, wab, wnq, wnkv, gpar)
    cos_tab, sin_tab = _rope_tables()
    qkv = _gdn_prep(pq, conv_qkv[li], cos_tab, sin_tab)
    o_f, o_b = _gdn_scan(qkv, gb)
    y_na = _neighbourhood_attention(naq, nakv, _na_bias_table(na_rpb[li]))

    wo = w_out[li].astype(BF16)
    x1, h2, aff_t = _out_projection(o_f, o_b, gate, y_na, x, gt1, sh2, sc2,
                                     gdn_norm[li].reshape(1, GDN_HEAD_DIM), norm_ffn[li].reshape(1, D),
                                     wo[:GDN_WIDTH], wo[GDN_WIDTH:], w_router[li].T)
    idx, gval = _route(aff_t)
    idx_flat = idx.reshape(-1)
    xe = _dispatch(idx_flat, h2)
    ye = _expert_ffn(xe, gval, w_gate[li], w_up[li], w_down[li])
    moe = _combine(idx_flat, ye)
    return _final(x1, moe, gt2, final_norm.reshape(1, D))
```

```python
import functools
import math

import numpy as np
import jax
import jax.numpy as jnp
from jax import lax
from jax.experimental import pallas as pl
from jax.experimental.pallas import tpu as pltpu

F32 = jnp.float32
BF16 = jnp.bfloat16

D_MODEL = 1024
SEQ = 4096
CTX_LEN = 256
GRID_W = 64
GRID_ROWS = SEQ // GRID_W
GDN_HEADS = 4
GDN_HEAD_DIM = 128
GDN_WIDTH = GDN_HEADS * GDN_HEAD_DIM
CONV_W = 5
CHUNK = 64
ROPE_BASE = 10000.0
NA_HEADS = 8
NA_HEAD_DIM = 64
NA_WIDTH = NA_HEADS * NA_HEAD_DIM
NA_ROWS = 8
NA_COLS = 16
N_EXPERTS = 16
CAPACITY = 2 * SEQ // N_EXPERTS
EPS = 1e-6

TILE = 256
T_ALL = CTX_LEN + SEQ
N_TILES = T_ALL // TILE
N_CHUNKS = T_ALL // CHUNK
CTX_CHUNKS = CTX_LEN // CHUNK
LANES = 128
SUBLANES = 8
VMEM_LIMIT = 56 * 1024 * 1024


def _params(*sem):
    return pltpu.CompilerParams(dimension_semantics=sem, vmem_limit_bytes=VMEM_LIMIT)


def _dot(a, b):
    return jnp.dot(a, b, preferred_element_type=F32)


def _dot_nt(a, b):
    return lax.dot_general(a, b, (((1,), (1,)), ((), ())), preferred_element_type=F32)


def _dot_tn(a, b):
    return lax.dot_general(a, b, (((0,), (0,)), ((), ())), preferred_element_type=F32)


def _split2(a):
    hi = a.astype(BF16)
    lo = (a - hi.astype(F32)).astype(BF16)
    return hi, lo


def _split3(a):
    hi = a.astype(BF16)
    r = a - hi.astype(F32)
    mid = r.astype(BF16)
    lo = (r - mid.astype(F32)).astype(BF16)
    return hi, mid, lo


def _dot3(a, b):
    ah, al = _split2(a)
    bh, bl = _split2(b)
    return _dot(ah, bh) + (_dot(ah, bl) + _dot(al, bh))


def _silu(x):
    return x * jax.nn.sigmoid(x)


def _mod_kernel(c_ref, w_ref, b_ref, o_ref):
    s = _silu(c_ref[...])
    o_ref[...] = _dot3(s, w_ref[...]) + b_ref[...]


def _modulation(cc, w_mod, b_mod):
    rows, d = cc.shape
    n = w_mod.shape[1]
    bn = 1024
    return pl.pallas_call(
        _mod_kernel,
        grid=(n // bn,),
        in_specs=[pl.BlockSpec((rows, d), lambda j: (0, 0)),
                  pl.BlockSpec((d, bn), lambda j: (0, j)),
                  pl.BlockSpec((1, bn), lambda j: (0, j))],
        out_specs=pl.BlockSpec((rows, bn), lambda j: (0, j)),
        out_shape=jax.ShapeDtypeStruct((rows, n), F32),
        compiler_params=_params("parallel"),
    )(cc, w_mod, b_mod.reshape(1, n))


def _inproj_kernel(x_ref, ctx_ref, sh_ref, sc_ref, nw_ref, wq_ref, wg_ref, wab_ref, wnq_ref, wnkv_ref,
                   gpar_ref, pq_ref, gate_ref, gb_ref, naq_ref, nakv_ref):
    i = pl.program_id(1)
    xt = jnp.where(i == 0, ctx_ref[...], x_ref[...])
    ms = jnp.mean(xt * xt, axis=-1, keepdims=True)
    h = xt * lax.rsqrt(ms + EPS) * nw_ref[...]
    h = h * (1.0 + sc_ref[...]) + sh_ref[...]
    hb = h.astype(BF16)
    pq_ref[...] = _dot(hb, wq_ref[...])
    gate_ref[...] = _dot(hb, wg_ref[...])
    ab = _dot(hb, wab_ref[...])
    z = ab + gpar_ref[1:2, :]
    softplus = jnp.maximum(z, 0.0) + jnp.log1p(jnp.exp(-jnp.abs(z)))
    lane = lax.broadcasted_iota(jnp.int32, ab.shape, 1)
    gb_ref[...] = jnp.where(lane < 2 * GDN_HEADS, gpar_ref[0:1, :] * softplus, jax.nn.sigmoid(ab))
    naq_ref[...] = (_dot(hb, wnq_ref[...]) * (NA_HEAD_DIM ** -0.5)).astype(BF16)
    nakv_ref[...] = _dot(hb, wnkv_ref[...]).astype(BF16)


def _in_projection(x, ctx, sh, sc, norm_w, wq, wg, wab, wnq, wnkv, gpar):
    B = x.shape[0]
    D = D_MODEL
    tok = lambda n: pl.BlockSpec((None, TILE, n), lambda b, i: (b, i, 0))
    full = lambda a: pl.BlockSpec(a.shape, lambda b, i: (0,) * a.ndim)
    modrow = pl.BlockSpec((None, 1, D), lambda b, i: (jnp.where(i == 0, B, b), 0, 0))
    outs = [(3 * GDN_WIDTH, F32), (GDN_WIDTH, F32), (LANES, F32), (NA_WIDTH, BF16), (2 * NA_WIDTH, BF16)]
    return pl.pallas_call(
        _inproj_kernel,
        grid=(B, N_TILES),
        in_specs=[pl.BlockSpec((None, TILE, D), lambda b, i: (b, jnp.maximum(i - 1, 0), 0)),
                  pl.BlockSpec((None, TILE, D), lambda b, i: (b, 0, 0)),
                  modrow, modrow, full(norm_w), full(wq), full(wg), full(wab), full(wnq), full(wnkv), full(gpar)],
        out_specs=[tok(n) for n, _ in outs],
        out_shape=[jax.ShapeDtypeStruct((B, T_ALL, n), dt) for n, dt in outs],
        compiler_params=_params("parallel", "arbitrary"),
    )(x, ctx, sh, sc, norm_w, wq, wg, wab, wnq, wnkv, gpar)


HALO = SUBLANES


def _gdn_prep_kernel(cur_ref, prev_ref, next_ref, cw_ref, cos_ref, sin_ref, out_ref, ext_ref):
    i = pl.program_id(1)
    has_prev = i >= 2
    has_next = jnp.logical_and(i >= 1, i < N_TILES - 1)
    ext_ref[0:HALO, :] = jnp.where(has_prev, prev_ref[...], 0.0)
    ext_ref[HALO:HALO + TILE, :] = cur_ref[...]
    ext_ref[HALO + TILE:, :] = jnp.where(has_next, next_ref[...], 0.0)
    first = HALO - CONV_W // 2
    acc = ext_ref[first:first + TILE, :] * cw_ref[0:1, :]
    for k in range(1, CONV_W):
        acc = acc + ext_ref[first + k:first + k + TILE, :] * cw_ref[k:k + 1, :]
    y = _silu(acc)
    cos = cos_ref[...]
    sin = sin_ref[...]
    lane = lax.broadcasted_iota(jnp.int32, (TILE, LANES), 1)
    take_upper = (lane % (GDN_HEAD_DIM // 2)) < (GDN_HEAD_DIM // 4)
    for j in range(2 * GDN_HEADS):
        t = y[:, j * LANES:(j + 1) * LANES]
        t = t * lax.rsqrt(jnp.sum(t * t, axis=-1, keepdims=True) + EPS)
        partner = jnp.where(take_upper, pltpu.roll(t, LANES - GDN_HEAD_DIM // 4, 1),
                            pltpu.roll(t, GDN_HEAD_DIM // 4, 1))
        t = t * cos + partner * sin
        if j < GDN_HEADS:
            t = t * (GDN_HEAD_DIM ** -0.5)
        out_ref[:, j * LANES:(j + 1) * LANES] = t
    out_ref[:, 2 * GDN_WIDTH:] = y[:, 2 * GDN_WIDTH:]


def _gdn_prep(pq, conv_w, cos_tab, sin_tab):
    B = pq.shape[0]
    W = 3 * GDN_WIDTH
    per = TILE // HALO
    return pl.pallas_call(
        _gdn_prep_kernel,
        grid=(B, N_TILES),
        in_specs=[pl.BlockSpec((None, TILE, W), lambda b, i: (b, i, 0)),
                  pl.BlockSpec((None, HALO, W), lambda b, i: (b, jnp.maximum(i * per - 1, 0), 0)),
                  pl.BlockSpec((None, HALO, W), lambda b, i: (b, jnp.minimum((i + 1) * per, T_ALL // HALO - 1), 0)),
                  pl.BlockSpec((CONV_W, W), lambda b, i: (0, 0)),
                  pl.BlockSpec((TILE, LANES), lambda b, i: (i, 0)),
                  pl.BlockSpec((TILE, LANES), lambda b, i: (i, 0))],
        out_specs=pl.BlockSpec((None, TILE, W), lambda b, i: (b, i, 0)),
        out_shape=jax.ShapeDtypeStruct((B, T_ALL, W), F32),
        scratch_shapes=[pltpu.VMEM((TILE + 2 * HALO, W), F32)],
        compiler_params=_params("parallel", "arbitrary"),
    )(pq, pq, pq, conv_w, cos_tab, sin_tab)


def _rope_tables():
    half = GDN_HEAD_DIM // 2
    pairs = half // 2
    t = np.arange(SEQ)
    inv_freq = jnp.asarray(ROPE_BASE, F32) ** (-jnp.arange(pairs, dtype=F32) / pairs)

    def tab(pos):
        ang = jnp.asarray(pos, F32)[:, None] * inv_freq[None, :]
        c, s = jnp.cos(ang), jnp.sin(ang)
        return jnp.concatenate([c, c], axis=-1), jnp.concatenate([-s, s], axis=-1)

    cr, sr = tab(t // GRID_W)
    cc, sc = tab(t % GRID_W)
    cos = jnp.concatenate([cr, cc], axis=-1)
    sin = jnp.concatenate([sr, sc], axis=-1)
    cos = jnp.concatenate([jnp.ones((CTX_LEN, LANES), F32), cos], axis=0)
    sin = jnp.concatenate([jnp.zeros((CTX_LEN, LANES), F32), sin], axis=0)
    return cos, sin


SCAN_SAMPLES = 4


def _gdn_scan_kernel(qf_ref, qb_ref, gf_ref, gb_ref, of_ref, ob_ref, s_ref):
    n = pl.program_id(1)

    @pl.when(n == 0)
    def _():
        s_ref[...] = jnp.zeros_like(s_ref)

    row = lax.broadcasted_iota(jnp.int32, (CHUNK, CHUNK), 0)
    col = lax.broadcasted_iota(jnp.int32, (CHUNK, CHUNK), 1)
    eye = jnp.where(row == col, 1.0, 0.0).astype(F32)
    chains = range(SCAN_SAMPLES * 2 * GDN_HEADS)
    bf = lambda t: t.astype(BF16)
    each = lambda f, *cols: [f(*args) for args in zip(*cols)]
    q, k, v, gcol, grow, gtot, bcol, incl, strict = [], [], [], [], [], [], [], [], []
    for s in range(SCAN_SAMPLES):
        for d in range(2):
            src = qf_ref if d == 0 else qb_ref
            gbt = (gf_ref if d == 0 else gb_ref)[s]
            inc = (row >= col) if d == 0 else (row <= col)
            cm = jnp.where(inc, 1.0, 0.0).astype(BF16)
            g1, g2, g3 = _split3(gbt)
            gc = _dot(cm, g1) + (_dot(cm, g2) + _dot(cm, g3))
            gct = gc.T
            tot = gc[CHUNK - 1:CHUNK, :] if d == 0 else gc[0:1, :]
            for h in range(GDN_HEADS):
                c = d * GDN_HEADS + h
                gcol.append(gc[:, c:c + 1])
                grow.append(gct[c:c + 1, :])
                gtot.append(tot[:, c:c + 1])
                bcol.append(gbt[:, 2 * GDN_HEADS + c:2 * GDN_HEADS + c + 1])
                q.append(src[s, :, h * LANES:(h + 1) * LANES])
                k.append(src[s, :, GDN_WIDTH + h * LANES:GDN_WIDTH + (h + 1) * LANES])
                v.append(src[s, :, 2 * GDN_WIDTH + h * LANES:2 * GDN_WIDTH + (h + 1) * LANES])
                incl.append(inc)
                strict.append((row > col) if d == 0 else (row < col))
    s_old = [s_ref[c] for c in chains]
    decay = each(lambda m, gc_, gr_: jnp.exp(jnp.where(m, gc_ - gr_, -jnp.inf)), incl, gcol, grow)
    eg = each(jnp.exp, gcol)
    kb = each(lambda k_, b_: k_ * b_, k, bcol)
    a = each(lambda kb_, q_, k_: _dot_nt(bf(jnp.concatenate([kb_, q_], axis=0)), bf(k_)), kb, q, k)
    lower = each(lambda m, a_, dc: jnp.where(m, a_[:CHUNK] * dc, 0.0), strict, a, decay)
    attn = each(lambda a_, dc: bf(a_[CHUNK:] * dc), a, decay)
    tinv = each(lambda l_: eye - l_, lower)
    m = lower
    for _ in range(int(math.log2(CHUNK)) - 1):
        m = each(lambda m_: _dot(bf(m_), bf(m_)), m)
        tinv = each(lambda t_, m_: t_ + _dot(bf(t_), bf(m_)), tinv, m)
    resid = each(lambda t_, l_: (eye - t_) - _dot3(l_, t_), tinv, lower)
    tinv = each(lambda t_, r_: t_ + _dot(bf(t_), bf(r_)), tinv, resid)
    uw = each(lambda t_, v_, b_, kb_, eg_: _dot(bf(t_), bf(jnp.concatenate([v_ * b_, kb_ * eg_], axis=1))),
              tinv, v, bcol, kb, eg)
    ws = each(lambda uw_, q_, eg_, s_: _dot(bf(jnp.concatenate([uw_[:, LANES:], q_ * eg_], axis=0)), bf(s_)),
              uw, q, eg, s_old)
    vb = each(lambda uw_, ws_: bf(uw_[:, :LANES] - ws_[:CHUNK]), uw, ws)
    o = each(lambda ws_, at_, vb_: ws_[CHUNK:] + _dot(at_, vb_), ws, attn, vb)
    s_new = each(lambda s_, gt_, k_, gc_, vb_: s_ * jnp.exp(gt_) + _dot_tn(bf(k_ * jnp.exp(gt_ - gc_)), vb_),
                 s_old, gtot, k, gcol, vb)
    for c in chains:
        s, d, h = c // (2 * GDN_HEADS), (c // GDN_HEADS) % 2, c % GDN_HEADS
        o_ref = of_ref if d == 0 else ob_ref
        o_ref[s, :, h * LANES:(h + 1) * LANES] = o[c]
        s_ref[c] = s_new[c]


def _gdn_scan(qkv, gb):
    B = qkv.shape[0]
    S = SCAN_SAMPLES
    assert B % S == 0
    fwd = lambda b, n: (b, n, 0)
    bwd = lambda b, n: (b, jnp.where(n < CTX_CHUNKS, CTX_CHUNKS - 1 - n, N_CHUNKS + CTX_CHUNKS - 1 - n), 0)
    out = jax.ShapeDtypeStruct((B, T_ALL, GDN_WIDTH), F32)
    return pl.pallas_call(
        _gdn_scan_kernel,
        grid=(B // S, N_CHUNKS),
        in_specs=[pl.BlockSpec((S, CHUNK, 3 * GDN_WIDTH), fwd),
                  pl.BlockSpec((S, CHUNK, 3 * GDN_WIDTH), bwd),
                  pl.BlockSpec((S, CHUNK, LANES), fwd),
                  pl.BlockSpec((S, CHUNK, LANES), bwd)],
        out_specs=[pl.BlockSpec((S, CHUNK, GDN_WIDTH), fwd),
                   pl.BlockSpec((S, CHUNK, GDN_WIDTH), bwd)],
        out_shape=[out, out],
        scratch_shapes=[pltpu.VMEM((S * 2 * GDN_HEADS, GDN_HEAD_DIM, GDN_HEAD_DIM), F32)],
        compiler_params=_params("parallel", "arbitrary"),
    )(qkv, qkv, gb, gb)


WIN_TOKENS = NA_ROWS * GRID_W
NA_GROUP = 8


def _na_kernel(q_ref, kv_ref, bias_ref, o_ref):
    r = pl.program_id(1)
    start = jnp.clip(r - NA_ROWS // 2, 0, GRID_ROWS - NA_ROWS)
    off = pl.multiple_of(CTX_LEN + start * GRID_W, GRID_W)
    lane = lax.broadcasted_iota(jnp.int32, (GRID_W, LANES), 1)
    low = lane < NA_HEAD_DIM
    zero = jnp.zeros((GRID_W, LANES), BF16)
    each = lambda f, *cols: [f(*args) for args in zip(*cols)]
    rowmax = lambda t: jnp.max(t, axis=-1, keepdims=True)
    rowsum = lambda t: jnp.sum(t, axis=-1, keepdims=True)
    for g in range(NA_HEADS // NA_GROUP):
        pairs = range(g * NA_GROUP // 2, (g + 1) * NA_GROUP // 2)
        klanes = [slice(p * LANES, (p + 1) * LANES) for p in pairs]
        vlanes = [slice(NA_WIDTH + s.start, NA_WIDTH + s.stop) for s in klanes]
        q2 = each(lambda s: jnp.concatenate([jnp.where(low, q_ref[:, s], zero),
                                             jnp.where(low, zero, q_ref[:, s])], axis=0), klanes)
        bias = each(lambda p: jnp.concatenate([bias_ref[2 * p], bias_ref[2 * p + 1]], axis=0), pairs)
        s_lat = each(lambda q_, s, b_: _dot_nt(q_, kv_ref[pl.ds(off, WIN_TOKENS), s]) + b_, q2, klanes, bias)
        s_ctx = each(lambda q_, s: _dot_nt(q_, kv_ref[0:CTX_LEN, s]), q2, klanes)
        mx = each(lambda a, b: jnp.maximum(rowmax(a), rowmax(b)), s_lat, s_ctx)
        p_lat = each(lambda a, m: jnp.exp(a - m), s_lat, mx)
        p_ctx = each(lambda a, m: jnp.exp(a - m), s_ctx, mx)
        den = each(lambda a, b: rowsum(a) + rowsum(b), p_lat, p_ctx)
        o = each(lambda a, b, s: _dot(a.astype(BF16), kv_ref[pl.ds(off, WIN_TOKENS), s])
                 + _dot(b.astype(BF16), kv_ref[0:CTX_LEN, s]), p_lat, p_ctx, vlanes)
        o = each(lambda o_, d_: o_ / d_, o, den)
        for s, o_ in zip(klanes, o):
            o_ref[:, s] = jnp.where(low, o_[:GRID_W], o_[GRID_W:]).astype(BF16)


def _neighbourhood_attention(naq, nakv, bias_tab):
    B = naq.shape[0]
    q_blocks_before = CTX_LEN // GRID_W

    def bias_map(b, r):
        start = jnp.clip(r - NA_ROWS // 2, 0, GRID_ROWS - NA_ROWS)
        return (r - start, 0, 0, 0)

    return pl.pallas_call(
        _na_kernel,
        grid=(B, GRID_ROWS),
        in_specs=[pl.BlockSpec((None, GRID_W, NA_WIDTH), lambda b, r: (b, r + q_blocks_before, 0)),
                  pl.BlockSpec((None, T_ALL, 2 * NA_WIDTH), lambda b, r: (b, 0, 0)),
                  pl.BlockSpec((None, NA_HEADS, GRID_W, WIN_TOKENS), bias_map)],
        out_specs=pl.BlockSpec((None, GRID_W, NA_WIDTH), lambda b, r: (b, r, 0)),
        out_shape=jax.ShapeDtypeStruct((B, SEQ, NA_WIDTH), BF16),
        compiler_params=_params("parallel", "arbitrary"),
    )(naq, nakv, bias_tab)


def _na_bias_table(rpb):
    col = np.arange(GRID_W)
    col_start = np.clip(col - NA_COLS // 2, 0, GRID_W - NA_COLS)
    col_in = (col[None, :] >= col_start[:, None]) & (col[None, :] < col_start[:, None] + NA_COLS)
    col_idx = np.clip(col[None, :] - col[:, None], -(NA_COLS - 1), NA_COLS - 1) + (NA_COLS - 1)
    row_idx = np.arange(NA_ROWS)[None, :] - np.arange(NA_ROWS)[:, None] + (NA_ROWS - 1)
    row_sel = (row_idx[..., None] == np.arange(2 * NA_ROWS - 1)).astype(np.float32)
    col_sel = (col_idx[..., None] == np.arange(2 * NA_COLS - 1)).astype(np.float32)
    t = jnp.einsum('hrc,vjr,qkc->vhqjk', rpb.astype(F32), row_sel, col_sel, precision=lax.Precision.HIGHEST)
    t = jnp.where(col_in[None, None, :, None, :], t, -jnp.inf)
    return t.reshape(NA_ROWS, NA_HEADS, GRID_W, WIN_TOKENS)


def _outproj_kernel(of_ref, ob_ref, gate_ref, yna_ref, x_ref, gt1_ref, sh2_ref, sc2_ref, gn_ref, nf_ref,
                    wog_ref, won_ref, wr_ref, x1_ref, h2_ref, aff_ref):
    o = of_ref[...] + ob_ref[...]
    ys = []
    for h in range(GDN_HEADS):
        lanes = slice(h * LANES, (h + 1) * LANES)
        oh = o[:, lanes]
        yh = oh * lax.rsqrt(jnp.mean(oh * oh, axis=-1, keepdims=True) + EPS) * gn_ref[...]
        ys.append(yh * _silu(gate_ref[:, lanes]))
    yg = jnp.concatenate(ys, axis=1).astype(BF16)
    y = _dot(yg, wog_ref[...]) + _dot(yna_ref[...], won_ref[...])
    x1 = x_ref[...] + gt1_ref[...] * y
    x1_ref[...] = x1
    h2 = x1 * lax.rsqrt(jnp.mean(x1 * x1, axis=-1, keepdims=True) + EPS) * nf_ref[...]
    h2 = h2 * (1.0 + sc2_ref[...]) + sh2_ref[...]
    h2_ref[...] = h2
    hb = h2.astype(BF16)
    h_lo = (h2 - hb.astype(F32)).astype(BF16)
    wr_hi, wr_lo = _split2(wr_ref[...])
    logits = _dot_nt(wr_hi, hb) + (_dot_nt(wr_hi, h_lo) + _dot_nt(wr_lo, hb))
    e = jnp.exp(logits - jnp.max(logits, axis=0, keepdims=True))
    aff_ref[...] = e / jnp.sum(e, axis=0, keepdims=True)


def _out_projection(o_f, o_b, gate, y_na, x, gt1, sh2, sc2, gdn_norm, norm_ffn, wog, won, wrt):
    B = x.shape[0]
    D = D_MODEL
    lat = lambda n: pl.BlockSpec((None, TILE, n), lambda b, i: (b, i + 1, 0))
    tok = lambda n: pl.BlockSpec((None, TILE, n), lambda b, i: (b, i, 0))
    full = lambda a: pl.BlockSpec(a.shape, lambda b, i: (0,) * a.ndim)
    modrow = pl.BlockSpec((None, 1, D), lambda b, i: (b, 0, 0))
    return pl.pallas_call(
        _outproj_kernel,
        grid=(B, SEQ // TILE),
        in_specs=[lat(GDN_WIDTH), lat(GDN_WIDTH), lat(GDN_WIDTH), tok(NA_WIDTH), tok(D),
                  modrow, modrow, modrow, full(gdn_norm), full(norm_ffn), full(wog), full(won), full(wrt)],
        out_specs=[tok(D), tok(D),
                   pl.BlockSpec((None, N_EXPERTS, TILE), lambda b, i: (b, 0, i))],
        out_shape=[jax.ShapeDtypeStruct((B, SEQ, D), F32),
                   jax.ShapeDtypeStruct((B, SEQ, D), F32),
                   jax.ShapeDtypeStruct((B, N_EXPERTS, SEQ), F32)],
        compiler_params=_params("parallel", "arbitrary"),
    )(o_f, o_b, gate, y_na, x, gt1, sh2, sc2, gdn_norm, norm_ffn, wog, won, wrt)


TOK_BLOCKS = SEQ // LANES
ROUTE_ROWS = N_EXPERTS * TOK_BLOCKS


def _topk_kernel(aff_ref, idx_ref, gval_ref):
    a = aff_ref[...]
    bf = lambda t: t.astype(BF16)
    mask = lambda c: jnp.where(c, 1.0, 0.0).astype(BF16)
    li = lax.broadcasted_iota(jnp.int32, (LANES, LANES), 0)
    lj = lax.broadcasted_iota(jnp.int32, (LANES, LANES), 1)
    ones = jnp.ones((LANES, LANES), BF16)
    upper = mask(li <= lj)
    ri = lax.broadcasted_iota(jnp.int32, (ROUTE_ROWS, ROUTE_ROWS), 0)
    rj = lax.broadcasted_iota(jnp.int32, (ROUTE_ROWS, ROUTE_ROWS), 1)
    shift = TOK_BLOCKS.bit_length() - 1
    same = (ri >> shift) == (rj >> shift)
    expert_sum = mask(same)
    rows_before = mask(jnp.logical_and(same, rj < ri))
    per_expert = lambda m: _dot(expert_sum, bf(_dot(m, ones)))

    def bisect(it, prefix):
        cand = prefix | (jnp.int32(1) << (30 - it))
        cnt = per_expert(mask(a >= pltpu.bitcast(cand, F32)))
        return jnp.where(cnt >= CAPACITY, cand, prefix)

    thr = lax.fori_loop(0, 31, bisect, jnp.zeros((ROUTE_ROWS, LANES), jnp.int32))
    gt = a >= pltpu.bitcast(thr + 1, F32)
    eq = jnp.logical_and(a >= pltpu.bitcast(thr, F32), jnp.logical_not(gt))
    need = CAPACITY - per_expert(mask(gt))

    def prefix(m):
        return _dot(m, upper), _dot(rows_before, bf(_dot(m, ones)))

    eq_m = mask(eq)
    eq_in_row, eq_rows_before = prefix(eq_m)
    eq_before = eq_in_row + eq_rows_before - eq_m.astype(F32)
    sel = jnp.logical_or(gt, jnp.logical_and(eq, eq_before < need))
    sel_m = mask(sel)
    rank_in_row, start = prefix(sel_m)
    local = jnp.where(sel, rank_in_row, 0.0)
    row_total = _dot(sel_m, ones)
    lane_f = lax.broadcasted_iota(jnp.int32, (CAPACITY, LANES), 1).astype(F32)
    slot = lax.broadcasted_iota(jnp.int32, (CAPACITY, LANES), 0).astype(F32)
    tok0 = (lax.broadcasted_iota(jnp.int32, (TOK_BLOCKS, LANES), 0) * LANES).astype(F32)
    tok0_hi, tok0_lo = _split2(tok0)
    ones_cl = jnp.ones((CAPACITY, LANES), BF16)
    pad = lambda t: jnp.concatenate([t, jnp.zeros((LANES - TOK_BLOCKS, LANES), t.dtype)], axis=0)

    def as_row(v):
        hi, lo = _split2(jnp.where(li == lj, pad(v), 0.0))
        return _dot(ones_cl, hi) + _dot(ones_cl, lo)

    for e in range(N_EXPERTS):
        rows = slice(e * TOK_BLOCKS, (e + 1) * TOK_BLOCKS)
        st, tot = start[rows], row_total[rows]
        st_r, tot_r = as_row(st), as_row(tot)
        owner = mask(jnp.logical_and(st_r <= slot, slot < st_r + tot_r))
        pick = lambda t: _dot(owner, pad(t))
        st_hi, st_lo = _split2(st)
        a_hi, a_mid, a_lo = _split3(a[rows])
        want = slot - (pick(st_hi) + pick(st_lo)) + 1.0
        hit = pick(bf(local[rows])) == want
        tok = jnp.sum(jnp.where(hit, lane_f, 0.0), axis=-1, keepdims=True)
        base = pick(tok0_hi) + pick(tok0_lo)
        val = pick(a_hi) + (pick(a_mid) + pick(a_lo))
        idx_ref[e] = (tok + base[:, 0:1]).astype(jnp.int32)
        gval_ref[e] = jnp.sum(jnp.where(hit, val, 0.0), axis=-1, keepdims=True)


def _route(aff_t):
    B = aff_t.shape[0]
    out = lambda dt: jax.ShapeDtypeStruct((B, N_EXPERTS, CAPACITY, 1), dt)
    spec = pl.BlockSpec((None, N_EXPERTS, CAPACITY, 1), lambda b: (b, 0, 0, 0))
    return pl.pallas_call(
        _topk_kernel,
        grid=(B,),
        in_specs=[pl.BlockSpec((None, ROUTE_ROWS, LANES), lambda b: (b, 0, 0))],
        out_specs=[spec, spec],
        out_shape=[out(jnp.int32), out(F32)],
        compiler_params=_params("parallel"),
    )(aff_t.reshape(B, ROUTE_ROWS, LANES))


GATHER_UNROLL = 8


def _dispatch_kernel(idx_ref, h_ref, xe_ref, rows_ref):
    base = (pl.program_id(0) * N_EXPERTS + pl.program_id(1)) * CAPACITY

    def body(r, _):
        t = idx_ref[base + r]
        rows_ref[pl.ds(r, 1), :] = h_ref[pl.ds(t, 1), :]
        return 0

    lax.fori_loop(0, CAPACITY, body, 0, unroll=GATHER_UNROLL)
    xe_ref[...] = rows_ref[...].astype(BF16)


def _dispatch(idx_flat, h2):
    B = h2.shape[0]
    D = D_MODEL
    return pl.pallas_call(
        _dispatch_kernel,
        grid_spec=pltpu.PrefetchScalarGridSpec(
            num_scalar_prefetch=1,
            grid=(B, N_EXPERTS),
            in_specs=[pl.BlockSpec((None, SEQ, D), lambda b, e, idx: (b, 0, 0))],
            out_specs=pl.BlockSpec((None, None, CAPACITY, D), lambda b, e, idx: (b, e, 0, 0)),
            scratch_shapes=[pltpu.VMEM((CAPACITY, D), F32)]),
        out_shape=jax.ShapeDtypeStruct((B, N_EXPERTS, CAPACITY, D), BF16),
        compiler_params=_params("parallel", "arbitrary"),
    )(idx_flat, h2)


def _ffn_kernel(xe_ref, gv_ref, wg_ref, wu_ref, wd_ref, ye_ref, wgb_ref, wub_ref, wdb_ref):
    @pl.when(pl.program_id(1) == 0)
    def _():
        wgb_ref[...] = wg_ref[...].astype(BF16)
        wub_ref[...] = wu_ref[...].astype(BF16)
        wdb_ref[...] = wd_ref[...].astype(BF16)

    xe = xe_ref[...]
    hid = (_silu(_dot(xe, wgb_ref[...])) * _dot(xe, wub_ref[...])).astype(BF16)
    ye_ref[...] = _dot(hid, wdb_ref[...]) * gv_ref[...]


def _expert_ffn(xe, gval, w_gate, w_up, w_down):
    B = xe.shape[0]
    D = D_MODEL
    F = w_gate.shape[-1]
    return pl.pallas_call(
        _ffn_kernel,
        grid=(N_EXPERTS, B),
        in_specs=[pl.BlockSpec((None, None, CAPACITY, D), lambda e, b: (b, e, 0, 0)),
                  pl.BlockSpec((None, None, CAPACITY, 1), lambda e, b: (b, e, 0, 0)),
                  pl.BlockSpec((None, D, F), lambda e, b: (e, 0, 0)),
                  pl.BlockSpec((None, D, F), lambda e, b: (e, 0, 0)),
                  pl.BlockSpec((None, F, D), lambda e, b: (e, 0, 0))],
        out_specs=pl.BlockSpec((None, None, CAPACITY, D), lambda e, b: (b, e, 0, 0)),
        out_shape=jax.ShapeDtypeStruct((B, N_EXPERTS, CAPACITY, D), F32),
        scratch_shapes=[pltpu.VMEM((D, F), BF16), pltpu.VMEM((D, F), BF16), pltpu.VMEM((F, D), BF16)],
        compiler_params=_params("arbitrary", "arbitrary"),
    )(xe, gval, w_gate, w_up, w_down)


def _combine_kernel(idx_ref, ye_ref, o_ref):
    e = pl.program_id(1)

    @pl.when(e == 0)
    def _():
        o_ref[...] = jnp.zeros_like(o_ref)

    base = (pl.program_id(0) * N_EXPERTS + e) * CAPACITY

    def body(r, _):
        t = idx_ref[base + r]
        o_ref[pl.ds(t, 1), :] = o_ref[pl.ds(t, 1), :] + ye_ref[pl.ds(r, 1), :]
        return 0

    lax.fori_loop(0, CAPACITY, body, 0, unroll=GATHER_UNROLL)


def _combine(idx_flat, ye):
    B = ye.shape[0]
    D = D_MODEL
    return pl.pallas_call(
        _combine_kernel,
        grid_spec=pltpu.PrefetchScalarGridSpec(
            num_scalar_prefetch=1,
            grid=(B, N_EXPERTS),
            in_specs=[pl.BlockSpec((None, None, CAPACITY, D), lambda b, e, idx: (b, e, 0, 0))],
            out_specs=pl.BlockSpec((None, SEQ, D), lambda b, e, idx: (b, 0, 0))),
        out_shape=jax.ShapeDtypeStruct((B, SEQ, D), F32),
        compiler_params=_params("parallel", "arbitrary"),
    )(idx_flat, ye)


def _final_kernel(x1_ref, moe_ref, gt2_ref, fw_ref, o_ref):
    x2 = x1_ref[...] + gt2_ref[...] * moe_ref[...]
    o_ref[...] = x2 * lax.rsqrt(jnp.mean(x2 * x2, axis=-1, keepdims=True) + EPS) * fw_ref[...]


def _final(x1, moe, gt2, final_norm):
    B = x1.shape[0]
    D = D_MODEL
    tok = pl.BlockSpec((None, TILE, D), lambda b, i: (b, i, 0))
    return pl.pallas_call(
        _final_kernel,
        grid=(B, SEQ // TILE),
        in_specs=[tok, tok, pl.BlockSpec((None, 1, D), lambda b, i: (b, 0, 0)),
                  pl.BlockSpec((1, D), lambda b, i: (0, 0))],
        out_specs=tok,
        out_shape=jax.ShapeDtypeStruct((B, SEQ, D), F32),
        compiler_params=_params("parallel", "arbitrary"),
    )(x1, moe, gt2, final_norm)


def kernel(x, c, ctx, c_ctx, w_mod, b_mod, norm_mix, norm_ffn, w_in, conv_qkv, a_log, dt_bias, gdn_norm, na_rpb,
           w_out, w_router, w_gate, w_up, w_down, final_norm):
    B, T, D = x.shape
    assert (T, D) == (SEQ, D_MODEL) and ctx.shape == (B, CTX_LEN, D) and w_mod.shape[0] == 1
    li = 0
    mod_rows = -(-(B + 1) // SUBLANES) * SUBLANES
    cc = jnp.zeros((mod_rows, D), F32).at[:B].set(c).at[B].set(c_ctx)
    mod = _modulation(cc, w_mod[li], b_mod[li])
    part = lambda j, rows: mod[:rows, j * D:(j + 1) * D].reshape(rows, 1, D)
    sh1, sc1 = part(0, B + 1), part(1, B + 1)
    gt1, sh2, sc2, gt2 = part(2, B), part(3, B), part(4, B), part(5, B)

    wi = w_in[li]
    q_end, g_end = 3 * GDN_WIDTH, 4 * GDN_WIDTH
    ab_end = g_end + 4 * GDN_HEADS
    wq = wi[:, :q_end].astype(BF16)
    wg = wi[:, q_end:g_end].astype(BF16)
    wab = jnp.zeros((D, LANES), F32).at[:, :4 * GDN_HEADS].set(wi[:, g_end:ab_end]).astype(BF16)
    wnq = wi[:, ab_end:ab_end + NA_WIDTH].astype(BF16)
    wnkv = wi[:, ab_end + NA_WIDTH:].astype(BF16)
    gpar = jnp.zeros((2, LANES), F32)
    gpar = gpar.at[0, :2 * GDN_HEADS].set(-jnp.exp(a_log[li].astype(F32)).reshape(-1))
    gpar = gpar.at[1, :2 * GDN_HEADS].set(dt_bias[li].astype(F32).reshape(-1))

    pq, gate, gb, naq, nakv = _in_projection(x, ctx, sh1, sc1, norm_mix[li].reshape(1, D),
                                             wq, wg, wab, wnq, wnkv, gpar)
    cos_tab, sin_tab = _rope_tables()
    qkv = _gdn_prep(pq, conv_qkv[li], cos_tab, sin_tab)
    o_f, o_b = _gdn_scan(qkv, gb)
    y_na = _neighbourhood_attention(naq, nakv, _na_bias_table(na_rpb[li]))

    wo = w_out[li].astype(BF16)
    x1, h2, aff_t = _out_projection(o_f, o_b, gate, y_na, x, gt1, sh2, sc2,
                                     gdn_norm[li].reshape(1, GDN_HEAD_DIM), norm_ffn[li].reshape(1, D),
                                     wo[:GDN_WIDTH], wo[GDN_WIDTH:], w_router[li].T)
    idx, gval = _route(aff_t)
    idx_flat = idx.reshape(-1)
    xe = _dispatch(idx_flat, h2)
    ye = _expert_ffn(xe, gval, w_gate[li], w_up[li], w_down[li])
    moe = _combine(idx_flat, ye)
    return _final(x1, moe, gt2, final_norm.reshape(1, D))
```

```python
import functools
import math

import numpy as np
import jax
import jax.numpy as jnp
from jax import lax
from jax.experimental import pallas as pl
from jax.experimental.pallas import tpu as pltpu

F32 = jnp.float32
BF16 = jnp.bfloat16

D_MODEL = 1024
SEQ = 4096
CTX_LEN = 256
GRID_W = 64
GRID_ROWS = SEQ // GRID_W
GDN_HEADS = 4
GDN_HEAD_DIM = 128
GDN_WIDTH = GDN_HEADS * GDN_HEAD_DIM
CONV_W = 5
CHUNK = 64
ROPE_BASE = 10000.0
NA_HEADS = 8
NA_HEAD_DIM = 64
NA_WIDTH = NA_HEADS * NA_HEAD_DIM
NA_ROWS = 8
NA_COLS = 16
N_EXPERTS = 16
CAPACITY = 2 * SEQ // N_EXPERTS
EPS = 1e-6

TILE = 256
T_ALL = CTX_LEN + SEQ
N_TILES = T_ALL // TILE
N_CHUNKS = T_ALL // CHUNK
CTX_CHUNKS = CTX_LEN // CHUNK
LANES = 128
SUBLANES = 8
VMEM_LIMIT = 56 * 1024 * 1024


def _params(*sem):
    return pltpu.CompilerParams(dimension_semantics=sem, vmem_limit_bytes=VMEM_LIMIT)


def _dot(a, b):
    return jnp.dot(a, b, preferred_element_type=F32)


def _dot_nt(a, b):
    return lax.dot_general(a, b, (((1,), (1,)), ((), ())), preferred_element_type=F32)


def _dot_tn(a, b):
    return lax.dot_general(a, b, (((0,), (0,)), ((), ())), preferred_element_type=F32)


def _split2(a):
    hi = a.astype(BF16)
    lo = (a - hi.astype(F32)).astype(BF16)
    return hi, lo


def _split3(a):
    hi = a.astype(BF16)
    r = a - hi.astype(F32)
    mid = r.astype(BF16)
    lo = (r - mid.astype(F32)).astype(BF16)
    return hi, mid, lo


def _dot3(a, b):
    ah, al = _split2(a)
    bh, bl = _split2(b)
    return _dot(ah, bh) + (_dot(ah, bl) + _dot(al, bh))


def _silu(x):
    return x * jax.nn.sigmoid(x)


def _mod_kernel(c_ref, w_ref, b_ref, o_ref):
    s = _silu(c_ref[...])
    o_ref[...] = _dot3(s, w_ref[...]) + b_ref[...]


def _modulation(cc, w_mod, b_mod):
    rows, d = cc.shape
    n = w_mod.shape[1]
    bn = 1024
    return pl.pallas_call(
        _mod_kernel,
        grid=(n // bn,),
        in_specs=[pl.BlockSpec((rows, d), lambda j: (0, 0)),
                  pl.BlockSpec((d, bn), lambda j: (0, j)),
                  pl.BlockSpec((1, bn), lambda j: (0, j))],
        out_specs=pl.BlockSpec((rows, bn), lambda j: (0, j)),
        out_shape=jax.ShapeDtypeStruct((rows, n), F32),
        compiler_params=_params("parallel"),
    )(cc, w_mod, b_mod.reshape(1, n))


def _inproj_kernel(x_ref, ctx_ref, sh_ref, sc_ref, nw_ref, wq_ref, wg_ref, wab_ref, wnq_ref, wnkv_ref,
                   gpar_ref, pq_ref, gate_ref, gb_ref, naq_ref, nakv_ref):
    i = pl.program_id(1)
    xt = jnp.where(i == 0, ctx_ref[...], x_ref[...])
    ms = jnp.mean(xt * xt, axis=-1, keepdims=True)
    h = xt * lax.rsqrt(ms + EPS) * nw_ref[...]
    h = h * (1.0 + sc_ref[...]) + sh_ref[...]
    hb = h.astype(BF16)
    pq_ref[...] = _dot(hb, wq_ref[...])
    gate_ref[...] = _dot(hb, wg_ref[...])
    ab = _dot(hb, wab_ref[...])
    z = ab + gpar_ref[1:2, :]
    softplus = jnp.maximum(z, 0.0) + jnp.log1p(jnp.exp(-jnp.abs(z)))
    lane = lax.broadcasted_iota(jnp.int32, ab.shape, 1)
    gb_ref[...] = jnp.where(lane < 2 * GDN_HEADS, gpar_ref[0:1, :] * softplus, jax.nn.sigmoid(ab))
    naq_ref[...] = (_dot(hb, wnq_ref[...]) * (NA_HEAD_DIM ** -0.5)).astype(BF16)
    nakv_ref[...] = _dot(hb, wnkv_ref[...]).astype(BF16)


def _in_projection(x, ctx, sh, sc, norm_w, wq, wg, wab, wnq, wnkv, gpar):
    B = x.shape[0]
    D = D_MODEL
    tok = lambda n: pl.BlockSpec((None, TILE, n), lambda b, i: (b, i, 0))
    full = lambda a: pl.BlockSpec(a.shape, lambda b, i: (0,) * a.ndim)
    modrow = pl.BlockSpec((None, 1, D), lambda b, i: (jnp.where(i == 0, B, b), 0, 0))
    outs = [(3 * GDN_WIDTH, F32), (GDN_WIDTH, F32), (LANES, F32), (NA_WIDTH, BF16), (2 * NA_WIDTH, BF16)]
    return pl.pallas_call(
        _inproj_kernel,
        grid=(B, N_TILES),
        in_specs=[pl.BlockSpec((None, TILE, D), lambda b, i: (b, jnp.maximum(i - 1, 0), 0)),
                  pl.BlockSpec((None, TILE, D), lambda b, i: (b, 0, 0)),
                  modrow, modrow, full(norm_w), full(wq), full(wg), full(wab), full(wnq), full(wnkv), full(gpar)],
        out_specs=[tok(n) for n, _ in outs],
        out_shape=[jax.ShapeDtypeStruct((B, T_ALL, n), dt) for n, dt in outs],
        compiler_params=_params("parallel", "arbitrary"),
    )(x, ctx, sh, sc, norm_w, wq, wg, wab, wnq, wnkv, gpar)


HALO = SUBLANES


def _gdn_prep_kernel(cur_ref, prev_ref, next_ref, cw_ref, cos_ref, sin_ref, out_ref, ext_ref):
    i = pl.program_id(1)
    has_prev = i >= 2
    has_next = jnp.logical_and(i >= 1, i < N_TILES - 1)
    ext_ref[0:HALO, :] = jnp.where(has_prev, prev_ref[...], 0.0)
    ext_ref[HALO:HALO + TILE, :] = cur_ref[...]
    ext_ref[HALO + TILE:, :] = jnp.where(has_next, next_ref[...], 0.0)
    first = HALO - CONV_W // 2
    acc = ext_ref[first:first + TILE, :] * cw_ref[0:1, :]
    for k in range(1, CONV_W):
        acc = acc + ext_ref[first + k:first + k + TILE, :] * cw_ref[k:k + 1, :]
    y = _silu(acc)
    cos = cos_ref[...]
    sin = sin_ref[...]
    lane = lax.broadcasted_iota(jnp.int32, (TILE, LANES), 1)
    take_upper = (lane % (GDN_HEAD_DIM // 2)) < (GDN_HEAD_DIM // 4)
    for j in range(2 * GDN_HEADS):
        t = y[:, j * LANES:(j + 1) * LANES]
        t = t * lax.rsqrt(jnp.sum(t * t, axis=-1, keepdims=True) + EPS)
        partner = jnp.where(take_upper, pltpu.roll(t, LANES - GDN_HEAD_DIM // 4, 1),
                            pltpu.roll(t, GDN_HEAD_DIM // 4, 1))
        t = t * cos + partner * sin
        if j < GDN_HEADS:
            t = t * (GDN_HEAD_DIM ** -0.5)
        out_ref[:, j * LANES:(j + 1) * LANES] = t
    out_ref[:, 2 * GDN_WIDTH:] = y[:, 2 * GDN_WIDTH:]


def _gdn_prep(pq, conv_w, cos_tab, sin_tab):
    B = pq.shape[0]
    W = 3 * GDN_WIDTH
    per = TILE // HALO
    return pl.pallas_call(
        _gdn_prep_kernel,
        grid=(B, N_TILES),
        in_specs=[pl.BlockSpec((None, TILE, W), lambda b, i: (b, i, 0)),
                  pl.BlockSpec((None, HALO, W), lambda b, i: (b, jnp.maximum(i * per - 1, 0), 0)),
                  pl.BlockSpec((None, HALO, W), lambda b, i: (b, jnp.minimum((i + 1) * per, T_ALL // HALO - 1), 0)),
                  pl.BlockSpec((CONV_W, W), lambda b, i: (0, 0)),
                  pl.BlockSpec((TILE, LANES), lambda b, i: (i, 0)),
                  pl.BlockSpec((TILE, LANES), lambda b, i: (i, 0))],
        out_specs=pl.BlockSpec((None, TILE, W), lambda b, i: (b, i, 0)),
        out_shape=jax.ShapeDtypeStruct((B, T_ALL, W), F32),
        scratch_shapes=[pltpu.VMEM((TILE + 2 * HALO, W), F32)],
        compiler_params=_params("parallel", "arbitrary"),
    )(pq, pq, pq, conv_w, cos_tab, sin_tab)


def _rope_tables():
    half = GDN_HEAD_DIM // 2
    pairs = half // 2
    t = np.arange(SEQ)
    inv_freq = jnp.asarray(ROPE_BASE, F32) ** (-jnp.arange(pairs, dtype=F32) / pairs)

    def tab(pos):
        ang = jnp.asarray(pos, F32)[:, None] * inv_freq[None, :]
        c, s = jnp.cos(ang), jnp.sin(ang)
        return jnp.concatenate([c, c], axis=-1), jnp.concatenate([-s, s], axis=-1)

    cr, sr = tab(t // GRID_W)
    cc, sc = tab(t % GRID_W)
    cos = jnp.concatenate([cr, cc], axis=-1)
    sin = jnp.concatenate([sr, sc], axis=-1)
    cos = jnp.concatenate([jnp.ones((CTX_LEN, LANES), F32), cos], axis=0)
    sin = jnp.concatenate([jnp.zeros((CTX_LEN, LANES), F32), sin], axis=0)
    return cos, sin


SCAN_SAMPLES = 4


def _gdn_scan_kernel(qf_ref, qb_ref, gf_ref, gb_ref, of_ref, ob_ref, s_ref):
    n = pl.program_id(1)

    @pl.when(n == 0)
    def _():
        s_ref[...] = jnp.zeros_like(s_ref)

    row = lax.broadcasted_iota(jnp.int32, (CHUNK, CHUNK), 0)
    col = lax.broadcasted_iota(jnp.int32, (CHUNK, CHUNK), 1)
    eye = jnp.where(row == col, 1.0, 0.0).astype(F32)
    chains = range(SCAN_SAMPLES * 2 * GDN_HEADS)
    bf = lambda t: t.astype(BF16)
    each = lambda f, *cols: [f(*args) for args in zip(*cols)]
    q, k, v, gcol, grow, gtot, bcol, incl, strict = [], [], [], [], [], [], [], [], []
    for s in range(SCAN_SAMPLES):
        for d in range(2):
            src = qf_ref if d == 0 else qb_ref
            gbt = (gf_ref if d == 0 else gb_ref)[s]
            inc = (row >= col) if d == 0 else (row <= col)
            cm = jnp.where(inc, 1.0, 0.0).astype(BF16)
            g1, g2, g3 = _split3(gbt)
            gc = _dot(cm, g1) + (_dot(cm, g2) + _dot(cm, g3))
            gct = gc.T
            tot = gc[CHUNK - 1:CHUNK, :] if d == 0 else gc[0:1, :]
            for h in range(GDN_HEADS):
                c = d * GDN_HEADS + h
                gcol.append(gc[:, c:c + 1])
                grow.append(gct[c:c + 1, :])
                gtot.append(tot[:, c:c + 1])
                bcol.append(gbt[:, 2 * GDN_HEADS + c:2 * GDN_HEADS + c + 1])
                q.append(src[s, :, h * LANES:(h + 1) * LANES])
                k.append(src[s, :, GDN_WIDTH + h * LANES:GDN_WIDTH + (h + 1) * LANES])
                v.append(src[s, :, 2 * GDN_WIDTH + h * LANES:2 * GDN_WIDTH + (h + 1) * LANES])
                incl.append(inc)
                strict.append((row > col) if d == 0 else (row < col))
    s_old = [s_ref[c] for c in chains]
    decay = each(lambda m, gc_, gr_: jnp.exp(jnp.where(m, gc_ - gr_, -jnp.inf)), incl, gcol, grow)
    eg = each(jnp.exp, gcol)
    kb = each(lambda k_, b_: k_ * b_, k, bcol)
    a = each(lambda kb_, q_, k_: _dot_nt(bf(jnp.concatenate([kb_, q_], axis=0)), bf(k_)), kb, q, k)
    lower = each(lambda m, a_, dc: jnp.where(m, a_[:CHUNK] * dc, 0.0), strict, a, decay)
    attn = each(lambda a_, dc: bf(a_[CHUNK:] * dc), a, decay)
    tinv = each(lambda l_: eye - l_, lower)
    m = lower
    for _ in range(int(math.log2(CHUNK)) - 1):
        m = each(lambda m_: _dot(bf(m_), bf(m_)), m)
        tinv = each(lambda t_, m_: t_ + _dot(bf(t_), bf(m_)), tinv, m)
    resid = each(lambda t_, l_: (eye - t_) - _dot3(l_, t_), tinv, lower)
    tinv = each(lambda t_, r_: t_ + _dot(bf(t_), bf(r_)), tinv, resid)
    uw = each(lambda t_, v_, b_, kb_, eg_: _dot(bf(t_), bf(jnp.concatenate([v_ * b_, kb_ * eg_], axis=1))),
              tinv, v, bcol, kb, eg)
    ws = each(lambda uw_, q_, eg_, s_: _dot(bf(jnp.concatenate([uw_[:, LANES:], q_ * eg_], axis=0)), bf(s_)),
              uw, q, eg, s_old)
    vb = each(lambda uw_, ws_: bf(uw_[:, :LANES] - ws_[:CHUNK]), uw, ws)
    o = each(lambda ws_, at_, vb_: ws_[CHUNK:] + _dot(at_, vb_), ws, attn, vb)
    s_new = each(lambda s_, gt_, k_, gc_, vb_: s_ * jnp.exp(gt_) + _dot_tn(bf(k_ * jnp.exp(gt_ - gc_)), vb_),
                 s_old, gtot, k, gcol, vb)
    for c in chains:
        s, d, h = c // (2 * GDN_HEADS), (c // GDN_HEADS) % 2, c % GDN_HEADS
        o_ref = of_ref if d == 0 else ob_ref
        o_ref[s, :, h * LANES:(h + 1) * LANES] = o[c]
        s_ref[c] = s_new[c]


def _gdn_scan(qkv, gb):
    B = qkv.shape[0]
    S = SCAN_SAMPLES
    assert B % S == 0
    fwd = lambda b, n: (b, n, 0)
    bwd = lambda b, n: (b, jnp.where(n < CTX_CHUNKS, CTX_CHUNKS - 1 - n, N_CHUNKS + CTX_CHUNKS - 1 - n), 0)
    out = jax.ShapeDtypeStruct((B, T_ALL, GDN_WIDTH), F32)
    return pl.pallas_call(
        _gdn_scan_kernel,
        grid=(B // S, N_CHUNKS),
        in_specs=[pl.BlockSpec((S, CHUNK, 3 * GDN_WIDTH), fwd),
                  pl.BlockSpec((S, CHUNK, 3 * GDN_WIDTH), bwd),
                  pl.BlockSpec((S, CHUNK, LANES), fwd),
                  pl.BlockSpec((S, CHUNK, LANES), bwd)],
        out_specs=[pl.BlockSpec((S, CHUNK, GDN_WIDTH), fwd),
                   pl.BlockSpec((S, CHUNK, GDN_WIDTH), bwd)],
        out_shape=[out, out],
        scratch_shapes=[pltpu.VMEM((S * 2 * GDN_HEADS, GDN_HEAD_DIM, GDN_HEAD_DIM), F32)],
        compiler_params=_params("parallel", "arbitrary"),
    )(qkv, qkv, gb, gb)


WIN_TOKENS = NA_ROWS * GRID_W
NA_GROUP = 8


def _na_kernel(q_ref, kv_ref, bias_ref, o_ref):
    r = pl.program_id(1)
    start = jnp.clip(r - NA_ROWS // 2, 0, GRID_ROWS - NA_ROWS)
    off = pl.multiple_of(CTX_LEN + start * GRID_W, GRID_W)
    lane = lax.broadcasted_iota(jnp.int32, (GRID_W, LANES), 1)
    low = lane < NA_HEAD_DIM
    zero = jnp.zeros((GRID_W, LANES), BF16)
    each = lambda f, *cols: [f(*args) for args in zip(*cols)]
    rowmax = lambda t: jnp.max(t, axis=-1, keepdims=True)
    rowsum = lambda t: jnp.sum(t, axis=-1, keepdims=True)
    for g in range(NA_HEADS // NA_GROUP):
        pairs = range(g * NA_GROUP // 2, (g + 1) * NA_GROUP // 2)
        klanes = [slice(p * LANES, (p + 1) * LANES) for p in pairs]
        vlanes = [slice(NA_WIDTH + s.start, NA_WIDTH + s.stop) for s in klanes]
        q2 = each(lambda s: jnp.concatenate([jnp.where(low, q_ref[:, s], zero),
                                             jnp.where(low, zero, q_ref[:, s])], axis=0), klanes)
        bias = each(lambda p: jnp.concatenate([bias_ref[2 * p], bias_ref[2 * p + 1]], axis=0), pairs)
        s_lat = each(lambda q_, s, b_: _dot_nt(q_, kv_ref[pl.ds(off, WIN_TOKENS), s]) + b_, q2, klanes, bias)
        s_ctx = each(lambda q_, s: _dot_nt(q_, kv_ref[0:CTX_LEN, s]), q2, klanes)
        mx = each(lambda a, b: jnp.maximum(rowmax(a), rowmax(b)), s_lat, s_ctx)
        p_lat = each(lambda a, m: jnp.exp(a - m), s_lat, mx)
        p_ctx = each(lambda a, m: jnp.exp(a - m), s_ctx, mx)
        den = each(lambda a, b: rowsum(a) + rowsum(b), p_lat, p_ctx)
        o = each(lambda a, b, s: _dot(a.astype(BF16), kv_ref[pl.ds(off, WIN_TOKENS), s])
                 + _dot(b.astype(BF16), kv_ref[0:CTX_LEN, s]), p_lat, p_ctx, vlanes)
        o = each(lambda o_, d_: o_ / d_, o, den)
        for s, o_ in zip(klanes, o):
            o_ref[:, s] = jnp.where(low, o_[:GRID_W], o_[GRID_W:]).astype(BF16)


def _neighbourhood_attention(naq, nakv, bias_tab):
    B = naq.shape[0]
    q_blocks_before = CTX_LEN // GRID_W

    def bias_map(b, r):
        start = jnp.clip(r - NA_ROWS // 2, 0, GRID_ROWS - NA_ROWS)
        return (r - start, 0, 0, 0)

    return pl.pallas_call(
        _na_kernel,
        grid=(B, GRID_ROWS),
        in_specs=[pl.BlockSpec((None, GRID_W, NA_WIDTH), lambda b, r: (b, r + q_blocks_before, 0)),
                  pl.BlockSpec((None, T_ALL, 2 * NA_WIDTH), lambda b, r: (b, 0, 0)),
                  pl.BlockSpec((None, NA_HEADS, GRID_W, WIN_TOKENS), bias_map)],
        out_specs=pl.BlockSpec((None, GRID_W, NA_WIDTH), lambda b, r: (b, r, 0)),
        out_shape=jax.ShapeDtypeStruct((B, SEQ, NA_WIDTH), BF16),
        compiler_params=_params("parallel", "arbitrary"),
    )(naq, nakv, bias_tab)


def _na_bias_table(rpb):
    col = np.arange(GRID_W)
    col_start = np.clip(col - NA_COLS // 2, 0, GRID_W - NA_COLS)
    col_in = (col[None, :] >= col_start[:, None]) & (col[None, :] < col_start[:, None] + NA_COLS)
    col_idx = np.clip(col[None, :] - col[:, None], -(NA_COLS - 1), NA_COLS - 1) + (NA_COLS - 1)
    row_idx = np.arange(NA_ROWS)[None, :] - np.arange(NA_ROWS)[:, None] + (NA_ROWS - 1)
    row_sel = (row_idx[..., None] == np.arange(2 * NA_ROWS - 1)).astype(np.float32)
    col_sel = (col_idx[..., None] == np.arange(2 * NA_COLS - 1)).astype(np.float32)
    t = jnp.einsum('hrc,vjr,qkc->vhqjk', rpb.astype(F32), row_sel, col_sel, precision=lax.Precision.HIGHEST)
    t = jnp.where(col_in[None, None, :, None, :], t, -jnp.inf)
    return t.reshape(NA_ROWS, NA_HEADS, GRID_W, WIN_TOKENS)


def _outproj_kernel(of_ref, ob_ref, gate_ref, yna_ref, x_ref, gt1_ref, sh2_ref, sc2_ref, gn_ref, nf_ref,
                    wog_ref, won_ref, wr_ref, x1_ref, h2_ref, aff_ref):
    o = of_ref[...] + ob_ref[...]
    ys = []
    for h in range(GDN_HEADS):
        lanes = slice(h * LANES, (h + 1) * LANES)
        oh = o[:, lanes]
        yh = oh * lax.rsqrt(jnp.mean(oh * oh, axis=-1, keepdims=True) + EPS) * gn_ref[...]
        ys.append(yh * _silu(gate_ref[:, lanes]))
    yg = jnp.concatenate(ys, axis=1).astype(BF16)
    y = _dot(yg, wog_ref[...]) + _dot(yna_ref[...], won_ref[...])
    x1 = x_ref[...] + gt1_ref[...] * y
    x1_ref[...] = x1
    h2 = x1 * lax.rsqrt(jnp.mean(x1 * x1, axis=-1, keepdims=True) + EPS) * nf_ref[...]
    h2 = h2 * (1.0 + sc2_ref[...]) + sh2_ref[...]
    h2_ref[...] = h2
    hb = h2.astype(BF16)
    h_lo = (h2 - hb.astype(F32)).astype(BF16)
    wr_hi, wr_lo = _split2(wr_ref[...])
    logits = _dot_nt(wr_hi, hb) + (_dot_nt(wr_hi, h_lo) + _dot_nt(wr_lo, hb))
    e = jnp.exp(logits - jnp.max(logits, axis=0, keepdims=True))
    aff_ref[...] = e / jnp.sum(e, axis=0, keepdims=True)


def _out_projection(o_f, o_b, gate, y_na, x, gt1, sh2, sc2, gdn_norm, norm_ffn, wog, won, wrt):
    B = x.shape[0]
    D = D_MODEL
    lat = lambda n: pl.BlockSpec((None, TILE, n), lambda b, i: (b, i + 1, 0))
    tok = lambda n: pl.BlockSpec((None, TILE, n), lambda b, i: (b, i, 0))
    full = lambda a: pl.BlockSpec(a.shape, lambda b, i: (0,) * a.ndim)
    modrow = pl.BlockSpec((None, 1, D), lambda b, i: (b, 0, 0))
    return pl.pallas_call(
        _outproj_kernel,
        grid=(B, SEQ // TILE),
        in_specs=[lat(GDN_WIDTH), lat(GDN_WIDTH), lat(GDN_WIDTH), tok(NA_WIDTH), tok(D),
                  modrow, modrow, modrow, full(gdn_norm), full(norm_ffn), full(wog), full(won), full(wrt)],
        out_specs=[tok(D), tok(D),
                   pl.BlockSpec((None, N_EXPERTS, TILE), lambda b, i: (b, 0, i))],
        out_shape=[jax.ShapeDtypeStruct((B, SEQ, D), F32),
                   jax.ShapeDtypeStruct((B, SEQ, D), F32),
                   jax.ShapeDtypeStruct((B, N_EXPERTS, SEQ), F32)],
        compiler_params=_params("parallel", "arbitrary"),
    )(o_f, o_b, gate, y_na, x, gt1, sh2, sc2, gdn_norm, norm_ffn, wog, won, wrt)


TOK_BLOCKS = SEQ // LANES
ROUTE_ROWS = N_EXPERTS * TOK_BLOCKS


def _topk_kernel(aff_ref, idx_ref, gval_ref):
    a = aff_ref[...]
    bf = lambda t: t.astype(BF16)
    mask = lambda c: jnp.where(c, 1.0, 0.0).astype(BF16)
    li = lax.broadcasted_iota(jnp.int32, (LANES, LANES), 0)
    lj = lax.broadcasted_iota(jnp.int32, (LANES, LANES), 1)
    ones = jnp.ones((LANES, LANES), BF16)
    upper = mask(li <= lj)
    ri = lax.broadcasted_iota(jnp.int32, (ROUTE_ROWS, ROUTE_ROWS), 0)
    rj = lax.broadcasted_iota(jnp.int32, (ROUTE_ROWS, ROUTE_ROWS), 1)
    shift = TOK_BLOCKS.bit_length() - 1
    same = (ri >> shift) == (rj >> shift)
    expert_sum = mask(same)
    rows_before = mask(jnp.logical_and(same, rj < ri))
    per_expert = lambda m: _dot(expert_sum, bf(_dot(m, ones)))

    def bisect(it, prefix):
        cand = prefix | (jnp.int32(1) << (30 - it))
        cnt = per_expert(mask(a >= pltpu.bitcast(cand, F32)))
        return jnp.where(cnt >= CAPACITY, cand, prefix)

    thr = lax.fori_loop(0, 31, bisect, jnp.zeros((ROUTE_ROWS, LANES), jnp.int32))
    gt = a >= pltpu.bitcast(thr + 1, F32)
    eq = jnp.logical_and(a >= pltpu.bitcast(thr, F32), jnp.logical_not(gt))
    need = CAPACITY - per_expert(mask(gt))

    def prefix(m):
        return _dot(m, upper), _dot(rows_before, bf(_dot(m, ones)))

    eq_m = mask(eq)
    eq_in_row, eq_rows_before = prefix(eq_m)
    eq_before = eq_in_row + eq_rows_before - eq_m.astype(F32)
    sel = jnp.logical_or(gt, jnp.logical_and(eq, eq_before < need))
    sel_m = mask(sel)
    rank_in_row, start = prefix(sel_m)
    local = jnp.where(sel, rank_in_row, 0.0)
    row_total = _dot(sel_m, ones)
    lane_f = lax.broadcasted_iota(jnp.int32, (CAPACITY, LANES), 1).astype(F32)
    slot = lax.broadcasted_iota(jnp.int32, (CAPACITY, LANES), 0).astype(F32)
    tok0 = (lax.broadcasted_iota(jnp.int32, (TOK_BLOCKS, LANES), 0) * LANES).astype(F32)
    tok0_hi, tok0_lo = _split2(tok0)
    ones_cl = jnp.ones((CAPACITY, LANES), BF16)
    pad = lambda t: jnp.concatenate([t, jnp.zeros((LANES - TOK_BLOCKS, LANES), t.dtype)], axis=0)

    def as_row(v):
        hi, lo = _split2(jnp.where(li == lj, pad(v), 0.0))
        return _dot(ones_cl, hi) + _dot(ones_cl, lo)

    for e in range(N_EXPERTS):
        rows = slice(e * TOK_BLOCKS, (e + 1) * TOK_BLOCKS)
        st, tot = start[rows], row_total[rows]
        st_r, tot_r = as_row(st), as_row(tot)
        owner = mask(jnp.logical_and(st_r <= slot, slot < st_r + tot_r))
        pick = lambda t: _dot(owner, pad(t))
        st_hi, st_lo = _split2(st)
        a_hi, a_mid, a_lo = _split3(a[rows])
        want = slot - (pick(st_hi) + pick(st_lo)) + 1.0
        hit = pick(bf(local[rows])) == want
        tok = jnp.sum(jnp.where(hit, lane_f, 0.0), axis=-1, keepdims=True)
        base = pick(tok0_hi) + pick(tok0_lo)
        val = pick(a_hi) + (pick(a_mid) + pick(a_lo))
        idx_ref[e] = (tok + base[:, 0:1]).astype(jnp.int32)
        gval_ref[e] = jnp.sum(jnp.where(hit, val, 0.0), axis=-1, keepdims=True)


def _route(aff_t):
    B = aff_t.shape[0]
    out = lambda dt: jax.ShapeDtypeStruct((B, N_EXPERTS, CAPACITY, 1), dt)
    spec = pl.BlockSpec((None, N_EXPERTS, CAPACITY, 1), lambda b: (b, 0, 0, 0))
    return pl.pallas_call(
        _topk_kernel,
        grid=(B,),
        in_specs=[pl.BlockSpec((None, ROUTE_ROWS, LANES), lambda b: (b, 0, 0))],
        out_specs=[spec, spec],
        out_shape=[out(jnp.int32), out(F32)],
        compiler_params=_params("parallel"),
    )(aff_t.reshape(B, ROUTE_ROWS, LANES))


GATHER_UNROLL = 8


def _dispatch_kernel(idx_ref, h_ref, xe_ref, rows_ref):
    base = (pl.program_id(0) * N_EXPERTS + pl.program_id(1)) * CAPACITY

    def body(r, _):
        t = idx_ref[base + r]
        rows_ref[pl.ds(r, 1), :] = h_ref[pl.ds(t, 1), :]
        return 0

    lax.fori_loop(0, CAPACITY, body, 0, unroll=GATHER_UNROLL)
    xe_ref[...] = rows_ref[...].astype(BF16)


def _dispatch(idx_flat, h2):
    B = h2.shape[0]
    D = D_MODEL
    return pl.pallas_call(
        _dispatch_kernel,
        grid_spec=pltpu.PrefetchScalarGridSpec(
            num_scalar_prefetch=1,
            grid=(B, N_EXPERTS),
            in_specs=[pl.BlockSpec((None, SEQ, D), lambda b, e, idx: (b, 0, 0))],
            out_specs=pl.BlockSpec((None, None, CAPACITY, D), lambda b, e, idx: (b, e, 0, 0)),
            scratch_shapes=[pltpu.VMEM((CAPACITY, D), F32)]),
        out_shape=jax.ShapeDtypeStruct((B, N_EXPERTS, CAPACITY, D), BF16),
        compiler_params=_params("parallel", "arbitrary"),
    )(idx_flat, h2)


def _ffn_kernel(xe_ref, gv_ref, wg_ref, wu_ref, wd_ref, ye_ref, wgb_ref, wub_ref, wdb_ref):
    @pl.when(pl.program_id(1) == 0)
    def _():
        wgb_ref[...] = wg_ref[...].astype(BF16)
        wub_ref[...] = wu_ref[...].astype(BF16)
        wdb_ref[...] = wd_ref[...].astype(BF16)

    xe = xe_ref[...]
    hid = (_silu(_dot(xe, wgb_ref[...])) * _dot(xe, wub_ref[...])).astype(BF16)
    ye_ref[...] = _dot(hid, wdb_ref[...]) * gv_ref[...]


def _expert_ffn(xe, gval, w_gate, w_up, w_down):
    B = xe.shape[0]
    D = D_MODEL
    F = w_gate.shape[-1]
    return pl.pallas_call(
        _ffn_kernel,
        grid=(N_EXPERTS, B),
        in_specs=[pl.BlockSpec((None, None, CAPACITY, D), lambda e, b: (b, e, 0, 0)),
                  pl.BlockSpec((None, None, CAPACITY, 1), lambda e, b: (b, e, 0, 0)),
                  pl.BlockSpec((None, D, F), lambda e, b: (e, 0, 0)),
                  pl.BlockSpec((None, D, F), lambda e, b: (e, 0, 0)),
                  pl.BlockSpec((None, F, D), lambda e, b: (e, 0, 0))],
        out_specs=pl.BlockSpec((None, None, CAPACITY, D), lambda e, b: (b, e, 0, 0)),
        out_shape=jax.ShapeDtypeStruct((B, N_EXPERTS, CAPACITY, D), F32),
        scratch_shapes=[pltpu.VMEM((D, F), BF16), pltpu.VMEM((D, F), BF16), pltpu.VMEM((F, D), BF16)],
        compiler_params=_params("arbitrary", "arbitrary"),
    )(xe, gval, w_gate, w_up, w_down)


HALF_E = N_EXPERTS // 2
OUT_TILES = SEQ // TILE


def _combine_kernel(idx_ref, ye0_ref, ye1_ref, x1_ref, gt2_ref, fw_ref, o_ref, acc0_ref, acc1_ref):
    b = pl.program_id(0)
    s = pl.program_id(1)

    @pl.when(s == 0)
    def _():
        acc0_ref[...] = jnp.zeros_like(acc0_ref)
        acc1_ref[...] = jnp.zeros_like(acc1_ref)

    @pl.when(s < HALF_E)
    def _():
        base0 = (b * N_EXPERTS + s) * CAPACITY
        base1 = base0 + HALF_E * CAPACITY

        def body(r, _):
            t0 = idx_ref[base0 + r]
            t1 = idx_ref[base1 + r]
            acc0_ref[pl.ds(t0, 1), :] = acc0_ref[pl.ds(t0, 1), :] + ye0_ref[pl.ds(r, 1), :]
            acc1_ref[pl.ds(t1, 1), :] = acc1_ref[pl.ds(t1, 1), :] + ye1_ref[pl.ds(r, 1), :]
            return 0

        lax.fori_loop(0, CAPACITY, body, 0, unroll=GATHER_UNROLL)

    @pl.when(s >= HALF_E)
    def _():
        t = pl.multiple_of((s - HALF_E) * TILE, TILE)
        moe = acc0_ref[pl.ds(t, TILE), :] + acc1_ref[pl.ds(t, TILE), :]
        x2 = x1_ref[...] + gt2_ref[...] * moe
        o_ref[...] = x2 * lax.rsqrt(jnp.mean(x2 * x2, axis=-1, keepdims=True) + EPS) * fw_ref[...]


def _combine(idx_flat, ye, x1, gt2, final_norm):
    B = ye.shape[0]
    D = D_MODEL
    expert = lambda off: pl.BlockSpec((None, None, CAPACITY, D),
                                      lambda b, s, idx: (b, jnp.minimum(s, HALF_E - 1) + off, 0, 0))
    tile = pl.BlockSpec((None, TILE, D), lambda b, s, idx: (b, jnp.maximum(s - HALF_E, 0), 0))
    return pl.pallas_call(
        _combine_kernel,
        grid_spec=pltpu.PrefetchScalarGridSpec(
            num_scalar_prefetch=1,
            grid=(B, HALF_E + OUT_TILES),
            in_specs=[expert(0), expert(HALF_E), tile,
                      pl.BlockSpec((None, 1, D), lambda b, s, idx: (b, 0, 0)),
                      pl.BlockSpec((1, D), lambda b, s, idx: (0, 0))],
            out_specs=tile,
            scratch_shapes=[pltpu.VMEM((SEQ, D), F32), pltpu.VMEM((SEQ, D), F32)]),
        out_shape=jax.ShapeDtypeStruct((B, SEQ, D), F32),
        compiler_params=_params("parallel", "arbitrary"),
    )(idx_flat, ye, ye, x1, gt2, final_norm)


def kernel(x, c, ctx, c_ctx, w_mod, b_mod, norm_mix, norm_ffn, w_in, conv_qkv, a_log, dt_bias, gdn_norm, na_rpb,
           w_out, w_router, w_gate, w_up, w_down, final_norm):
    B, T, D = x.shape
    assert (T, D) == (SEQ, D_MODEL) and ctx.shape == (B, CTX_LEN, D) and w_mod.shape[0] == 1
    li = 0
    mod_rows = -(-(B + 1) // SUBLANES) * SUBLANES
    cc = jnp.zeros((mod_rows, D), F32).at[:B].set(c).at[B].set(c_ctx)
    mod = _modulation(cc, w_mod[li], b_mod[li])
    part = lambda j, rows: mod[:rows, j * D:(j + 1) * D].reshape(rows, 1, D)
    sh1, sc1 = part(0, B + 1), part(1, B + 1)
    gt1, sh2, sc2, gt2 = part(2, B), part(3, B), part(4, B), part(5, B)

    wi = w_in[li]
    q_end, g_end = 3 * GDN_WIDTH, 4 * GDN_WIDTH
    ab_end = g_end + 4 * GDN_HEADS
    wq = wi[:, :q_end].astype(BF16)
    wg = wi[:, q_end:g_end].astype(BF16)
    wab = jnp.zeros((D, LANES), F32).at[:, :4 * GDN_HEADS].set(wi[:, g_end:ab_end]).astype(BF16)
    wnq = wi[:, ab_end:ab_end + NA_WIDTH].astype(BF16)
    wnkv = wi[:, ab_end + NA_WIDTH:].astype(BF16)
    gpar = jnp.zeros((2, LANES), F32)
    gpar = gpar.at[0, :2 * GDN_HEADS].set(-jnp.exp(a_log[li].astype(F32)).reshape(-1))
    gpar = gpar.at[1, :2 * GDN_HEADS].set(dt_bias[li].astype(F32).reshape(-1))

    pq, gate, gb, naq, nakv = _in_projection(x, ctx, sh1, sc1, norm_mix[li].reshape(1, D),
                                             wq, wg, wab, wnq, wnkv, gpar)
    cos_tab, sin_tab = _rope_tables()
    qkv = _gdn_prep(pq, conv_qkv[li], cos_tab, sin_tab)
    o_f, o_b = _gdn_scan(qkv, gb)
    y_na = _neighbourhood_attention(naq, nakv, _na_bias_table(na_rpb[li]))

    wo = w_out[li].astype(BF16)
    x1, h2, aff_t = _out_projection(o_f, o_b, gate, y_na, x, gt1, sh2, sc2,
                                     gdn_norm[li].reshape(1, GDN_HEAD_DIM), norm_ffn[li].reshape(1, D),
                                     wo[:GDN_WIDTH], wo[GDN_WIDTH:], w_router[li].T)
    idx, gval = _route(aff_t)
    idx_flat = idx.reshape(-1)
    xe = _dispatch(idx_flat, h2)
    ye = _expert_ffn(xe, gval, w_gate[li], w_up[li], w_down[li])
    return _combine(idx_flat, ye, x1, gt2, final_norm.reshape(1, D))
```

```python
import functools
import math

import numpy as np
import jax
import jax.numpy as jnp
from jax import lax
from jax.experimental import pallas as pl
from jax.experimental.pallas import tpu as pltpu

F32 = jnp.float32
BF16 = jnp.bfloat16

D_MODEL = 1024
SEQ = 4096
CTX_LEN = 256
GRID_W = 64
GRID_ROWS = SEQ // GRID_W
GDN_HEADS = 4
GDN_HEAD_DIM = 128
GDN_WIDTH = GDN_HEADS * GDN_HEAD_DIM
CONV_W = 5
CHUNK = 64
ROPE_BASE = 10000.0
NA_HEADS = 8
NA_HEAD_DIM = 64
NA_WIDTH = NA_HEADS * NA_HEAD_DIM
NA_ROWS = 8
NA_COLS = 16
N_EXPERTS = 16
CAPACITY = 2 * SEQ // N_EXPERTS
EPS = 1e-6

TILE = 256
T_ALL = CTX_LEN + SEQ
N_TILES = T_ALL // TILE
N_CHUNKS = T_ALL // CHUNK
CTX_CHUNKS = CTX_LEN // CHUNK
LANES = 128
SUBLANES = 8
VMEM_LIMIT = 56 * 1024 * 1024


def _params(*sem):
    return pltpu.CompilerParams(dimension_semantics=sem, vmem_limit_bytes=VMEM_LIMIT)


def _dot(a, b):
    return jnp.dot(a, b, preferred_element_type=F32)


def _dot_nt(a, b):
    return lax.dot_general(a, b, (((1,), (1,)), ((), ())), preferred_element_type=F32)


def _dot_tn(a, b):
    return lax.dot_general(a, b, (((0,), (0,)), ((), ())), preferred_element_type=F32)


def _split2(a):
    hi = a.astype(BF16)
    lo = (a - hi.astype(F32)).astype(BF16)
    return hi, lo


def _split3(a):
    hi = a.astype(BF16)
    r = a - hi.astype(F32)
    mid = r.astype(BF16)
    lo = (r - mid.astype(F32)).astype(BF16)
    return hi, mid, lo


def _dot3(a, b):
    ah, al = _split2(a)
    bh, bl = _split2(b)
    return _dot(ah, bh) + (_dot(ah, bl) + _dot(al, bh))


def _silu(x):
    return x * jax.nn.sigmoid(x)


def _mod_kernel(c_ref, w_ref, b_ref, o_ref):
    s = _silu(c_ref[...])
    o_ref[...] = _dot3(s, w_ref[...]) + b_ref[...]


def _modulation(cc, w_mod, b_mod):
    rows, d = cc.shape
    n = w_mod.shape[1]
    bn = 1024
    return pl.pallas_call(
        _mod_kernel,
        grid=(n // bn,),
        in_specs=[pl.BlockSpec((rows, d), lambda j: (0, 0)),
                  pl.BlockSpec((d, bn), lambda j: (0, j)),
                  pl.BlockSpec((1, bn), lambda j: (0, j))],
        out_specs=pl.BlockSpec((rows, bn), lambda j: (0, j)),
        out_shape=jax.ShapeDtypeStruct((rows, n), F32),
        compiler_params=_params("parallel"),
    )(cc, w_mod, b_mod.reshape(1, n))


HALO = SUBLANES


def _inproj_kernel(x_ref, ctx_ref, xp_ref, xn_ref, sh_ref, sc_ref, nw_ref, wq_ref, wg_ref, wab_ref, wnq_ref,
                   wnkv_ref, gpar_ref, cw_ref, cos_ref, sin_ref, qkv_ref, gate_ref, gb_ref, naq_ref, nakv_ref, ext_ref):
    i = pl.program_id(1)
    has_prev = i >= 2
    has_next = jnp.logical_and(i >= 1, i < N_TILES - 1)
    xt = jnp.concatenate([xp_ref[...], jnp.where(i == 0, ctx_ref[...], x_ref[...]), xn_ref[...]], axis=0)
    ms = jnp.mean(xt * xt, axis=-1, keepdims=True)
    h = xt * lax.rsqrt(ms + EPS) * nw_ref[...]
    h = h * (1.0 + sc_ref[...]) + sh_ref[...]
    pq = _dot(h.astype(BF16), wq_ref[...])
    ext_ref[0:HALO, :] = jnp.where(has_prev, pq[0:HALO], 0.0)
    ext_ref[HALO:HALO + TILE, :] = pq[HALO:HALO + TILE]
    ext_ref[HALO + TILE:, :] = jnp.where(has_next, pq[HALO + TILE:], 0.0)
    hb = h[HALO:HALO + TILE].astype(BF16)
    gate_ref[...] = _dot(hb, wg_ref[...])
    first = HALO - CONV_W // 2
    acc = ext_ref[first:first + TILE, :] * cw_ref[0:1, :]
    for k in range(1, CONV_W):
        acc = acc + ext_ref[first + k:first + k + TILE, :] * cw_ref[k:k + 1, :]
    naq_ref[...] = (_dot(hb, wnq_ref[...]) * (NA_HEAD_DIM ** -0.5)).astype(BF16)
    y = _silu(acc)
    nakv_ref[...] = _dot(hb, wnkv_ref[...]).astype(BF16)
    cos = cos_ref[...]
    sin = sin_ref[...]
    lane = lax.broadcasted_iota(jnp.int32, (TILE, LANES), 1)
    take_upper = (lane % (GDN_HEAD_DIM // 2)) < (GDN_HEAD_DIM // 4)
    for j in range(2 * GDN_HEADS):
        t = y[:, j * LANES:(j + 1) * LANES]
        t = t * lax.rsqrt(jnp.sum(t * t, axis=-1, keepdims=True) + EPS)
        partner = jnp.where(take_upper, pltpu.roll(t, LANES - GDN_HEAD_DIM // 4, 1),
                            pltpu.roll(t, GDN_HEAD_DIM // 4, 1))
        t = t * cos + partner * sin
        if j < GDN_HEADS:
            t = t * (GDN_HEAD_DIM ** -0.5)
        qkv_ref[:, j * LANES:(j + 1) * LANES] = t
    qkv_ref[:, 2 * GDN_WIDTH:] = y[:, 2 * GDN_WIDTH:]
    ab = _dot(hb, wab_ref[...])
    z = ab + gpar_ref[1:2, :]
    softplus = jnp.maximum(z, 0.0) + jnp.log1p(jnp.exp(-jnp.abs(z)))
    lane = lax.broadcasted_iota(jnp.int32, ab.shape, 1)
    gb_ref[...] = jnp.where(lane < 2 * GDN_HEADS, gpar_ref[0:1, :] * softplus, jax.nn.sigmoid(ab))


def _in_projection(x, ctx, sh, sc, norm_w, wq, wg, wab, wnq, wnkv, gpar, conv_w, cos_tab, sin_tab):
    B = x.shape[0]
    D = D_MODEL
    per = TILE // HALO
    tok = lambda n: pl.BlockSpec((None, TILE, n), lambda b, i: (b, i, 0))
    full = lambda a: pl.BlockSpec(a.shape, lambda b, i: (0,) * a.ndim)
    modrow = pl.BlockSpec((None, 1, D), lambda b, i: (jnp.where(i == 0, B, b), 0, 0))
    rope = pl.BlockSpec((TILE, LANES), lambda b, i: (i, 0))
    before = pl.BlockSpec((None, HALO, D), lambda b, i: (b, jnp.maximum((i - 1) * per - 1, 0), 0))
    after = pl.BlockSpec((None, HALO, D), lambda b, i: (b, jnp.minimum(jnp.maximum(i, 1) * per, SEQ // HALO - 1), 0))
    outs = [(3 * GDN_WIDTH, F32), (GDN_WIDTH, F32), (LANES, F32), (NA_WIDTH, BF16), (2 * NA_WIDTH, BF16)]
    return pl.pallas_call(
        _inproj_kernel,
        grid=(B, N_TILES),
        in_specs=[pl.BlockSpec((None, TILE, D), lambda b, i: (b, jnp.maximum(i - 1, 0), 0)),
                  pl.BlockSpec((None, TILE, D), lambda b, i: (b, 0, 0)), before, after,
                  modrow, modrow, full(norm_w), full(wq), full(wg), full(wab), full(wnq), full(wnkv), full(gpar),
                  full(conv_w), rope, rope],
        out_specs=[tok(n) for n, _ in outs],
        out_shape=[jax.ShapeDtypeStruct((B, T_ALL, n), dt) for n, dt in outs],
        scratch_shapes=[pltpu.VMEM((TILE + 2 * HALO, 3 * GDN_WIDTH), F32)],
        compiler_params=_params("parallel", "arbitrary"),
    )(x, ctx, x, x, sh, sc, norm_w, wq, wg, wab, wnq, wnkv, gpar, conv_w, cos_tab, sin_tab)


def _rope_tables():
    half = GDN_HEAD_DIM // 2
    pairs = half // 2
    t = np.arange(SEQ)
    inv_freq = jnp.asarray(ROPE_BASE, F32) ** (-jnp.arange(pairs, dtype=F32) / pairs)

    def tab(pos):
        ang = jnp.asarray(pos, F32)[:, None] * inv_freq[None, :]
        c, s = jnp.cos(ang), jnp.sin(ang)
        return jnp.concatenate([c, c], axis=-1), jnp.concatenate([-s, s], axis=-1)

    cr, sr = tab(t // GRID_W)
    cc, sc = tab(t % GRID_W)
    cos = jnp.concatenate([cr, cc], axis=-1)
    sin = jnp.concatenate([sr, sc], axis=-1)
    cos = jnp.concatenate([jnp.ones((CTX_LEN, LANES), F32), cos], axis=0)
    sin = jnp.concatenate([jnp.zeros((CTX_LEN, LANES), F32), sin], axis=0)
    return cos, sin


SCAN_SAMPLES = 4


def _gdn_scan_kernel(qf_ref, qb_ref, gf_ref, gb_ref, of_ref, ob_ref, s_ref):
    n = pl.program_id(1)

    @pl.when(n == 0)
    def _():
        s_ref[...] = jnp.zeros_like(s_ref)

    row = lax.broadcasted_iota(jnp.int32, (CHUNK, CHUNK), 0)
    col = lax.broadcasted_iota(jnp.int32, (CHUNK, CHUNK), 1)
    eye = jnp.where(row == col, 1.0, 0.0).astype(F32)
    chains = range(SCAN_SAMPLES * 2 * GDN_HEADS)
    bf = lambda t: t.astype(BF16)
    each = lambda f, *cols: [f(*args) for args in zip(*cols)]
    q, k, v, gcol, grow, gtot, bcol, incl, strict = [], [], [], [], [], [], [], [], []
    for s in range(SCAN_SAMPLES):
        for d in range(2):
            src = qf_ref if d == 0 else qb_ref
            gbt = (gf_ref if d == 0 else gb_ref)[s]
            inc = (row >= col) if d == 0 else (row <= col)
            cm = jnp.where(inc, 1.0, 0.0).astype(BF16)
            g1, g2, g3 = _split3(gbt)
            gc = _dot(cm, g1) + (_dot(cm, g2) + _dot(cm, g3))
            gct = gc.T
            tot = gc[CHUNK - 1:CHUNK, :] if d == 0 else gc[0:1, :]
            for h in range(GDN_HEADS):
                c = d * GDN_HEADS + h
                gcol.append(gc[:, c:c + 1])
                grow.append(gct[c:c + 1, :])
                gtot.append(tot[:, c:c + 1])
                bcol.append(gbt[:, 2 * GDN_HEADS + c:2 * GDN_HEADS + c + 1])
                q.append(src[s, :, h * LANES:(h + 1) * LANES])
                k.append(src[s, :, GDN_WIDTH + h * LANES:GDN_WIDTH + (h + 1) * LANES])
                v.append(src[s, :, 2 * GDN_WIDTH + h * LANES:2 * GDN_WIDTH + (h + 1) * LANES])
                incl.append(inc)
                strict.append((row > col) if d == 0 else (row < col))
    s_old = [s_ref[c] for c in chains]
    decay = each(lambda m, gc_, gr_: jnp.exp(jnp.where(m, gc_ - gr_, -jnp.inf)), incl, gcol, grow)
    eg = each(jnp.exp, gcol)
    kb = each(lambda k_, b_: k_ * b_, k, bcol)
    a = each(lambda kb_, q_, k_: _dot_nt(bf(jnp.concatenate([kb_, q_], axis=0)), bf(k_)), kb, q, k)
    lower = each(lambda m, a_, dc: jnp.where(m, a_[:CHUNK] * dc, 0.0), strict, a, decay)
    attn = each(lambda a_, dc: bf(a_[CHUNK:] * dc), a, decay)
    tinv = each(lambda l_: eye - l_, lower)
    m = lower
    for _ in range(int(math.log2(CHUNK)) - 1):
        m = each(lambda m_: _dot(bf(m_), bf(m_)), m)
        tinv = each(lambda t_, m_: t_ + _dot(bf(t_), bf(m_)), tinv, m)
    resid = each(lambda t_, l_: (eye - t_) - _dot3(l_, t_), tinv, lower)
    tinv = each(lambda t_, r_: t_ + _dot(bf(t_), bf(r_)), tinv, resid)
    uw = each(lambda t_, v_, b_, kb_, eg_: _dot(bf(t_), bf(jnp.concatenate([v_ * b_, kb_ * eg_], axis=1))),
              tinv, v, bcol, kb, eg)
    ws = each(lambda uw_, q_, eg_, s_: _dot(bf(jnp.concatenate([uw_[:, LANES:], q_ * eg_], axis=0)), bf(s_)),
              uw, q, eg, s_old)
    vb = each(lambda uw_, ws_: bf(uw_[:, :LANES] - ws_[:CHUNK]), uw, ws)
    o = each(lambda ws_, at_, vb_: ws_[CHUNK:] + _dot(at_, vb_), ws, attn, vb)
    s_new = each(lambda s_, gt_, k_, gc_, vb_: s_ * jnp.exp(gt_) + _dot_tn(bf(k_ * jnp.exp(gt_ - gc_)), vb_),
                 s_old, gtot, k, gcol, vb)
    for c in chains:
        s, d, h = c // (2 * GDN_HEADS), (c // GDN_HEADS) % 2, c % GDN_HEADS
        o_ref = of_ref if d == 0 else ob_ref
        o_ref[s, :, h * LANES:(h + 1) * LANES] = o[c]
        s_ref[c] = s_new[c]


def _gdn_scan(qkv, gb):
    B = qkv.shape[0]
    S = SCAN_SAMPLES
    assert B % S == 0
    fwd = lambda b, n: (b, n, 0)
    bwd = lambda b, n: (b, jnp.where(n < CTX_CHUNKS, CTX_CHUNKS - 1 - n, N_CHUNKS + CTX_CHUNKS - 1 - n), 0)
    out = jax.ShapeDtypeStruct((B, T_ALL, GDN_WIDTH), F32)
    return pl.pallas_call(
        _gdn_scan_kernel,
        grid=(B // S, N_CHUNKS),
        in_specs=[pl.BlockSpec((S, CHUNK, 3 * GDN_WIDTH), fwd),
                  pl.BlockSpec((S, CHUNK, 3 * GDN_WIDTH), bwd),
                  pl.BlockSpec((S, CHUNK, LANES), fwd),
                  pl.BlockSpec((S, CHUNK, LANES), bwd)],
        out_specs=[pl.BlockSpec((S, CHUNK, GDN_WIDTH), fwd),
                   pl.BlockSpec((S, CHUNK, GDN_WIDTH), bwd)],
        out_shape=[out, out],
        scratch_shapes=[pltpu.VMEM((S * 2 * GDN_HEADS, GDN_HEAD_DIM, GDN_HEAD_DIM), F32)],
        compiler_params=_params("parallel", "arbitrary"),
    )(qkv, qkv, gb, gb)


WIN_TOKENS = NA_ROWS * GRID_W
NA_GROUP = 8


def _na_kernel(q_ref, kv_ref, bias_ref, o_ref):
    r = pl.program_id(1)
    start = jnp.clip(r - NA_ROWS // 2, 0, GRID_ROWS - NA_ROWS)
    off = pl.multiple_of(CTX_LEN + start * GRID_W, GRID_W)
    lane = lax.broadcasted_iota(jnp.int32, (GRID_W, LANES), 1)
    low = lane < NA_HEAD_DIM
    zero = jnp.zeros((GRID_W, LANES), BF16)
    each = lambda f, *cols: [f(*args) for args in zip(*cols)]
    rowmax = lambda t: jnp.max(t, axis=-1, keepdims=True)
    rowsum = lambda t: jnp.sum(t, axis=-1, keepdims=True)
    for g in range(NA_HEADS // NA_GROUP):
        pairs = range(g * NA_GROUP // 2, (g + 1) * NA_GROUP // 2)
        klanes = [slice(p * LANES, (p + 1) * LANES) for p in pairs]
        vlanes = [slice(NA_WIDTH + s.start, NA_WIDTH + s.stop) for s in klanes]
        q2 = each(lambda s: jnp.concatenate([jnp.where(low, q_ref[:, s], zero),
                                             jnp.where(low, zero, q_ref[:, s])], axis=0), klanes)
        bias = each(lambda p: jnp.concatenate([bias_ref[2 * p], bias_ref[2 * p + 1]], axis=0), pairs)
        s_lat = each(lambda q_, s, b_: _dot_nt(q_, kv_ref[pl.ds(off, WIN_TOKENS), s]) + b_, q2, klanes, bias)
        s_ctx = each(lambda q_, s: _dot_nt(q_, kv_ref[0:CTX_LEN, s]), q2, klanes)
        mx = each(lambda a, b: jnp.maximum(rowmax(a), rowmax(b)), s_lat, s_ctx)
        p_lat = each(lambda a, m: jnp.exp(a - m), s_lat, mx)
        p_ctx = each(lambda a, m: jnp.exp(a - m), s_ctx, mx)
        den = each(lambda a, b: rowsum(a) + rowsum(b), p_lat, p_ctx)
        o = each(lambda a, b, s: _dot(a.astype(BF16), kv_ref[pl.ds(off, WIN_TOKENS), s])
                 + _dot(b.astype(BF16), kv_ref[0:CTX_LEN, s]), p_lat, p_ctx, vlanes)
        o = each(lambda o_, d_: o_ / d_, o, den)
        for s, o_ in zip(klanes, o):
            o_ref[:, s] = jnp.where(low, o_[:GRID_W], o_[GRID_W:]).astype(BF16)


def _neighbourhood_attention(naq, nakv, bias_tab):
    B = naq.shape[0]
    q_blocks_before = CTX_LEN // GRID_W

    def bias_map(b, r):
        start = jnp.clip(r - NA_ROWS // 2, 0, GRID_ROWS - NA_ROWS)
        return (r - start, 0, 0, 0)

    return pl.pallas_call(
        _na_kernel,
        grid=(B, GRID_ROWS),
        in_specs=[pl.BlockSpec((None, GRID_W, NA_WIDTH), lambda b, r: (b, r + q_blocks_before, 0)),
                  pl.BlockSpec((None, T_ALL, 2 * NA_WIDTH), lambda b, r: (b, 0, 0)),
                  pl.BlockSpec((None, NA_HEADS, GRID_W, WIN_TOKENS), bias_map)],
        out_specs=pl.BlockSpec((None, GRID_W, NA_WIDTH), lambda b, r: (b, r, 0)),
        out_shape=jax.ShapeDtypeStruct((B, SEQ, NA_WIDTH), BF16),
        compiler_params=_params("parallel", "arbitrary"),
    )(naq, nakv, bias_tab)


def _na_bias_table(rpb):
    col = np.arange(GRID_W)
    col_start = np.clip(col - NA_COLS // 2, 0, GRID_W - NA_COLS)
    col_in = (col[None, :] >= col_start[:, None]) & (col[None, :] < col_start[:, None] + NA_COLS)
    col_idx = np.clip(col[None, :] - col[:, None], -(NA_COLS - 1), NA_COLS - 1) + (NA_COLS - 1)
    row_idx = np.arange(NA_ROWS)[None, :] - np.arange(NA_ROWS)[:, None] + (NA_ROWS - 1)
    row_sel = (row_idx[..., None] == np.arange(2 * NA_ROWS - 1)).astype(np.float32)
    col_sel = (col_idx[..., None] == np.arange(2 * NA_COLS - 1)).astype(np.float32)
    t = jnp.einsum('hrc,vjr,qkc->vhqjk', rpb.astype(F32), row_sel, col_sel, precision=lax.Precision.HIGHEST)
    t = jnp.where(col_in[None, None, :, None, :], t, -jnp.inf)
    return t.reshape(NA_ROWS, NA_HEADS, GRID_W, WIN_TOKENS)


def _outproj_kernel(of_ref, ob_ref, gate_ref, yna_ref, x_ref, gt1_ref, sh2_ref, sc2_ref, gn_ref, nf_ref,
                    wog_ref, won_ref, wr_ref, x1_ref, h2_ref, aff_ref):
    o = of_ref[...] + ob_ref[...]
    ys = []
    for h in range(GDN_HEADS):
        lanes = slice(h * LANES, (h + 1) * LANES)
        oh = o[:, lanes]
        yh = oh * lax.rsqrt(jnp.mean(oh * oh, axis=-1, keepdims=True) + EPS) * gn_ref[...]
        ys.append(yh * _silu(gate_ref[:, lanes]))
    yg = jnp.concatenate(ys, axis=1).astype(BF16)
    y = _dot(yg, wog_ref[...]) + _dot(yna_ref[...], won_ref[...])
    x1 = x_ref[...] + gt1_ref[...] * y
    x1_ref[...] = x1
    h2 = x1 * lax.rsqrt(jnp.mean(x1 * x1, axis=-1, keepdims=True) + EPS) * nf_ref[...]
    h2 = h2 * (1.0 + sc2_ref[...]) + sh2_ref[...]
    h2_ref[...] = h2
    hb = h2.astype(BF16)
    h_lo = (h2 - hb.astype(F32)).astype(BF16)
    wr_hi, wr_lo = _split2(wr_ref[...])
    logits = _dot_nt(wr_hi, hb) + (_dot_nt(wr_hi, h_lo) + _dot_nt(wr_lo, hb))
    e = jnp.exp(logits - jnp.max(logits, axis=0, keepdims=True))
    aff_ref[...] = e / jnp.sum(e, axis=0, keepdims=True)


def _out_projection(o_f, o_b, gate, y_na, x, gt1, sh2, sc2, gdn_norm, norm_ffn, wog, won, wrt):
    B = x.shape[0]
    D = D_MODEL
    lat = lambda n: pl.BlockSpec((None, TILE, n), lambda b, i: (b, i + 1, 0))
    tok = lambda n: pl.BlockSpec((None, TILE, n), lambda b, i: (b, i, 0))
    full = lambda a: pl.BlockSpec(a.shape, lambda b, i: (0,) * a.ndim)
    modrow = pl.BlockSpec((None, 1, D), lambda b, i: (b, 0, 0))
    return pl.pallas_call(
        _outproj_kernel,
        grid=(B, SEQ // TILE),
        in_specs=[lat(GDN_WIDTH), lat(GDN_WIDTH), lat(GDN_WIDTH), tok(NA_WIDTH), tok(D),
                  modrow, modrow, modrow, full(gdn_norm), full(norm_ffn), full(wog), full(won), full(wrt)],
        out_specs=[tok(D), tok(D),
                   pl.BlockSpec((None, N_EXPERTS, TILE), lambda b, i: (b, 0, i))],
        out_shape=[jax.ShapeDtypeStruct((B, SEQ, D), F32),
                   jax.ShapeDtypeStruct((B, SEQ, D), F32),
                   jax.ShapeDtypeStruct((B, N_EXPERTS, SEQ), F32)],
        compiler_params=_params("parallel", "arbitrary"),
    )(o_f, o_b, gate, y_na, x, gt1, sh2, sc2, gdn_norm, norm_ffn, wog, won, wrt)


TOK_BLOCKS = SEQ // LANES
ROUTE_ROWS = N_EXPERTS * TOK_BLOCKS


def _topk_kernel(aff_ref, idx_ref, gval_ref):
    a = aff_ref[...]
    bf = lambda t: t.astype(BF16)
    mask = lambda c: jnp.where(c, 1.0, 0.0).astype(BF16)
    li = lax.broadcasted_iota(jnp.int32, (LANES, LANES), 0)
    lj = lax.broadcasted_iota(jnp.int32, (LANES, LANES), 1)
    ones = jnp.ones((LANES, LANES), BF16)
    upper = mask(li <= lj)
    ri = lax.broadcasted_iota(jnp.int32, (ROUTE_ROWS, ROUTE_ROWS), 0)
    rj = lax.broadcasted_iota(jnp.int32, (ROUTE_ROWS, ROUTE_ROWS), 1)
    shift = TOK_BLOCKS.bit_length() - 1
    same = (ri >> shift) == (rj >> shift)
    expert_sum = mask(same)
    rows_before = mask(jnp.logical_and(same, rj < ri))
    per_expert = lambda m: _dot(expert_sum, bf(_dot(m, ones)))

    def bisect(it, prefix):
        cand = prefix | (jnp.int32(1) << (30 - it))
        cnt = per_expert(mask(a >= pltpu.bitcast(cand, F32)))
        return jnp.where(cnt >= CAPACITY, cand, prefix)

    thr = lax.fori_loop(0, 31, bisect, jnp.zeros((ROUTE_ROWS, LANES), jnp.int32))
    gt = a >= pltpu.bitcast(thr + 1, F32)
    eq = jnp.logical_and(a >= pltpu.bitcast(thr, F32), jnp.logical_not(gt))
    need = CAPACITY - per_expert(mask(gt))

    def prefix(m):
        return _dot(m, upper), _dot(rows_before, bf(_dot(m, ones)))

    eq_m = mask(eq)
    eq_in_row, eq_rows_before = prefix(eq_m)
    eq_before = eq_in_row + eq_rows_before - eq_m.astype(F32)
    sel = jnp.logical_or(gt, jnp.logical_and(eq, eq_before < need))
    sel_m = mask(sel)
    rank_in_row, start = prefix(sel_m)
    local = jnp.where(sel, rank_in_row, 0.0)
    row_total = _dot(sel_m, ones)
    lane_f = lax.broadcasted_iota(jnp.int32, (CAPACITY, LANES), 1).astype(F32)
    slot = lax.broadcasted_iota(jnp.int32, (CAPACITY, LANES), 0).astype(F32)
    tok0 = (lax.broadcasted_iota(jnp.int32, (TOK_BLOCKS, LANES), 0) * LANES).astype(F32)
    tok0_hi, tok0_lo = _split2(tok0)
    ones_cl = jnp.ones((CAPACITY, LANES), BF16)
    pad = lambda t: jnp.concatenate([t, jnp.zeros((LANES - TOK_BLOCKS, LANES), t.dtype)], axis=0)

    def as_row(v):
        hi, lo = _split2(jnp.where(li == lj, pad(v), 0.0))
        return _dot(ones_cl, hi) + _dot(ones_cl, lo)

    for e in range(N_EXPERTS):
        rows = slice(e * TOK_BLOCKS, (e + 1) * TOK_BLOCKS)
        st, tot = start[rows], row_total[rows]
        st_r, tot_r = as_row(st), as_row(tot)
        owner = mask(jnp.logical_and(st_r <= slot, slot < st_r + tot_r))
        pick = lambda t: _dot(owner, pad(t))
        st_hi, st_lo = _split2(st)
        a_hi, a_mid, a_lo = _split3(a[rows])
        want = slot - (pick(st_hi) + pick(st_lo)) + 1.0
        hit = pick(bf(local[rows])) == want
        tok = jnp.sum(jnp.where(hit, lane_f, 0.0), axis=-1, keepdims=True)
        base = pick(tok0_hi) + pick(tok0_lo)
        val = pick(a_hi) + (pick(a_mid) + pick(a_lo))
        idx_ref[e] = (tok + base[:, 0:1]).astype(jnp.int32)
        gval_ref[e] = jnp.sum(jnp.where(hit, val, 0.0), axis=-1, keepdims=True)


def _route(aff_t):
    B = aff_t.shape[0]
    out = lambda dt: jax.ShapeDtypeStruct((B, N_EXPERTS, CAPACITY, 1), dt)
    spec = pl.BlockSpec((None, N_EXPERTS, CAPACITY, 1), lambda b: (b, 0, 0, 0))
    return pl.pallas_call(
        _topk_kernel,
        grid=(B,),
        in_specs=[pl.BlockSpec((None, ROUTE_ROWS, LANES), lambda b: (b, 0, 0))],
        out_specs=[spec, spec],
        out_shape=[out(jnp.int32), out(F32)],
        compiler_params=_params("parallel"),
    )(aff_t.reshape(B, ROUTE_ROWS, LANES))


GATHER_UNROLL = 8


def _dispatch_kernel(idx_ref, h_ref, xe_ref, rows_ref):
    base = (pl.program_id(0) * N_EXPERTS + pl.program_id(1)) * CAPACITY

    def body(r, _):
        t = idx_ref[base + r]
        rows_ref[pl.ds(r, 1), :] = h_ref[pl.ds(t, 1), :]
        return 0

    lax.fori_loop(0, CAPACITY, body, 0, unroll=GATHER_UNROLL)
    xe_ref[...] = rows_ref[...].astype(BF16)


def _dispatch(idx_flat, h2):
    B = h2.shape[0]
    D = D_MODEL
    return pl.pallas_call(
        _dispatch_kernel,
        grid_spec=pltpu.PrefetchScalarGridSpec(
            num_scalar_prefetch=1,
            grid=(B, N_EXPERTS),
            in_specs=[pl.BlockSpec((None, SEQ, D), lambda b, e, idx: (b, 0, 0))],
            out_specs=pl.BlockSpec((None, None, CAPACITY, D), lambda b, e, idx: (b, e, 0, 0)),
            scratch_shapes=[pltpu.VMEM((CAPACITY, D), F32)]),
        out_shape=jax.ShapeDtypeStruct((B, N_EXPERTS, CAPACITY, D), BF16),
        compiler_params=_params("parallel", "arbitrary"),
    )(idx_flat, h2)


def _ffn_kernel(xe_ref, gv_ref, wg_ref, wu_ref, wd_ref, ye_ref, wgb_ref, wub_ref, wdb_ref):
    @pl.when(pl.program_id(1) == 0)
    def _():
        wgb_ref[...] = wg_ref[...].astype(BF16)
        wub_ref[...] = wu_ref[...].astype(BF16)
        wdb_ref[...] = wd_ref[...].astype(BF16)

    xe = xe_ref[...]
    hid = (_silu(_dot(xe, wgb_ref[...])) * _dot(xe, wub_ref[...])).astype(BF16)
    ye_ref[...] = _dot(hid, wdb_ref[...]) * gv_ref[...]


def _expert_ffn(xe, gval, w_gate, w_up, w_down):
    B = xe.shape[0]
    D = D_MODEL
    F = w_gate.shape[-1]
    return pl.pallas_call(
        _ffn_kernel,
        grid=(N_EXPERTS, B),
        in_specs=[pl.BlockSpec((None, None, CAPACITY, D), lambda e, b: (b, e, 0, 0)),
                  pl.BlockSpec((None, None, CAPACITY, 1), lambda e, b: (b, e, 0, 0)),
                  pl.BlockSpec((None, D, F), lambda e, b: (e, 0, 0)),
                  pl.BlockSpec((None, D, F), lambda e, b: (e, 0, 0)),
                  pl.BlockSpec((None, F, D), lambda e, b: (e, 0, 0))],
        out_specs=pl.BlockSpec((None, None, CAPACITY, D), lambda e, b: (b, e, 0, 0)),
        out_shape=jax.ShapeDtypeStruct((B, N_EXPERTS, CAPACITY, D), F32),
        scratch_shapes=[pltpu.VMEM((D, F), BF16), pltpu.VMEM((D, F), BF16), pltpu.VMEM((F, D), BF16)],
        compiler_params=_params("arbitrary", "arbitrary"),
    )(xe, gval, w_gate, w_up, w_down)


HALF_E = N_EXPERTS // 2
OUT_TILES = SEQ // TILE


def _combine_kernel(idx_ref, ye0_ref, ye1_ref, x1_ref, gt2_ref, fw_ref, o_ref, acc0_ref, acc1_ref):
    b = pl.program_id(0)
    s = pl.program_id(1)

    @pl.when(s == 0)
    def _():
        acc0_ref[...] = jnp.zeros_like(acc0_ref)
        acc1_ref[...] = jnp.zeros_like(acc1_ref)

    @pl.when(s < HALF_E)
    def _():
        base0 = (b * N_EXPERTS + s) * CAPACITY
        base1 = base0 + HALF_E * CAPACITY

        def body(r, _):
            t0 = idx_ref[base0 + r]
            t1 = idx_ref[base1 + r]
            acc0_ref[pl.ds(t0, 1), :] = acc0_ref[pl.ds(t0, 1), :] + ye0_ref[pl.ds(r, 1), :]
            acc1_ref[pl.ds(t1, 1), :] = acc1_ref[pl.ds(t1, 1), :] + ye1_ref[pl.ds(r, 1), :]
            return 0

        lax.fori_loop(0, CAPACITY, body, 0, unroll=GATHER_UNROLL)

    @pl.when(s >= HALF_E)
    def _():
        t = pl.multiple_of((s - HALF_E) * TILE, TILE)
        moe = acc0_ref[pl.ds(t, TILE), :] + acc1_ref[pl.ds(t, TILE), :]
        x2 = x1_ref[...] + gt2_ref[...] * moe
        o_ref[...] = x2 * lax.rsqrt(jnp.mean(x2 * x2, axis=-1, keepdims=True) + EPS) * fw_ref[...]


def _combine(idx_flat, ye, x1, gt2, final_norm):
    B = ye.shape[0]
    D = D_MODEL
    expert = lambda off: pl.BlockSpec((None, None, CAPACITY, D),
                                      lambda b, s, idx: (b, jnp.minimum(s, HALF_E - 1) + off, 0, 0))
    tile = pl.BlockSpec((None, TILE, D), lambda b, s, idx: (b, jnp.maximum(s - HALF_E, 0), 0))
    return pl.pallas_call(
        _combine_kernel,
        grid_spec=pltpu.PrefetchScalarGridSpec(
            num_scalar_prefetch=1,
            grid=(B, HALF_E + OUT_TILES),
            in_specs=[expert(0), expert(HALF_E), tile,
                      pl.BlockSpec((None, 1, D), lambda b, s, idx: (b, 0, 0)),
                      pl.BlockSpec((1, D), lambda b, s, idx: (0, 0))],
            out_specs=tile,
            scratch_shapes=[pltpu.VMEM((SEQ, D), F32), pltpu.VMEM((SEQ, D), F32)]),
        out_shape=jax.ShapeDtypeStruct((B, SEQ, D), F32),
        compiler_params=_params("parallel", "arbitrary"),
    )(idx_flat, ye, ye, x1, gt2, final_norm)


def kernel(x, c, ctx, c_ctx, w_mod, b_mod, norm_mix, norm_ffn, w_in, conv_qkv, a_log, dt_bias, gdn_norm, na_rpb,
           w_out, w_router, w_gate, w_up, w_down, final_norm):
    B, T, D = x.shape
    assert (T, D) == (SEQ, D_MODEL) and ctx.shape == (B, CTX_LEN, D) and w_mod.shape[0] == 1
    li = 0
    mod_rows = -(-(B + 1) // SUBLANES) * SUBLANES
    cc = jnp.zeros((mod_rows, D), F32).at[:B].set(c).at[B].set(c_ctx)
    mod = _modulation(cc, w_mod[li], b_mod[li])
    part = lambda j, rows: mod[:rows, j * D:(j + 1) * D].reshape(rows, 1, D)
    sh1, sc1 = part(0, B + 1), part(1, B + 1)
    gt1, sh2, sc2, gt2 = part(2, B), part(3, B), part(4, B), part(5, B)

    wi = w_in[li]
    q_end, g_end = 3 * GDN_WIDTH, 4 * GDN_WIDTH
    ab_end = g_end + 4 * GDN_HEADS
    wq = wi[:, :q_end].astype(BF16)
    wg = wi[:, q_end:g_end].astype(BF16)
    wab = jnp.zeros((D, LANES), F32).at[:, :4 * GDN_HEADS].set(wi[:, g_end:ab_end]).astype(BF16)
    wnq = wi[:, ab_end:ab_end + NA_WIDTH].astype(BF16)
    wnkv = wi[:, ab_end + NA_WIDTH:].astype(BF16)
    gpar = jnp.zeros((2, LANES), F32)
    gpar = gpar.at[0, :2 * GDN_HEADS].set(-jnp.exp(a_log[li].astype(F32)).reshape(-1))
    gpar = gpar.at[1, :2 * GDN_HEADS].set(dt_bias[li].astype(F32).reshape(-1))

    cos_tab, sin_tab = _rope_tables()
    qkv, gate, gb, naq, nakv = _in_projection(x, ctx, sh1, sc1, norm_mix[li].reshape(1, D),
                                              wq, wg, wab, wnq, wnkv, gpar, conv_qkv[li], cos_tab, sin_tab)
    o_f, o_b = _gdn_scan(qkv, gb)
    y_na = _neighbourhood_attention(naq, nakv, _na_bias_table(na_rpb[li]))

    wo = w_out[li].astype(BF16)
    x1, h2, aff_t = _out_projection(o_f, o_b, gate, y_na, x, gt1, sh2, sc2,
                                     gdn_norm[li].reshape(1, GDN_HEAD_DIM), norm_ffn[li].reshape(1, D),
                                     wo[:GDN_WIDTH], wo[GDN_WIDTH:], w_router[li].T)
    idx, gval = _route(aff_t)
    idx_flat = idx.reshape(-1)
    xe = _dispatch(idx_flat, h2)
    ye = _expert_ffn(xe, gval, w_gate[li], w_up[li], w_down[li])
    return _combine(idx_flat, ye, x1, gt2, final_norm.reshape(1, D))
```

```python
import functools
import math

import numpy as np
import jax
import jax.numpy as jnp
from jax import lax
from jax.experimental import pallas as pl
from jax.experimental.pallas import tpu as pltpu

F32 = jnp.float32
BF16 = jnp.bfloat16

D_MODEL = 1024
SEQ = 4096
CTX_LEN = 256
GRID_W = 64
GRID_ROWS = SEQ // GRID_W
GDN_HEADS = 4
GDN_HEAD_DIM = 128
GDN_WIDTH = GDN_HEADS * GDN_HEAD_DIM
CONV_W = 5
CHUNK = 64
ROPE_BASE = 10000.0
NA_HEADS = 8
NA_HEAD_DIM = 64
NA_WIDTH = NA_HEADS * NA_HEAD_DIM
NA_ROWS = 8
NA_COLS = 16
N_EXPERTS = 16
CAPACITY = 2 * SEQ // N_EXPERTS
EPS = 1e-6

TILE = 256
T_ALL = CTX_LEN + SEQ
N_TILES = T_ALL // TILE
N_CHUNKS = T_ALL // CHUNK
CTX_CHUNKS = CTX_LEN // CHUNK
LANES = 128
SUBLANES = 8
VMEM_LIMIT = 56 * 1024 * 1024


def _params(*sem):
    return pltpu.CompilerParams(dimension_semantics=sem, vmem_limit_bytes=VMEM_LIMIT)


def _dot(a, b):
    return jnp.dot(a, b, preferred_element_type=F32)


def _dot_nt(a, b):
    return lax.dot_general(a, b, (((1,), (1,)), ((), ())), preferred_element_type=F32)


def _dot_tn(a, b):
    return lax.dot_general(a, b, (((0,), (0,)), ((), ())), preferred_element_type=F32)


def _split2(a):
    hi = a.astype(BF16)
    lo = (a - hi.astype(F32)).astype(BF16)
    return hi, lo


def _split3(a):
    hi = a.astype(BF16)
    r = a - hi.astype(F32)
    mid = r.astype(BF16)
    lo = (r - mid.astype(F32)).astype(BF16)
    return hi, mid, lo


def _dot3(a, b):
    ah, al = _split2(a)
    bh, bl = _split2(b)
    return _dot(ah, bh) + (_dot(ah, bl) + _dot(al, bh))


def _silu(x):
    return x * jax.nn.sigmoid(x)


def _mod_kernel(c_ref, w_ref, b_ref, o_ref):
    s = _silu(c_ref[...])
    o_ref[...] = _dot3(s, w_ref[...]) + b_ref[...]


def _modulation(cc, w_mod, b_mod):
    rows, d = cc.shape
    n = w_mod.shape[1]
    bn = 1024
    return pl.pallas_call(
        _mod_kernel,
        grid=(n // bn,),
        in_specs=[pl.BlockSpec((rows, d), lambda j: (0, 0)),
                  pl.BlockSpec((d, bn), lambda j: (0, j)),
                  pl.BlockSpec((1, bn), lambda j: (0, j))],
        out_specs=pl.BlockSpec((rows, bn), lambda j: (0, j)),
        out_shape=jax.ShapeDtypeStruct((rows, n), F32),
        compiler_params=_params("parallel"),
    )(cc, w_mod, b_mod.reshape(1, n))


HALO = SUBLANES


def _inproj_kernel(x_ref, ctx_ref, xp_ref, xn_ref, sh_ref, sc_ref, nw_ref, wq_ref, wg_ref, wab_ref, wnq_ref,
                   wnkv_ref, gpar_ref, cw_ref, cos_ref, sin_ref, qkv_ref, gate_ref, gb_ref, naq_ref, nakv_ref, ext_ref):
    i = pl.program_id(1)
    has_prev = i >= 2
    has_next = jnp.logical_and(i >= 1, i < N_TILES - 1)
    xt = jnp.concatenate([xp_ref[...], jnp.where(i == 0, ctx_ref[...], x_ref[...]), xn_ref[...]], axis=0)
    ms = jnp.mean(xt * xt, axis=-1, keepdims=True)
    h = xt * lax.rsqrt(ms + EPS) * nw_ref[...]
    h = h * (1.0 + sc_ref[...]) + sh_ref[...]
    pq = _dot(h.astype(BF16), wq_ref[...])
    ext_ref[0:HALO, :] = jnp.where(has_prev, pq[0:HALO], 0.0)
    ext_ref[HALO:HALO + TILE, :] = pq[HALO:HALO + TILE]
    ext_ref[HALO + TILE:, :] = jnp.where(has_next, pq[HALO + TILE:], 0.0)
    hb = h[HALO:HALO + TILE].astype(BF16)
    gate_ref[...] = _dot(hb, wg_ref[...])
    first = HALO - CONV_W // 2
    acc = ext_ref[first:first + TILE, :] * cw_ref[0:1, :]
    for k in range(1, CONV_W):
        acc = acc + ext_ref[first + k:first + k + TILE, :] * cw_ref[k:k + 1, :]
    naq_ref[...] = (_dot(hb, wnq_ref[...]) * (NA_HEAD_DIM ** -0.5)).astype(BF16)
    y = _silu(acc)
    nakv_ref[...] = _dot(hb, wnkv_ref[...]).astype(BF16)
    cos = cos_ref[...]
    sin = sin_ref[...]
    lane = lax.broadcasted_iota(jnp.int32, (TILE, LANES), 1)
    take_upper = (lane % (GDN_HEAD_DIM // 2)) < (GDN_HEAD_DIM // 4)
    for j in range(2 * GDN_HEADS):
        t = y[:, j * LANES:(j + 1) * LANES]
        t = t * lax.rsqrt(jnp.sum(t * t, axis=-1, keepdims=True) + EPS)
        partner = jnp.where(take_upper, pltpu.roll(t, LANES - GDN_HEAD_DIM // 4, 1),
                            pltpu.roll(t, GDN_HEAD_DIM // 4, 1))
        t = t * cos + partner * sin
        if j < GDN_HEADS:
            t = t * (GDN_HEAD_DIM ** -0.5)
        qkv_ref[:, j * LANES:(j + 1) * LANES] = t
    qkv_ref[:, 2 * GDN_WIDTH:] = y[:, 2 * GDN_WIDTH:]
    ab = _dot(hb, wab_ref[...])
    z = ab + gpar_ref[1:2, :]
    softplus = jnp.maximum(z, 0.0) + jnp.log1p(jnp.exp(-jnp.abs(z)))
    lane = lax.broadcasted_iota(jnp.int32, ab.shape, 1)
    gb_ref[...] = jnp.where(lane < 2 * GDN_HEADS, gpar_ref[0:1, :] * softplus, jax.nn.sigmoid(ab))


def _in_projection(x, ctx, sh, sc, norm_w, wq, wg, wab, wnq, wnkv, gpar, conv_w, cos_tab, sin_tab):
    B = x.shape[0]
    D = D_MODEL
    per = TILE // HALO
    tok = lambda n: pl.BlockSpec((None, TILE, n), lambda b, i: (b, i, 0))
    full = lambda a: pl.BlockSpec(a.shape, lambda b, i: (0,) * a.ndim)
    modrow = pl.BlockSpec((None, 1, D), lambda b, i: (jnp.where(i == 0, B, b), 0, 0))
    rope = pl.BlockSpec((TILE, LANES), lambda b, i: (i, 0))
    before = pl.BlockSpec((None, HALO, D), lambda b, i: (b, jnp.maximum((i - 1) * per - 1, 0), 0))
    after = pl.BlockSpec((None, HALO, D), lambda b, i: (b, jnp.minimum(jnp.maximum(i, 1) * per, SEQ // HALO - 1), 0))
    outs = [(3 * GDN_WIDTH, F32), (GDN_WIDTH, F32), (LANES, F32), (NA_WIDTH, BF16), (2 * NA_WIDTH, BF16)]
    return pl.pallas_call(
        _inproj_kernel,
        grid=(B, N_TILES),
        in_specs=[pl.BlockSpec((None, TILE, D), lambda b, i: (b, jnp.maximum(i - 1, 0), 0)),
                  pl.BlockSpec((None, TILE, D), lambda b, i: (b, 0, 0)), before, after,
                  modrow, modrow, full(norm_w), full(wq), full(wg), full(wab), full(wnq), full(wnkv), full(gpar),
                  full(conv_w), rope, rope],
        out_specs=[tok(n) for n, _ in outs],
        out_shape=[jax.ShapeDtypeStruct((B, T_ALL, n), dt) for n, dt in outs],
        scratch_shapes=[pltpu.VMEM((TILE + 2 * HALO, 3 * GDN_WIDTH), F32)],
        compiler_params=_params("parallel", "arbitrary"),
    )(x, ctx, x, x, sh, sc, norm_w, wq, wg, wab, wnq, wnkv, gpar, conv_w, cos_tab, sin_tab)


def _rope_tables():
    half = GDN_HEAD_DIM // 2
    pairs = half // 2
    t = np.arange(SEQ)
    inv_freq = jnp.asarray(ROPE_BASE, F32) ** (-jnp.arange(pairs, dtype=F32) / pairs)

    def tab(pos):
        ang = jnp.asarray(pos, F32)[:, None] * inv_freq[None, :]
        c, s = jnp.cos(ang), jnp.sin(ang)
        return jnp.concatenate([c, c], axis=-1), jnp.concatenate([-s, s], axis=-1)

    cr, sr = tab(t // GRID_W)
    cc, sc = tab(t % GRID_W)
    cos = jnp.concatenate([cr, cc], axis=-1)
    sin = jnp.concatenate([sr, sc], axis=-1)
    cos = jnp.concatenate([jnp.ones((CTX_LEN, LANES), F32), cos], axis=0)
    sin = jnp.concatenate([jnp.zeros((CTX_LEN, LANES), F32), sin], axis=0)
    return cos, sin


SCAN_SAMPLES = 4


def _gdn_scan_kernel(qf_ref, qb_ref, gf_ref, gb_ref, of_ref, ob_ref, s_ref):
    n = pl.program_id(1)

    @pl.when(n == 0)
    def _():
        s_ref[...] = jnp.zeros_like(s_ref)

    row = lax.broadcasted_iota(jnp.int32, (CHUNK, CHUNK), 0)
    col = lax.broadcasted_iota(jnp.int32, (CHUNK, CHUNK), 1)
    eye = jnp.where(row == col, 1.0, 0.0).astype(F32)
    chains = range(SCAN_SAMPLES * 2 * GDN_HEADS)
    bf = lambda t: t.astype(BF16)
    each = lambda f, *cols: [f(*args) for args in zip(*cols)]
    q, k, v, gcol, grow, gtot, bcol, incl, strict = [], [], [], [], [], [], [], [], []
    for s in range(SCAN_SAMPLES):
        for d in range(2):
            src = qf_ref if d == 0 else qb_ref
            gbt = (gf_ref if d == 0 else gb_ref)[s]
            inc = (row >= col) if d == 0 else (row <= col)
            cm = jnp.where(inc, 1.0, 0.0).astype(BF16)
            g1, g2, g3 = _split3(gbt)
            gc = _dot(cm, g1) + (_dot(cm, g2) + _dot(cm, g3))
            gct = gc.T
            tot = gc[CHUNK - 1:CHUNK, :] if d == 0 else gc[0:1, :]
            for h in range(GDN_HEADS):
                c = d * GDN_HEADS + h
                gcol.append(gc[:, c:c + 1])
                grow.append(gct[c:c + 1, :])
                gtot.append(tot[:, c:c + 1])
                bcol.append(gbt[:, 2 * GDN_HEADS + c:2 * GDN_HEADS + c + 1])
                q.append(src[s, :, h * LANES:(h + 1) * LANES])
                k.append(src[s, :, GDN_WIDTH + h * LANES:GDN_WIDTH + (h + 1) * LANES])
                v.append(src[s, :, 2 * GDN_WIDTH + h * LANES:2 * GDN_WIDTH + (h + 1) * LANES])
                incl.append(inc)
                strict.append((row > col) if d == 0 else (row < col))
    s_old = [s_ref[c] for c in chains]
    decay = each(lambda m, gc_, gr_: jnp.exp(jnp.where(m, gc_ - gr_, -jnp.inf)), incl, gcol, grow)
    eg = each(jnp.exp, gcol)
    kb = each(lambda k_, b_: k_ * b_, k, bcol)
    a = each(lambda kb_, q_, k_: _dot_nt(bf(jnp.concatenate([kb_, q_], axis=0)), bf(k_)), kb, q, k)
    lower = each(lambda m, a_, dc: jnp.where(m, a_[:CHUNK] * dc, 0.0), strict, a, decay)
    attn = each(lambda a_, dc: bf(a_[CHUNK:] * dc), a, decay)
    tinv = each(lambda l_: eye - l_, lower)
    m = lower
    for _ in range(int(math.log2(CHUNK)) - 1):
        m = each(lambda m_: _dot(bf(m_), bf(m_)), m)
        tinv = each(lambda t_, m_: t_ + _dot(bf(t_), bf(m_)), tinv, m)
    resid = each(lambda t_, l_: (eye - t_) - _dot3(l_, t_), tinv, lower)
    tinv = each(lambda t_, r_: t_ + _dot(bf(t_), bf(r_)), tinv, resid)
    uw = each(lambda t_, v_, b_, kb_, eg_: _dot(bf(t_), bf(jnp.concatenate([v_ * b_, kb_ * eg_], axis=1))),
              tinv, v, bcol, kb, eg)
    ws = each(lambda uw_, q_, eg_, s_: _dot(bf(jnp.concatenate([uw_[:, LANES:], q_ * eg_], axis=0)), bf(s_)),
              uw, q, eg, s_old)
    vb = each(lambda uw_, ws_: bf(uw_[:, :LANES] - ws_[:CHUNK]), uw, ws)
    o = each(lambda ws_, at_, vb_: ws_[CHUNK:] + _dot(at_, vb_), ws, attn, vb)
    s_new = each(lambda s_, gt_, k_, gc_, vb_: s_ * jnp.exp(gt_) + _dot_tn(bf(k_ * jnp.exp(gt_ - gc_)), vb_),
                 s_old, gtot, k, gcol, vb)
    for c in chains:
        s, d, h = c // (2 * GDN_HEADS), (c // GDN_HEADS) % 2, c % GDN_HEADS
        o_ref = of_ref if d == 0 else ob_ref
        o_ref[s, :, h * LANES:(h + 1) * LANES] = o[c]
        s_ref[c] = s_new[c]


def _gdn_scan(qkv, gb):
    B = qkv.shape[0]
    S = SCAN_SAMPLES
    assert B % S == 0
    fwd = lambda b, n: (b, n, 0)
    bwd = lambda b, n: (b, jnp.where(n < CTX_CHUNKS, CTX_CHUNKS - 1 - n, N_CHUNKS + CTX_CHUNKS - 1 - n), 0)
    out = jax.ShapeDtypeStruct((B, T_ALL, GDN_WIDTH), F32)
    return pl.pallas_call(
        _gdn_scan_kernel,
        grid=(B // S, N_CHUNKS),
        in_specs=[pl.BlockSpec((S, CHUNK, 3 * GDN_WIDTH), fwd),
                  pl.BlockSpec((S, CHUNK, 3 * GDN_WIDTH), bwd),
                  pl.BlockSpec((S, CHUNK, LANES), fwd),
                  pl.BlockSpec((S, CHUNK, LANES), bwd)],
        out_specs=[pl.BlockSpec((S, CHUNK, GDN_WIDTH), fwd),
                   pl.BlockSpec((S, CHUNK, GDN_WIDTH), bwd)],
        out_shape=[out, out],
        scratch_shapes=[pltpu.VMEM((S * 2 * GDN_HEADS, GDN_HEAD_DIM, GDN_HEAD_DIM), F32)],
        compiler_params=_params("parallel", "arbitrary"),
    )(qkv, qkv, gb, gb)


WIN_TOKENS = NA_ROWS * GRID_W
NA_STEP_ROWS = 2


def _window_start(r):
    return jnp.clip(r - NA_ROWS // 2, 0, GRID_ROWS - NA_ROWS)


def _na_kernel(q_ref, kv_ref, *rest):
    bias_refs, o_ref = rest[:NA_STEP_ROWS], rest[NA_STEP_ROWS]
    lane = lax.broadcasted_iota(jnp.int32, (GRID_W, LANES), 1)
    low = lane < NA_HEAD_DIM
    zero = jnp.zeros((GRID_W, LANES), BF16)
    each = lambda f, *cols: [f(*args) for args in zip(*cols)]
    rowmax = lambda t: jnp.max(t, axis=-1, keepdims=True)
    rowsum = lambda t: jnp.sum(t, axis=-1, keepdims=True)
    items = [(j, p) for j in range(NA_STEP_ROWS) for p in range(NA_HEADS // 2)]
    qrows = [slice(j * GRID_W, (j + 1) * GRID_W) for j, _ in items]
    klanes = [slice(p * LANES, (p + 1) * LANES) for _, p in items]
    vlanes = [slice(NA_WIDTH + s.start, NA_WIDTH + s.stop) for s in klanes]
    offs = [pl.multiple_of(CTX_LEN + _window_start(pl.program_id(1) * NA_STEP_ROWS + j) * GRID_W, GRID_W)
            for j in range(NA_STEP_ROWS)]
    off = [offs[j] for j, _ in items]
    q2 = each(lambda rw, s: jnp.concatenate([jnp.where(low, q_ref[rw, s], zero),
                                             jnp.where(low, zero, q_ref[rw, s])], axis=0), qrows, klanes)
    bias = [jnp.concatenate([bias_refs[j][2 * p], bias_refs[j][2 * p + 1]], axis=0) for j, p in items]
    s_lat = each(lambda q_, o_, s, b_: _dot_nt(q_, kv_ref[pl.ds(o_, WIN_TOKENS), s]) + b_, q2, off, klanes, bias)
    s_ctx = each(lambda q_, s: _dot_nt(q_, kv_ref[0:CTX_LEN, s]), q2, klanes)
    mx = each(lambda a, b: jnp.maximum(rowmax(a), rowmax(b)), s_lat, s_ctx)
    p_lat = each(lambda a, m: jnp.exp(a - m), s_lat, mx)
    p_ctx = each(lambda a, m: jnp.exp(a - m), s_ctx, mx)
    den = each(lambda a, b: rowsum(a) + rowsum(b), p_lat, p_ctx)
    o = each(lambda a, b, o_, s: _dot(a.astype(BF16), kv_ref[pl.ds(o_, WIN_TOKENS), s])
             + _dot(b.astype(BF16), kv_ref[0:CTX_LEN, s]), p_lat, p_ctx, off, vlanes)
    o = each(lambda o_, d_: o_ / d_, o, den)
    for rw, s, o_ in zip(qrows, klanes, o):
        o_ref[rw, s] = jnp.where(low, o_[:GRID_W], o_[GRID_W:]).astype(BF16)


def _neighbourhood_attention(naq, nakv, bias_tab):
    B = naq.shape[0]
    R = NA_STEP_ROWS
    step_tokens = R * GRID_W
    q_blocks_before = CTX_LEN // step_tokens

    def bias_spec(j):
        return pl.BlockSpec((None, NA_HEADS, GRID_W, WIN_TOKENS),
                            lambda b, s: (s * R + j - _window_start(s * R + j), 0, 0, 0))

    return pl.pallas_call(
        _na_kernel,
        grid=(B, GRID_ROWS // R),
        in_specs=[pl.BlockSpec((None, step_tokens, NA_WIDTH), lambda b, s: (b, s + q_blocks_before, 0)),
                  pl.BlockSpec((None, T_ALL, 2 * NA_WIDTH), lambda b, s: (b, 0, 0))]
                 + [bias_spec(j) for j in range(R)],
        out_specs=pl.BlockSpec((None, step_tokens, NA_WIDTH), lambda b, s: (b, s, 0)),
        out_shape=jax.ShapeDtypeStruct((B, SEQ, NA_WIDTH), BF16),
        compiler_params=_params("parallel", "arbitrary"),
    )(naq, nakv, *([bias_tab] * R))


def _na_bias_table(rpb):
    col = np.arange(GRID_W)
    col_start = np.clip(col - NA_COLS // 2, 0, GRID_W - NA_COLS)
    col_in = (col[None, :] >= col_start[:, None]) & (col[None, :] < col_start[:, None] + NA_COLS)
    col_idx = np.clip(col[None, :] - col[:, None], -(NA_COLS - 1), NA_COLS - 1) + (NA_COLS - 1)
    row_idx = np.arange(NA_ROWS)[None, :] - np.arange(NA_ROWS)[:, None] + (NA_ROWS - 1)
    row_sel = (row_idx[..., None] == np.arange(2 * NA_ROWS - 1)).astype(np.float32)
    col_sel = (col_idx[..., None] == np.arange(2 * NA_COLS - 1)).astype(np.float32)
    t = jnp.einsum('hrc,vjr,qkc->vhqjk', rpb.astype(F32), row_sel, col_sel, precision=lax.Precision.HIGHEST)
    t = jnp.where(col_in[None, None, :, None, :], t, -jnp.inf)
    return t.reshape(NA_ROWS, NA_HEADS, GRID_W, WIN_TOKENS)


def _outproj_kernel(of_ref, ob_ref, gate_ref, yna_ref, x_ref, gt1_ref, sh2_ref, sc2_ref, gn_ref, nf_ref,
                    wog_ref, won_ref, wr_ref, x1_ref, h2_ref, aff_ref):
    o = of_ref[...] + ob_ref[...]
    ys = []
    for h in range(GDN_HEADS):
        lanes = slice(h * LANES, (h + 1) * LANES)
        oh = o[:, lanes]
        yh = oh * lax.rsqrt(jnp.mean(oh * oh, axis=-1, keepdims=True) + EPS) * gn_ref[...]
        ys.append(yh * _silu(gate_ref[:, lanes]))
    yg = jnp.concatenate(ys, axis=1).astype(BF16)
    y = _dot(yg, wog_ref[...]) + _dot(yna_ref[...], won_ref[...])
    x1 = x_ref[...] + gt1_ref[...] * y
    x1_ref[...] = x1
    h2 = x1 * lax.rsqrt(jnp.mean(x1 * x1, axis=-1, keepdims=True) + EPS) * nf_ref[...]
    h2 = h2 * (1.0 + sc2_ref[...]) + sh2_ref[...]
    h2_ref[...] = h2
    hb = h2.astype(BF16)
    h_lo = (h2 - hb.astype(F32)).astype(BF16)
    wr_hi, wr_lo = _split2(wr_ref[...])
    logits = _dot_nt(wr_hi, hb) + (_dot_nt(wr_hi, h_lo) + _dot_nt(wr_lo, hb))
    e = jnp.exp(logits - jnp.max(logits, axis=0, keepdims=True))
    aff_ref[...] = e / jnp.sum(e, axis=0, keepdims=True)


def _out_projection(o_f, o_b, gate, y_na, x, gt1, sh2, sc2, gdn_norm, norm_ffn, wog, won, wrt):
    B = x.shape[0]
    D = D_MODEL
    lat = lambda n: pl.BlockSpec((None, TILE, n), lambda b, i: (b, i + 1, 0))
    tok = lambda n: pl.BlockSpec((None, TILE, n), lambda b, i: (b, i, 0))
    full = lambda a: pl.BlockSpec(a.shape, lambda b, i: (0,) * a.ndim)
    modrow = pl.BlockSpec((None, 1, D), lambda b, i: (b, 0, 0))
    return pl.pallas_call(
        _outproj_kernel,
        grid=(B, SEQ // TILE),
        in_specs=[lat(GDN_WIDTH), lat(GDN_WIDTH), lat(GDN_WIDTH), tok(NA_WIDTH), tok(D),
                  modrow, modrow, modrow, full(gdn_norm), full(norm_ffn), full(wog), full(won), full(wrt)],
        out_specs=[tok(D), tok(D),
                   pl.BlockSpec((None, N_EXPERTS, TILE), lambda b, i: (b, 0, i))],
        out_shape=[jax.ShapeDtypeStruct((B, SEQ, D), F32),
                   jax.ShapeDtypeStruct((B, SEQ, D), F32),
                   jax.ShapeDtypeStruct((B, N_EXPERTS, SEQ), F32)],
        compiler_params=_params("parallel", "arbitrary"),
    )(o_f, o_b, gate, y_na, x, gt1, sh2, sc2, gdn_norm, norm_ffn, wog, won, wrt)


TOK_BLOCKS = SEQ // LANES
ROUTE_ROWS = N_EXPERTS * TOK_BLOCKS


def _topk_kernel(aff_ref, idx_ref, gval_ref):
    a = aff_ref[...]
    bf = lambda t: t.astype(BF16)
    mask = lambda c: jnp.where(c, 1.0, 0.0).astype(BF16)
    li = lax.broadcasted_iota(jnp.int32, (LANES, LANES), 0)
    lj = lax.broadcasted_iota(jnp.int32, (LANES, LANES), 1)
    ones = jnp.ones((LANES, LANES), BF16)
    upper = mask(li <= lj)
    ri = lax.broadcasted_iota(jnp.int32, (ROUTE_ROWS, ROUTE_ROWS), 0)
    rj = lax.broadcasted_iota(jnp.int32, (ROUTE_ROWS, ROUTE_ROWS), 1)
    shift = TOK_BLOCKS.bit_length() - 1
    same = (ri >> shift) == (rj >> shift)
    expert_sum = mask(same)
    rows_before = mask(jnp.logical_and(same, rj < ri))
    per_expert = lambda m: _dot(expert_sum, bf(_dot(m, ones)))

    def bisect(it, prefix):
        cand = prefix | (jnp.int32(1) << (30 - it))
        cnt = per_expert(mask(a >= pltpu.bitcast(cand, F32)))
        return jnp.where(cnt >= CAPACITY, cand, prefix)

    thr = lax.fori_loop(0, 31, bisect, jnp.zeros((ROUTE_ROWS, LANES), jnp.int32))
    gt = a >= pltpu.bitcast(thr + 1, F32)
    eq = jnp.logical_and(a >= pltpu.bitcast(thr, F32), jnp.logical_not(gt))
    need = CAPACITY - per_expert(mask(gt))

    def prefix(m):
        return _dot(m, upper), _dot(rows_before, bf(_dot(m, ones)))

    eq_m = mask(eq)
    eq_in_row, eq_rows_before = prefix(eq_m)
    eq_before = eq_in_row + eq_rows_before - eq_m.astype(F32)
    sel = jnp.logical_or(gt, jnp.logical_and(eq, eq_before < need))
    sel_m = mask(sel)
    rank_in_row, start = prefix(sel_m)
    local = jnp.where(sel, rank_in_row, 0.0)
    row_total = _dot(sel_m, ones)
    lane_f = lax.broadcasted_iota(jnp.int32, (CAPACITY, LANES), 1).astype(F32)
    slot = lax.broadcasted_iota(jnp.int32, (CAPACITY, LANES), 0).astype(F32)
    tok0 = (lax.broadcasted_iota(jnp.int32, (TOK_BLOCKS, LANES), 0) * LANES).astype(F32)
    tok0_hi, tok0_lo = _split2(tok0)
    ones_cl = jnp.ones((CAPACITY, LANES), BF16)
    pad = lambda t: jnp.concatenate([t, jnp.zeros((LANES - TOK_BLOCKS, LANES), t.dtype)], axis=0)

    def as_row(v):
        hi, lo = _split2(jnp.where(li == lj, pad(v), 0.0))
        return _dot(ones_cl, hi) + _dot(ones_cl, lo)

    for e in range(N_EXPERTS):
        rows = slice(e * TOK_BLOCKS, (e + 1) * TOK_BLOCKS)
        st, tot = start[rows], row_total[rows]
        st_r, tot_r = as_row(st), as_row(tot)
        owner = mask(jnp.logical_and(st_r <= slot, slot < st_r + tot_r))
        pick = lambda t: _dot(owner, pad(t))
        st_hi, st_lo = _split2(st)
        a_hi, a_mid, a_lo = _split3(a[rows])
        want = slot - (pick(st_hi) + pick(st_lo)) + 1.0
        hit = pick(bf(local[rows])) == want
        tok = jnp.sum(jnp.where(hit, lane_f, 0.0), axis=-1, keepdims=True)
        base = pick(tok0_hi) + pick(tok0_lo)
        val = pick(a_hi) + (pick(a_mid) + pick(a_lo))
        idx_ref[e] = (tok + base[:, 0:1]).astype(jnp.int32)
        gval_ref[e] = jnp.sum(jnp.where(hit, val, 0.0), axis=-1, keepdims=True)


def _route(aff_t):
    B = aff_t.shape[0]
    out = lambda dt: jax.ShapeDtypeStruct((B, N_EXPERTS, CAPACITY, 1), dt)
    spec = pl.BlockSpec((None, N_EXPERTS, CAPACITY, 1), lambda b: (b, 0, 0, 0))
    return pl.pallas_call(
        _topk_kernel,
        grid=(B,),
        in_specs=[pl.BlockSpec((None, ROUTE_ROWS, LANES), lambda b: (b, 0, 0))],
        out_specs=[spec, spec],
        out_shape=[out(jnp.int32), out(F32)],
        compiler_params=_params("parallel"),
    )(aff_t.reshape(B, ROUTE_ROWS, LANES))


GATHER_UNROLL = 8


def _dispatch_kernel(idx_ref, h_ref, xe_ref, rows_ref):
    base = (pl.program_id(0) * N_EXPERTS + pl.program_id(1)) * CAPACITY

    def body(r, _):
        t = idx_ref[base + r]
        rows_ref[pl.ds(r, 1), :] = h_ref[pl.ds(t, 1), :]
        return 0

    lax.fori_loop(0, CAPACITY, body, 0, unroll=GATHER_UNROLL)
    xe_ref[...] = rows_ref[...].astype(BF16)


def _dispatch(idx_flat, h2):
    B = h2.shape[0]
    D = D_MODEL
    return pl.pallas_call(
        _dispatch_kernel,
        grid_spec=pltpu.PrefetchScalarGridSpec(
            num_scalar_prefetch=1,
            grid=(B, N_EXPERTS),
            in_specs=[pl.BlockSpec((None, SEQ, D), lambda b, e, idx: (b, 0, 0))],
            out_specs=pl.BlockSpec((None, None, CAPACITY, D), lambda b, e, idx: (b, e, 0, 0)),
            scratch_shapes=[pltpu.VMEM((CAPACITY, D), F32)]),
        out_shape=jax.ShapeDtypeStruct((B, N_EXPERTS, CAPACITY, D), BF16),
        compiler_params=_params("parallel", "arbitrary"),
    )(idx_flat, h2)


def _ffn_kernel(xe_ref, gv_ref, wg_ref, wu_ref, wd_ref, ye_ref, wgb_ref, wub_ref, wdb_ref):
    @pl.when(pl.program_id(1) == 0)
    def _():
        wgb_ref[...] = wg_ref[...].astype(BF16)
        wub_ref[...] = wu_ref[...].astype(BF16)
        wdb_ref[...] = wd_ref[...].astype(BF16)

    xe = xe_ref[...]
    hid = (_silu(_dot(xe, wgb_ref[...])) * _dot(xe, wub_ref[...])).astype(BF16)
    ye_ref[...] = _dot(hid, wdb_ref[...]) * gv_ref[...]


def _expert_ffn(xe, gval, w_gate, w_up, w_down):
    B = xe.shape[0]
    D = D_MODEL
    F = w_gate.shape[-1]
    return pl.pallas_call(
        _ffn_kernel,
        grid=(N_EXPERTS, B),
        in_specs=[pl.BlockSpec((None, None, CAPACITY, D), lambda e, b: (b, e, 0, 0)),
                  pl.BlockSpec((None, None, CAPACITY, 1), lambda e, b: (b, e, 0, 0)),
                  pl.BlockSpec((None, D, F), lambda e, b: (e, 0, 0)),
                  pl.BlockSpec((None, D, F), lambda e, b: (e, 0, 0)),
                  pl.BlockSpec((None, F, D), lambda e, b: (e, 0, 0))],
        out_specs=pl.BlockSpec((None, None, CAPACITY, D), lambda e, b: (b, e, 0, 0)),
        out_shape=jax.ShapeDtypeStruct((B, N_EXPERTS, CAPACITY, D), F32),
        scratch_shapes=[pltpu.VMEM((D, F), BF16), pltpu.VMEM((D, F), BF16), pltpu.VMEM((F, D), BF16)],
        compiler_params=_params("arbitrary", "arbitrary"),
    )(xe, gval, w_gate, w_up, w_down)


HALF_E = N_EXPERTS // 2
OUT_TILES = SEQ // TILE


def _combine_kernel(idx_ref, ye0_ref, ye1_ref, x1_ref, gt2_ref, fw_ref, o_ref, acc0_ref, acc1_ref):
    b = pl.program_id(0)
    s = pl.program_id(1)

    @pl.when(s == 0)
    def _():
        acc0_ref[...] = jnp.zeros_like(acc0_ref)
        acc1_ref[...] = jnp.zeros_like(acc1_ref)

    @pl.when(s < HALF_E)
    def _():
        base0 = (b * N_EXPERTS + s) * CAPACITY
        base1 = base0 + HALF_E * CAPACITY

        def body(r, _):
            t0 = idx_ref[base0 + r]
            t1 = idx_ref[base1 + r]
            acc0_ref[pl.ds(t0, 1), :] = acc0_ref[pl.ds(t0, 1), :] + ye0_ref[pl.ds(r, 1), :]
            acc1_ref[pl.ds(t1, 1), :] = acc1_ref[pl.ds(t1, 1), :] + ye1_ref[pl.ds(r, 1), :]
            return 0

        lax.fori_loop(0, CAPACITY, body, 0, unroll=GATHER_UNROLL)

    @pl.when(s >= HALF_E)
    def _():
        t = pl.multiple_of((s - HALF_E) * TILE, TILE)
        moe = acc0_ref[pl.ds(t, TILE), :] + acc1_ref[pl.ds(t, TILE), :]
        x2 = x1_ref[...] + gt2_ref[...] * moe
        o_ref[...] = x2 * lax.rsqrt(jnp.mean(x2 * x2, axis=-1, keepdims=True) + EPS) * fw_ref[...]


def _combine(idx_flat, ye, x1, gt2, final_norm):
    B = ye.shape[0]
    D = D_MODEL
    expert = lambda off: pl.BlockSpec((None, None, CAPACITY, D),
                                      lambda b, s, idx: (b, jnp.minimum(s, HALF_E - 1) + off, 0, 0))
    tile = pl.BlockSpec((None, TILE, D), lambda b, s, idx: (b, jnp.maximum(s - HALF_E, 0), 0))
    return pl.pallas_call(
        _combine_kernel,
        grid_spec=pltpu.PrefetchScalarGridSpec(
            num_scalar_prefetch=1,
            grid=(B, HALF_E + OUT_TILES),
            in_specs=[expert(0), expert(HALF_E), tile,
                      pl.BlockSpec((None, 1, D), lambda b, s, idx: (b, 0, 0)),
                      pl.BlockSpec((1, D), lambda b, s, idx: (0, 0))],
            out_specs=tile,
            scratch_shapes=[pltpu.VMEM((SEQ, D), F32), pltpu.VMEM((SEQ, D), F32)]),
        out_shape=jax.ShapeDtypeStruct((B, SEQ, D), F32),
        compiler_params=_params("parallel", "arbitrary"),
    )(idx_flat, ye, ye, x1, gt2, final_norm)


def kernel(x, c, ctx, c_ctx, w_mod, b_mod, norm_mix, norm_ffn, w_in, conv_qkv, a_log, dt_bias, gdn_norm, na_rpb,
           w_out, w_router, w_gate, w_up, w_down, final_norm):
    B, T, D = x.shape
    assert (T, D) == (SEQ, D_MODEL) and ctx.shape == (B, CTX_LEN, D) and w_mod.shape[0] == 1
    li = 0
    mod_rows = -(-(B + 1) // SUBLANES) * SUBLANES
    cc = jnp.zeros((mod_rows, D), F32).at[:B].set(c).at[B].set(c_ctx)
    mod = _modulation(cc, w_mod[li], b_mod[li])
    part = lambda j, rows: mod[:rows, j * D:(j + 1) * D].reshape(rows, 1, D)
    sh1, sc1 = part(0, B + 1), part(1, B + 1)
    gt1, sh2, sc2, gt2 = part(2, B), part(3, B), part(4, B), part(5, B)

    wi = w_in[li]
    q_end, g_end = 3 * GDN_WIDTH, 4 * GDN_WIDTH
    ab_end = g_end + 4 * GDN_HEADS
    wq = wi[:, :q_end].astype(BF16)
    wg = wi[:, q_end:g_end].astype(BF16)
    wab = jnp.zeros((D, LANES), F32).at[:, :4 * GDN_HEADS].set(wi[:, g_end:ab_end]).astype(BF16)
    wnq = wi[:, ab_end:ab_end + NA_WIDTH].astype(BF16)
    wnkv = wi[:, ab_end + NA_WIDTH:].astype(BF16)
    gpar = jnp.zeros((2, LANES), F32)
    gpar = gpar.at[0, :2 * GDN_HEADS].set(-jnp.exp(a_log[li].astype(F32)).reshape(-1))
    gpar = gpar.at[1, :2 * GDN_HEADS].set(dt_bias[li].astype(F32).reshape(-1))

    cos_tab, sin_tab = _rope_tables()
    qkv, gate, gb, naq, nakv = _in_projection(x, ctx, sh1, sc1, norm_mix[li].reshape(1, D),
                                              wq, wg, wab, wnq, wnkv, gpar, conv_qkv[li], cos_tab, sin_tab)
    o_f, o_b = _gdn_scan(qkv, gb)
    y_na = _neighbourhood_attention(naq, nakv, _na_bias_table(na_rpb[li]))

    wo = w_out[li].astype(BF16)
    x1, h2, aff_t = _out_projection(o_f, o_b, gate, y_na, x, gt1, sh2, sc2,
                                     gdn_norm[li].reshape(1, GDN_HEAD_DIM), norm_ffn[li].reshape(1, D),
                                     wo[:GDN_WIDTH], wo[GDN_WIDTH:], w_router[li].T)
    idx, gval = _route(aff_t)
    idx_flat = idx.reshape(-1)
    xe = _dispatch(idx_flat, h2)
    ye = _expert_ffn(xe, gval, w_gate[li], w_up[li], w_down[li])
    return _combine(idx_flat, ye, x1, gt2, final_norm.reshape(1, D))
```

```python
import functools
import math

import numpy as np
import jax
import jax.numpy as jnp
from jax import lax
from jax.experimental import pallas as pl
from jax.experimental.pallas import tpu as pltpu

F32 = jnp.float32
BF16 = jnp.bfloat16

D_MODEL = 1024
SEQ = 4096
CTX_LEN = 256
GRID_W = 64
GRID_ROWS = SEQ // GRID_W
GDN_HEADS = 4
GDN_HEAD_DIM = 128
GDN_WIDTH = GDN_HEADS * GDN_HEAD_DIM
CONV_W = 5
CHUNK = 64
ROPE_BASE = 10000.0
NA_HEADS = 8
NA_HEAD_DIM = 64
NA_WIDTH = NA_HEADS * NA_HEAD_DIM
NA_ROWS = 8
NA_COLS = 16
N_EXPERTS = 16
CAPACITY = 2 * SEQ // N_EXPERTS
EPS = 1e-6

TILE = 256
T_ALL = CTX_LEN + SEQ
N_TILES = T_ALL // TILE
N_CHUNKS = T_ALL // CHUNK
CTX_CHUNKS = CTX_LEN // CHUNK
LANES = 128
SUBLANES = 8
VMEM_LIMIT = 56 * 1024 * 1024


def _params(*sem):
    return pltpu.CompilerParams(dimension_semantics=sem, vmem_limit_bytes=VMEM_LIMIT)


def _dot(a, b):
    return jnp.dot(a, b, preferred_element_type=F32)


def _dot_nt(a, b):
    return lax.dot_general(a, b, (((1,), (1,)), ((), ())), preferred_element_type=F32)


def _dot_tn(a, b):
    return lax.dot_general(a, b, (((0,), (0,)), ((), ())), preferred_element_type=F32)


def _split2(a):
    hi = a.astype(BF16)
    lo = (a - hi.astype(F32)).astype(BF16)
    return hi, lo


def _split3(a):
    hi = a.astype(BF16)
    r = a - hi.astype(F32)
    mid = r.astype(BF16)
    lo = (r - mid.astype(F32)).astype(BF16)
    return hi, mid, lo


def _dot3(a, b):
    ah, al = _split2(a)
    bh, bl = _split2(b)
    return _dot(ah, bh) + (_dot(ah, bl) + _dot(al, bh))


def _silu(x):
    return x * jax.nn.sigmoid(x)


def _mod_kernel(c_ref, w_ref, b_ref, o_ref):
    s = _silu(c_ref[...])
    o_ref[...] = _dot3(s, w_ref[...]) + b_ref[...]


def _modulation(cc, w_mod, b_mod):
    rows, d = cc.shape
    n = w_mod.shape[1]
    bn = 1024
    return pl.pallas_call(
        _mod_kernel,
        grid=(n // bn,),
        in_specs=[pl.BlockSpec((rows, d), lambda j: (0, 0)),
                  pl.BlockSpec((d, bn), lambda j: (0, j)),
                  pl.BlockSpec((1, bn), lambda j: (0, j))],
        out_specs=pl.BlockSpec((rows, bn), lambda j: (0, j)),
        out_shape=jax.ShapeDtypeStruct((rows, n), F32),
        compiler_params=_params("parallel"),
    )(cc, w_mod, b_mod.reshape(1, n))


HALO = SUBLANES


def _inproj_kernel(x_ref, ctx_ref, xp_ref, xn_ref, sh_ref, sc_ref, nw_ref, wq_ref, wg_ref, wab_ref, wnq_ref,
                   wnkv_ref, gpar_ref, cw_ref, cos_ref, sin_ref, qkv_ref, gate_ref, gb_ref, naq_ref, nakv_ref, ext_ref):
    i = pl.program_id(1)
    has_prev = i >= 2
    has_next = jnp.logical_and(i >= 1, i < N_TILES - 1)
    xt = jnp.concatenate([xp_ref[...], jnp.where(i == 0, ctx_ref[...], x_ref[...]), xn_ref[...]], axis=0)
    ms = jnp.mean(xt * xt, axis=-1, keepdims=True)
    h = xt * lax.rsqrt(ms + EPS) * nw_ref[...]
    h = h * (1.0 + sc_ref[...]) + sh_ref[...]
    pq = _dot(h.astype(BF16), wq_ref[...])
    ext_ref[0:HALO, :] = jnp.where(has_prev, pq[0:HALO], 0.0)
    ext_ref[HALO:HALO + TILE, :] = pq[HALO:HALO + TILE]
    ext_ref[HALO + TILE:, :] = jnp.where(has_next, pq[HALO + TILE:], 0.0)
    hb = h[HALO:HALO + TILE].astype(BF16)
    gate_ref[...] = _dot(hb, wg_ref[...])
    first = HALO - CONV_W // 2
    acc = ext_ref[first:first + TILE, :] * cw_ref[0:1, :]
    for k in range(1, CONV_W):
        acc = acc + ext_ref[first + k:first + k + TILE, :] * cw_ref[k:k + 1, :]
    naq_ref[...] = (_dot(hb, wnq_ref[...]) * (NA_HEAD_DIM ** -0.5)).astype(BF16)
    y = _silu(acc)
    nakv_ref[...] = _dot(hb, wnkv_ref[...]).astype(BF16)
    cos = cos_ref[...]
    sin = sin_ref[...]
    lane = lax.broadcasted_iota(jnp.int32, (TILE, LANES), 1)
    take_upper = (lane % (GDN_HEAD_DIM // 2)) < (GDN_HEAD_DIM // 4)
    for j in range(2 * GDN_HEADS):
        t = y[:, j * LANES:(j + 1) * LANES]
        t = t * lax.rsqrt(jnp.sum(t * t, axis=-1, keepdims=True) + EPS)
        partner = jnp.where(take_upper, pltpu.roll(t, LANES - GDN_HEAD_DIM // 4, 1),
                            pltpu.roll(t, GDN_HEAD_DIM // 4, 1))
        t = t * cos + partner * sin
        if j < GDN_HEADS:
            t = t * (GDN_HEAD_DIM ** -0.5)
        qkv_ref[:, j * LANES:(j + 1) * LANES] = t
    qkv_ref[:, 2 * GDN_WIDTH:] = y[:, 2 * GDN_WIDTH:]
    ab = _dot(hb, wab_ref[...])
    z = ab + gpar_ref[1:2, :]
    softplus = jnp.maximum(z, 0.0) + jnp.log1p(jnp.exp(-jnp.abs(z)))
    lane = lax.broadcasted_iota(jnp.int32, ab.shape, 1)
    gb_ref[...] = jnp.where(lane < 2 * GDN_HEADS, gpar_ref[0:1, :] * softplus, jax.nn.sigmoid(ab))


def _in_projection(x, ctx, sh, sc, norm_w, wq, wg, wab, wnq, wnkv, gpar, conv_w, cos_tab, sin_tab):
    B = x.shape[0]
    D = D_MODEL
    per = TILE // HALO
    tok = lambda n: pl.BlockSpec((None, TILE, n), lambda b, i: (b, i, 0))
    full = lambda a: pl.BlockSpec(a.shape, lambda b, i: (0,) * a.ndim)
    modrow = pl.BlockSpec((None, 1, D), lambda b, i: (jnp.where(i == 0, B, b), 0, 0))
    rope = pl.BlockSpec((TILE, LANES), lambda b, i: (i, 0))
    before = pl.BlockSpec((None, HALO, D), lambda b, i: (b, jnp.maximum((i - 1) * per - 1, 0), 0))
    after = pl.BlockSpec((None, HALO, D), lambda b, i: (b, jnp.minimum(jnp.maximum(i, 1) * per, SEQ // HALO - 1), 0))
    outs = [(3 * GDN_WIDTH, F32), (GDN_WIDTH, F32), (LANES, F32), (NA_WIDTH, BF16), (2 * NA_WIDTH, BF16)]
    return pl.pallas_call(
        _inproj_kernel,
        grid=(B, N_TILES),
        in_specs=[pl.BlockSpec((None, TILE, D), lambda b, i: (b, jnp.maximum(i - 1, 0), 0)),
                  pl.BlockSpec((None, TILE, D), lambda b, i: (b, 0, 0)), before, after,
                  modrow, modrow, full(norm_w), full(wq), full(wg), full(wab), full(wnq), full(wnkv), full(gpar),
                  full(conv_w), rope, rope],
        out_specs=[tok(n) for n, _ in outs],
        out_shape=[jax.ShapeDtypeStruct((B, T_ALL, n), dt) for n, dt in outs],
        scratch_shapes=[pltpu.VMEM((TILE + 2 * HALO, 3 * GDN_WIDTH), F32)],
        compiler_params=_params("parallel", "arbitrary"),
    )(x, ctx, x, x, sh, sc, norm_w, wq, wg, wab, wnq, wnkv, gpar, conv_w, cos_tab, sin_tab)


def _rope_tables():
    half = GDN_HEAD_DIM // 2
    pairs = half // 2
    t = np.arange(SEQ)
    inv_freq = jnp.asarray(ROPE_BASE, F32) ** (-jnp.arange(pairs, dtype=F32) / pairs)

    def tab(pos):
        ang = jnp.asarray(pos, F32)[:, None] * inv_freq[None, :]
        c, s = jnp.cos(ang), jnp.sin(ang)
        return jnp.concatenate([c, c], axis=-1), jnp.concatenate([-s, s], axis=-1)

    cr, sr = tab(t // GRID_W)
    cc, sc = tab(t % GRID_W)
    cos = jnp.concatenate([cr, cc], axis=-1)
    sin = jnp.concatenate([sr, sc], axis=-1)
    cos = jnp.concatenate([jnp.ones((CTX_LEN, LANES), F32), cos], axis=0)
    sin = jnp.concatenate([jnp.zeros((CTX_LEN, LANES), F32), sin], axis=0)
    return cos, sin


SCAN_SAMPLES = 8


def _gdn_scan_kernel(qf_ref, qb_ref, gf_ref, gb_ref, of_ref, ob_ref, s_ref):
    n = pl.program_id(1)

    @pl.when(n == 0)
    def _():
        s_ref[...] = jnp.zeros_like(s_ref)

    row = lax.broadcasted_iota(jnp.int32, (CHUNK, CHUNK), 0)
    col = lax.broadcasted_iota(jnp.int32, (CHUNK, CHUNK), 1)
    eye = jnp.where(row == col, 1.0, 0.0).astype(F32)
    chains = range(SCAN_SAMPLES * 2 * GDN_HEADS)
    bf = lambda t: t.astype(BF16)
    each = lambda f, *cols: [f(*args) for args in zip(*cols)]
    q, k, v, gcol, grow, gtot, bcol, incl, strict = [], [], [], [], [], [], [], [], []
    for s in range(SCAN_SAMPLES):
        for d in range(2):
            src = qf_ref if d == 0 else qb_ref
            gbt = (gf_ref if d == 0 else gb_ref)[s]
            inc = (row >= col) if d == 0 else (row <= col)
            cm = jnp.where(inc, 1.0, 0.0).astype(BF16)
            g1, g2, g3 = _split3(gbt)
            gc = _dot(cm, g1) + (_dot(cm, g2) + _dot(cm, g3))
            gct = gc.T
            tot = gc[CHUNK - 1:CHUNK, :] if d == 0 else gc[0:1, :]
            for h in range(GDN_HEADS):
                c = d * GDN_HEADS + h
                gcol.append(gc[:, c:c + 1])
                grow.append(gct[c:c + 1, :])
                gtot.append(tot[:, c:c + 1])
                bcol.append(gbt[:, 2 * GDN_HEADS + c:2 * GDN_HEADS + c + 1])
                q.append(src[s, :, h * LANES:(h + 1) * LANES])
                k.append(src[s, :, GDN_WIDTH + h * LANES:GDN_WIDTH + (h + 1) * LANES])
                v.append(src[s, :, 2 * GDN_WIDTH + h * LANES:2 * GDN_WIDTH + (h + 1) * LANES])
                incl.append(inc)
                strict.append((row > col) if d == 0 else (row < col))
    s_old = [s_ref[c] for c in chains]
    decay = each(lambda m, gc_, gr_: jnp.exp(jnp.where(m, gc_ - gr_, -jnp.inf)), incl, gcol, grow)
    eg = each(jnp.exp, gcol)
    kb = each(lambda k_, b_: k_ * b_, k, bcol)
    a = each(lambda kb_, q_, k_: _dot_nt(bf(jnp.concatenate([kb_, q_], axis=0)), bf(k_)), kb, q, k)
    lower = each(lambda m, a_, dc: jnp.where(m, a_[:CHUNK] * dc, 0.0), strict, a, decay)
    attn = each(lambda a_, dc: bf(a_[CHUNK:] * dc), a, decay)
    tinv = each(lambda l_: eye - l_, lower)
    m = lower
    for _ in range(int(math.log2(CHUNK)) - 1):
        m = each(lambda m_: _dot(bf(m_), bf(m_)), m)
        tinv = each(lambda t_, m_: t_ + _dot(bf(t_), bf(m_)), tinv, m)
    resid = each(lambda t_, l_: (eye - t_) - _dot3(l_, t_), tinv, lower)
    tinv = each(lambda t_, r_: t_ + _dot(bf(t_), bf(r_)), tinv, resid)
    uw = each(lambda t_, v_, b_, kb_, eg_: _dot(bf(t_), bf(jnp.concatenate([v_ * b_, kb_ * eg_], axis=1))),
              tinv, v, bcol, kb, eg)
    ws = each(lambda uw_, q_, eg_, s_: _dot(bf(jnp.concatenate([uw_[:, LANES:], q_ * eg_], axis=0)), bf(s_)),
              uw, q, eg, s_old)
    vb = each(lambda uw_, ws_: bf(uw_[:, :LANES] - ws_[:CHUNK]), uw, ws)
    o = each(lambda ws_, at_, vb_: ws_[CHUNK:] + _dot(at_, vb_), ws, attn, vb)
    s_new = each(lambda s_, gt_, k_, gc_, vb_: s_ * jnp.exp(gt_) + _dot_tn(bf(k_ * jnp.exp(gt_ - gc_)), vb_),
                 s_old, gtot, k, gcol, vb)
    for c in chains:
        s, d, h = c // (2 * GDN_HEADS), (c // GDN_HEADS) % 2, c % GDN_HEADS
        o_ref = of_ref if d == 0 else ob_ref
        o_ref[s, :, h * LANES:(h + 1) * LANES] = o[c]
        s_ref[c] = s_new[c]


def _gdn_scan(qkv, gb):
    B = qkv.shape[0]
    S = SCAN_SAMPLES
    assert B % S == 0
    fwd = lambda b, n: (b, n, 0)
    bwd = lambda b, n: (b, jnp.where(n < CTX_CHUNKS, CTX_CHUNKS - 1 - n, N_CHUNKS + CTX_CHUNKS - 1 - n), 0)
    out = jax.ShapeDtypeStruct((B, T_ALL, GDN_WIDTH), F32)
    return pl.pallas_call(
        _gdn_scan_kernel,
        grid=(B // S, N_CHUNKS),
        in_specs=[pl.BlockSpec((S, CHUNK, 3 * GDN_WIDTH), fwd),
                  pl.BlockSpec((S, CHUNK, 3 * GDN_WIDTH), bwd),
                  pl.BlockSpec((S, CHUNK, LANES), fwd),
                  pl.BlockSpec((S, CHUNK, LANES), bwd)],
        out_specs=[pl.BlockSpec((S, CHUNK, GDN_WIDTH), fwd),
                   pl.BlockSpec((S, CHUNK, GDN_WIDTH), bwd)],
        out_shape=[out, out],
        scratch_shapes=[pltpu.VMEM((S * 2 * GDN_HEADS, GDN_HEAD_DIM, GDN_HEAD_DIM), F32)],
        compiler_params=_params("parallel", "arbitrary"),
    )(qkv, qkv, gb, gb)


WIN_TOKENS = NA_ROWS * GRID_W
NA_STEP_ROWS = 4


def _window_start(r):
    return jnp.clip(r - NA_ROWS // 2, 0, GRID_ROWS - NA_ROWS)


def _na_kernel(q_ref, kv_ref, *rest):
    bias_refs, o_ref = rest[:NA_STEP_ROWS], rest[NA_STEP_ROWS]
    lane = lax.broadcasted_iota(jnp.int32, (GRID_W, LANES), 1)
    low = lane < NA_HEAD_DIM
    zero = jnp.zeros((GRID_W, LANES), BF16)
    each = lambda f, *cols: [f(*args) for args in zip(*cols)]
    rowmax = lambda t: jnp.max(t, axis=-1, keepdims=True)
    rowsum = lambda t: jnp.sum(t, axis=-1, keepdims=True)
    items = [(j, p) for j in range(NA_STEP_ROWS) for p in range(NA_HEADS // 2)]
    qrows = [slice(j * GRID_W, (j + 1) * GRID_W) for j, _ in items]
    klanes = [slice(p * LANES, (p + 1) * LANES) for _, p in items]
    vlanes = [slice(NA_WIDTH + s.start, NA_WIDTH + s.stop) for s in klanes]
    offs = [pl.multiple_of(CTX_LEN + _window_start(pl.program_id(1) * NA_STEP_ROWS + j) * GRID_W, GRID_W)
            for j in range(NA_STEP_ROWS)]
    off = [offs[j] for j, _ in items]
    q2 = each(lambda rw, s: jnp.concatenate([jnp.where(low, q_ref[rw, s], zero),
                                             jnp.where(low, zero, q_ref[rw, s])], axis=0), qrows, klanes)
    bias = [jnp.concatenate([bias_refs[j][2 * p], bias_refs[j][2 * p + 1]], axis=0) for j, p in items]
    s_lat = each(lambda q_, o_, s, b_: _dot_nt(q_, kv_ref[pl.ds(o_, WIN_TOKENS), s]) + b_, q2, off, klanes, bias)
    s_ctx = each(lambda q_, s: _dot_nt(q_, kv_ref[0:CTX_LEN, s]), q2, klanes)
    mx = each(lambda a, b: jnp.maximum(rowmax(a), rowmax(b)), s_lat, s_ctx)
    p_lat = each(lambda a, m: jnp.exp(a - m), s_lat, mx)
    p_ctx = each(lambda a, m: jnp.exp(a - m), s_ctx, mx)
    den = each(lambda a, b: rowsum(a) + rowsum(b), p_lat, p_ctx)
    o = each(lambda a, b, o_, s: _dot(a.astype(BF16), kv_ref[pl.ds(o_, WIN_TOKENS), s])
             + _dot(b.astype(BF16), kv_ref[0:CTX_LEN, s]), p_lat, p_ctx, off, vlanes)
    o = each(lambda o_, d_: o_ / d_, o, den)
    for rw, s, o_ in zip(qrows, klanes, o):
        o_ref[rw, s] = jnp.where(low, o_[:GRID_W], o_[GRID_W:]).astype(BF16)


def _neighbourhood_attention(naq, nakv, bias_tab):
    B = naq.shape[0]
    R = NA_STEP_ROWS
    step_tokens = R * GRID_W
    q_blocks_before = CTX_LEN // step_tokens

    def bias_spec(j):
        return pl.BlockSpec((None, NA_HEADS, GRID_W, WIN_TOKENS),
                            lambda b, s: (s * R + j - _window_start(s * R + j), 0, 0, 0))

    return pl.pallas_call(
        _na_kernel,
        grid=(B, GRID_ROWS // R),
        in_specs=[pl.BlockSpec((None, step_tokens, NA_WIDTH), lambda b, s: (b, s + q_blocks_before, 0)),
                  pl.BlockSpec((None, T_ALL, 2 * NA_WIDTH), lambda b, s: (b, 0, 0))]
                 + [bias_spec(j) for j in range(R)],
        out_specs=pl.BlockSpec((None, step_tokens, NA_WIDTH), lambda b, s: (b, s, 0)),
        out_shape=jax.ShapeDtypeStruct((B, SEQ, NA_WIDTH), BF16),
        compiler_params=_params("parallel", "arbitrary"),
    )(naq, nakv, *([bias_tab] * R))


def _na_bias_table(rpb):
    col = np.arange(GRID_W)
    col_start = np.clip(col - NA_COLS // 2, 0, GRID_W - NA_COLS)
    col_in = (col[None, :] >= col_start[:, None]) & (col[None, :] < col_start[:, None] + NA_COLS)
    col_idx = np.clip(col[None, :] - col[:, None], -(NA_COLS - 1), NA_COLS - 1) + (NA_COLS - 1)
    row_idx = np.arange(NA_ROWS)[None, :] - np.arange(NA_ROWS)[:, None] + (NA_ROWS - 1)
    row_sel = (row_idx[..., None] == np.arange(2 * NA_ROWS - 1)).astype(np.float32)
    col_sel = (col_idx[..., None] == np.arange(2 * NA_COLS - 1)).astype(np.float32)
    t = jnp.einsum('hrc,vjr,qkc->vhqjk', rpb.astype(F32), row_sel, col_sel, precision=lax.Precision.HIGHEST)
    t = jnp.where(col_in[None, None, :, None, :], t, -jnp.inf)
    return t.reshape(NA_ROWS, NA_HEADS, GRID_W, WIN_TOKENS)


def _outproj_kernel(of_ref, ob_ref, gate_ref, yna_ref, x_ref, gt1_ref, sh2_ref, sc2_ref, gn_ref, nf_ref,
                    wog_ref, won_ref, wr_ref, x1_ref, h2_ref, aff_ref):
    o = of_ref[...] + ob_ref[...]
    ys = []
    for h in range(GDN_HEADS):
        lanes = slice(h * LANES, (h + 1) * LANES)
        oh = o[:, lanes]
        yh = oh * lax.rsqrt(jnp.mean(oh * oh, axis=-1, keepdims=True) + EPS) * gn_ref[...]
        ys.append(yh * _silu(gate_ref[:, lanes]))
    yg = jnp.concatenate(ys, axis=1).astype(BF16)
    y = _dot(yg, wog_ref[...]) + _dot(yna_ref[...], won_ref[...])
    x1 = x_ref[...] + gt1_ref[...] * y
    x1_ref[...] = x1
    h2 = x1 * lax.rsqrt(jnp.mean(x1 * x1, axis=-1, keepdims=True) + EPS) * nf_ref[...]
    h2 = h2 * (1.0 + sc2_ref[...]) + sh2_ref[...]
    h2_ref[...] = h2
    hb = h2.astype(BF16)
    h_lo = (h2 - hb.astype(F32)).astype(BF16)
    wr_hi, wr_lo = _split2(wr_ref[...])
    logits = _dot_nt(wr_hi, hb) + (_dot_nt(wr_hi, h_lo) + _dot_nt(wr_lo, hb))
    e = jnp.exp(logits - jnp.max(logits, axis=0, keepdims=True))
    aff_ref[...] = e / jnp.sum(e, axis=0, keepdims=True)


def _out_projection(o_f, o_b, gate, y_na, x, gt1, sh2, sc2, gdn_norm, norm_ffn, wog, won, wrt):
    B = x.shape[0]
    D = D_MODEL
    lat = lambda n: pl.BlockSpec((None, TILE, n), lambda b, i: (b, i + 1, 0))
    tok = lambda n: pl.BlockSpec((None, TILE, n), lambda b, i: (b, i, 0))
    full = lambda a: pl.BlockSpec(a.shape, lambda b, i: (0,) * a.ndim)
    modrow = pl.BlockSpec((None, 1, D), lambda b, i: (b, 0, 0))
    return pl.pallas_call(
        _outproj_kernel,
        grid=(B, SEQ // TILE),
        in_specs=[lat(GDN_WIDTH), lat(GDN_WIDTH), lat(GDN_WIDTH), tok(NA_WIDTH), tok(D),
                  modrow, modrow, modrow, full(gdn_norm), full(norm_ffn), full(wog), full(won), full(wrt)],
        out_specs=[tok(D), tok(D),
                   pl.BlockSpec((None, N_EXPERTS, TILE), lambda b, i: (b, 0, i))],
        out_shape=[jax.ShapeDtypeStruct((B, SEQ, D), F32),
                   jax.ShapeDtypeStruct((B, SEQ, D), F32),
                   jax.ShapeDtypeStruct((B, N_EXPERTS, SEQ), F32)],
        compiler_params=_params("parallel", "arbitrary"),
    )(o_f, o_b, gate, y_na, x, gt1, sh2, sc2, gdn_norm, norm_ffn, wog, won, wrt)


TOK_BLOCKS = SEQ // LANES
ROUTE_ROWS = N_EXPERTS * TOK_BLOCKS


def _topk_kernel(aff_ref, idx_ref, gval_ref):
    a = aff_ref[...]
    bf = lambda t: t.astype(BF16)
    mask = lambda c: jnp.where(c, 1.0, 0.0).astype(BF16)
    li = lax.broadcasted_iota(jnp.int32, (LANES, LANES), 0)
    lj = lax.broadcasted_iota(jnp.int32, (LANES, LANES), 1)
    ones = jnp.ones((LANES, LANES), BF16)
    upper = mask(li <= lj)
    ri = lax.broadcasted_iota(jnp.int32, (ROUTE_ROWS, ROUTE_ROWS), 0)
    rj = lax.broadcasted_iota(jnp.int32, (ROUTE_ROWS, ROUTE_ROWS), 1)
    shift = TOK_BLOCKS.bit_length() - 1
    same = (ri >> shift) == (rj >> shift)
    expert_sum = mask(same)
    rows_before = mask(jnp.logical_and(same, rj < ri))
    per_expert = lambda m: _dot(expert_sum, bf(_dot(m, ones)))

    def bisect(it, prefix):
        cand = prefix | (jnp.int32(1) << (30 - it))
        cnt = per_expert(mask(a >= pltpu.bitcast(cand, F32)))
        return jnp.where(cnt >= CAPACITY, cand, prefix)

    thr = lax.fori_loop(0, 31, bisect, jnp.zeros((ROUTE_ROWS, LANES), jnp.int32))
    gt = a >= pltpu.bitcast(thr + 1, F32)
    eq = jnp.logical_and(a >= pltpu.bitcast(thr, F32), jnp.logical_not(gt))
    need = CAPACITY - per_expert(mask(gt))

    def prefix(m):
        return _dot(m, upper), _dot(rows_before, bf(_dot(m, ones)))

    eq_m = mask(eq)
    eq_in_row, eq_rows_before = prefix(eq_m)
    eq_before = eq_in_row + eq_rows_before - eq_m.astype(F32)
    sel = jnp.logical_or(gt, jnp.logical_and(eq, eq_before < need))
    sel_m = mask(sel)
    rank_in_row, start = prefix(sel_m)
    local = jnp.where(sel, rank_in_row, 0.0)
    row_total = _dot(sel_m, ones)
    lane_f = lax.broadcasted_iota(jnp.int32, (CAPACITY, LANES), 1).astype(F32)
    slot = lax.broadcasted_iota(jnp.int32, (CAPACITY, LANES), 0).astype(F32)
    tok0 = (lax.broadcasted_iota(jnp.int32, (TOK_BLOCKS, LANES), 0) * LANES).astype(F32)
    tok0_hi, tok0_lo = _split2(tok0)
    ones_cl = jnp.ones((CAPACITY, LANES), BF16)
    pad = lambda t: jnp.concatenate([t, jnp.zeros((LANES - TOK_BLOCKS, LANES), t.dtype)], axis=0)

    def as_row(v):
        hi, lo = _split2(jnp.where(li == lj, pad(v), 0.0))
        return _dot(ones_cl, hi) + _dot(ones_cl, lo)

    for e in range(N_EXPERTS):
        rows = slice(e * TOK_BLOCKS, (e + 1) * TOK_BLOCKS)
        st, tot = start[rows], row_total[rows]
        st_r, tot_r = as_row(st), as_row(tot)
        owner = mask(jnp.logical_and(st_r <= slot, slot < st_r + tot_r))
        pick = lambda t: _dot(owner, pad(t))
        st_hi, st_lo = _split2(st)
        a_hi, a_mid, a_lo = _split3(a[rows])
        want = slot - (pick(st_hi) + pick(st_lo)) + 1.0
        hit = pick(bf(local[rows])) == want
        tok = jnp.sum(jnp.where(hit, lane_f, 0.0), axis=-1, keepdims=True)
        base = pick(tok0_hi) + pick(tok0_lo)
        val = pick(a_hi) + (pick(a_mid) + pick(a_lo))
        idx_ref[e] = (tok + base[:, 0:1]).astype(jnp.int32)
        gval_ref[e] = jnp.sum(jnp.where(hit, val, 0.0), axis=-1, keepdims=True)


def _route(aff_t):
    B = aff_t.shape[0]
    out = lambda dt: jax.ShapeDtypeStruct((B, N_EXPERTS, CAPACITY, 1), dt)
    spec = pl.BlockSpec((None, N_EXPERTS, CAPACITY, 1), lambda b: (b, 0, 0, 0))
    return pl.pallas_call(
        _topk_kernel,
        grid=(B,),
        in_specs=[pl.BlockSpec((None, ROUTE_ROWS, LANES), lambda b: (b, 0, 0))],
        out_specs=[spec, spec],
        out_shape=[out(jnp.int32), out(F32)],
        compiler_params=_params("parallel"),
    )(aff_t.reshape(B, ROUTE_ROWS, LANES))


GATHER_UNROLL = 8


def _dispatch_kernel(idx_ref, h_ref, xe_ref, rows_ref):
    base = (pl.program_id(0) * N_EXPERTS + pl.program_id(1)) * CAPACITY

    def body(r, _):
        t = idx_ref[base + r]
        rows_ref[pl.ds(r, 1), :] = h_ref[pl.ds(t, 1), :]
        return 0

    lax.fori_loop(0, CAPACITY, body, 0, unroll=GATHER_UNROLL)
    xe_ref[...] = rows_ref[...].astype(BF16)


def _dispatch(idx_flat, h2):
    B = h2.shape[0]
    D = D_MODEL
    return pl.pallas_call(
        _dispatch_kernel,
        grid_spec=pltpu.PrefetchScalarGridSpec(
            num_scalar_prefetch=1,
            grid=(B, N_EXPERTS),
            in_specs=[pl.BlockSpec((None, SEQ, D), lambda b, e, idx: (b, 0, 0))],
            out_specs=pl.BlockSpec((None, None, CAPACITY, D), lambda b, e, idx: (b, e, 0, 0)),
            scratch_shapes=[pltpu.VMEM((CAPACITY, D), F32)]),
        out_shape=jax.ShapeDtypeStruct((B, N_EXPERTS, CAPACITY, D), BF16),
        compiler_params=_params("parallel", "arbitrary"),
    )(idx_flat, h2)


def _ffn_kernel(xe_ref, gv_ref, wg_ref, wu_ref, wd_ref, ye_ref, wgb_ref, wub_ref, wdb_ref):
    @pl.when(pl.program_id(1) == 0)
    def _():
        wgb_ref[...] = wg_ref[...].astype(BF16)
        wub_ref[...] = wu_ref[...].astype(BF16)
        wdb_ref[...] = wd_ref[...].astype(BF16)

    xe = xe_ref[...]
    hid = (_silu(_dot(xe, wgb_ref[...])) * _dot(xe, wub_ref[...])).astype(BF16)
    ye_ref[...] = _dot(hid, wdb_ref[...]) * gv_ref[...]


def _expert_ffn(xe, gval, w_gate, w_up, w_down):
    B = xe.shape[0]
    D = D_MODEL
    F = w_gate.shape[-1]
    return pl.pallas_call(
        _ffn_kernel,
        grid=(N_EXPERTS, B),
        in_specs=[pl.BlockSpec((None, None, CAPACITY, D), lambda e, b: (b, e, 0, 0)),
                  pl.BlockSpec((None, None, CAPACITY, 1), lambda e, b: (b, e, 0, 0)),
                  pl.BlockSpec((None, D, F), lambda e, b: (e, 0, 0)),
                  pl.BlockSpec((None, D, F), lambda e, b: (e, 0, 0)),
                  pl.BlockSpec((None, F, D), lambda e, b: (e, 0, 0))],
        out_specs=pl.BlockSpec((None, None, CAPACITY, D), lambda e, b: (b, e, 0, 0)),
        out_shape=jax.ShapeDtypeStruct((B, N_EXPERTS, CAPACITY, D), F32),
        scratch_shapes=[pltpu.VMEM((D, F), BF16), pltpu.VMEM((D, F), BF16), pltpu.VMEM((F, D), BF16)],
        compiler_params=_params("arbitrary", "arbitrary"),
    )(xe, gval, w_gate, w_up, w_down)


HALF_E = N_EXPERTS // 2
OUT_TILES = SEQ // TILE


def _combine_kernel(idx_ref, ye0_ref, ye1_ref, x1_ref, gt2_ref, fw_ref, o_ref, acc0_ref, acc1_ref):
    b = pl.program_id(0)
    s = pl.program_id(1)

    @pl.when(s == 0)
    def _():
        acc0_ref[...] = jnp.zeros_like(acc0_ref)
        acc1_ref[...] = jnp.zeros_like(acc1_ref)

    @pl.when(s < HALF_E)
    def _():
        base0 = (b * N_EXPERTS + s) * CAPACITY
        base1 = base0 + HALF_E * CAPACITY

        def body(r, _):
            t0 = idx_ref[base0 + r]
            t1 = idx_ref[base1 + r]
            acc0_ref[pl.ds(t0, 1), :] = acc0_ref[pl.ds(t0, 1), :] + ye0_ref[pl.ds(r, 1), :]
            acc1_ref[pl.ds(t1, 1), :] = acc1_ref[pl.ds(t1, 1), :] + ye1_ref[pl.ds(r, 1), :]
            return 0

        lax.fori_loop(0, CAPACITY, body, 0, unroll=GATHER_UNROLL)

    @pl.when(s >= HALF_E)
    def _():
        t = pl.multiple_of((s - HALF_E) * TILE, TILE)
        moe = acc0_ref[pl.ds(t, TILE), :] + acc1_ref[pl.ds(t, TILE), :]
        x2 = x1_ref[...] + gt2_ref[...] * moe
        o_ref[...] = x2 * lax.rsqrt(jnp.mean(x2 * x2, axis=-1, keepdims=True) + EPS) * fw_ref[...]


def _combine(idx_flat, ye, x1, gt2, final_norm):
    B = ye.shape[0]
    D = D_MODEL
    expert = lambda off: pl.BlockSpec((None, None, CAPACITY, D),
                                      lambda b, s, idx: (b, jnp.minimum(s, HALF_E - 1) + off, 0, 0))
    tile = pl.BlockSpec((None, TILE, D), lambda b, s, idx: (b, jnp.maximum(s - HALF_E, 0), 0))
    return pl.pallas_call(
        _combine_kernel,
        grid_spec=pltpu.PrefetchScalarGridSpec(
            num_scalar_prefetch=1,
            grid=(B, HALF_E + OUT_TILES),
            in_specs=[expert(0), expert(HALF_E), tile,
                      pl.BlockSpec((None, 1, D), lambda b, s, idx: (b, 0, 0)),
                      pl.BlockSpec((1, D), lambda b, s, idx: (0, 0))],
            out_specs=tile,
            scratch_shapes=[pltpu.VMEM((SEQ, D), F32), pltpu.VMEM((SEQ, D), F32)]),
        out_shape=jax.ShapeDtypeStruct((B, SEQ, D), F32),
        compiler_params=_params("parallel", "arbitrary"),
    )(idx_flat, ye, ye, x1, gt2, final_norm)


def kernel(x, c, ctx, c_ctx, w_mod, b_mod, norm_mix, norm_ffn, w_in, conv_qkv, a_log, dt_bias, gdn_norm, na_rpb,
           w_out, w_router, w_gate, w_up, w_down, final_norm):
    B, T, D = x.shape
    assert (T, D) == (SEQ, D_MODEL) and ctx.shape == (B, CTX_LEN, D) and w_mod.shape[0] == 1
    li = 0
    mod_rows = -(-(B + 1) // SUBLANES) * SUBLANES
    cc = jnp.zeros((mod_rows, D), F32).at[:B].set(c).at[B].set(c_ctx)
    mod = _modulation(cc, w_mod[li], b_mod[li])
    part = lambda j, rows: mod[:rows, j * D:(j + 1) * D].reshape(rows, 1, D)
    sh1, sc1 = part(0, B + 1), part(1, B + 1)
    gt1, sh2, sc2, gt2 = part(2, B), part(3, B), part(4, B), part(5, B)

    wi = w_in[li]
    q_end, g_end = 3 * GDN_WIDTH, 4 * GDN_WIDTH
    ab_end = g_end + 4 * GDN_HEADS
    wq = wi[:, :q_end].astype(BF16)
    wg = wi[:, q_end:g_end].astype(BF16)
    wab = jnp.zeros((D, LANES), F32).at[:, :4 * GDN_HEADS].set(wi[:, g_end:ab_end]).astype(BF16)
    wnq = wi[:, ab_end:ab_end + NA_WIDTH].astype(BF16)
    wnkv = wi[:, ab_end + NA_WIDTH:].astype(BF16)
    gpar = jnp.zeros((2, LANES), F32)
    gpar = gpar.at[0, :2 * GDN_HEADS].set(-jnp.exp(a_log[li].astype(F32)).reshape(-1))
    gpar = gpar.at[1, :2 * GDN_HEADS].set(dt_bias[li].astype(F32).reshape(-1))

    cos_tab, sin_tab = _rope_tables()
    qkv, gate, gb, naq, nakv = _in_projection(x, ctx, sh1, sc1, norm_mix[li].reshape(1, D),
                                              wq, wg, wab, wnq, wnkv, gpar, conv_qkv[li], cos_tab, sin_tab)
    o_f, o_b = _gdn_scan(qkv, gb)
    y_na = _neighbourhood_attention(naq, nakv, _na_bias_table(na_rpb[li]))

    wo = w_out[li].astype(BF16)
    x1, h2, aff_t = _out_projection(o_f, o_b, gate, y_na, x, gt1, sh2, sc2,
                                     gdn_norm[li].reshape(1, GDN_HEAD_DIM), norm_ffn[li].reshape(1, D),
                                     wo[:GDN_WIDTH], wo[GDN_WIDTH:], w_router[li].T)
    idx, gval = _route(aff_t)
    idx_flat = idx.reshape(-1)
    xe = _dispatch(idx_flat, h2)
    ye = _expert_ffn(xe, gval, w_gate[li], w_up[li], w_down[li])
    return _combine(idx_flat, ye, x1, gt2, final_norm.reshape(1, D))
```

```python
import math

import numpy as np
import jax
import jax.numpy as jnp
from jax import lax
from jax.experimental import pallas as pl
from jax.experimental.pallas import tpu as pltpu

F32 = jnp.float32
BF16 = jnp.bfloat16

D_MODEL = 1024
SEQ = 4096
CTX_LEN = 256
GRID_W = 64
GRID_ROWS = SEQ // GRID_W
GDN_HEADS = 4
GDN_HEAD_DIM = 128
GDN_WIDTH = GDN_HEADS * GDN_HEAD_DIM
CONV_W = 5
CHUNK = 64
ROPE_BASE = 10000.0
NA_HEADS = 8
NA_HEAD_DIM = 64
NA_WIDTH = NA_HEADS * NA_HEAD_DIM
NA_ROWS = 8
NA_COLS = 16
N_EXPERTS = 16
CAPACITY = 2 * SEQ // N_EXPERTS
EPS = 1e-6

TILE = 256
T_ALL = CTX_LEN + SEQ
N_TILES = T_ALL // TILE
N_CHUNKS = T_ALL // CHUNK
CTX_CHUNKS = CTX_LEN // CHUNK
LANES = 128
SUBLANES = 8
VMEM_LIMIT = 56 * 1024 * 1024


def _params(*sem):
    return pltpu.CompilerParams(dimension_semantics=sem, vmem_limit_bytes=VMEM_LIMIT)


def _dot(a, b):
    return jnp.dot(a, b, preferred_element_type=F32)


def _dot_nt(a, b):
    return lax.dot_general(a, b, (((1,), (1,)), ((), ())), preferred_element_type=F32)


def _dot_tn(a, b):
    return lax.dot_general(a, b, (((0,), (0,)), ((), ())), preferred_element_type=F32)


def _split2(a):
    hi = a.astype(BF16)
    lo = (a - hi.astype(F32)).astype(BF16)
    return hi, lo


def _split3(a):
    hi = a.astype(BF16)
    r = a - hi.astype(F32)
    mid = r.astype(BF16)
    lo = (r - mid.astype(F32)).astype(BF16)
    return hi, mid, lo


def _dot3(a, b):
    ah, al = _split2(a)
    bh, bl = _split2(b)
    return _dot(ah, bh) + (_dot(ah, bl) + _dot(al, bh))


def _silu(x):
    return x * jax.nn.sigmoid(x)


def _mod_kernel(c_ref, w_ref, b_ref, o_ref):
    s = _silu(c_ref[...])
    o_ref[...] = _dot3(s, w_ref[...]) + b_ref[...]


def _modulation(cc, w_mod, b_mod):
    rows, d = cc.shape
    n = w_mod.shape[1]
    bn = 1024
    return pl.pallas_call(
        _mod_kernel,
        grid=(n // bn,),
        in_specs=[pl.BlockSpec((rows, d), lambda j: (0, 0)),
                  pl.BlockSpec((d, bn), lambda j: (0, j)),
                  pl.BlockSpec((1, bn), lambda j: (0, j))],
        out_specs=pl.BlockSpec((rows, bn), lambda j: (0, j)),
        out_shape=jax.ShapeDtypeStruct((rows, n), F32),
        compiler_params=_params("parallel"),
    )(cc, w_mod, b_mod.reshape(1, n))


HALO = SUBLANES


def _inproj_kernel(x_ref, ctx_ref, xp_ref, xn_ref, sh_ref, sc_ref, nw_ref, wq_ref, wg_ref, wab_ref, wnq_ref,
                   wnkv_ref, gpar_ref, cw_ref, cos_ref, sin_ref, qkv_ref, gate_ref, gb_ref, naq_ref, nakv_ref, ext_ref):
    i = pl.program_id(1)
    has_prev = i >= 2
    has_next = jnp.logical_and(i >= 1, i < N_TILES - 1)
    xt = jnp.concatenate([xp_ref[...], jnp.where(i == 0, ctx_ref[...], x_ref[...]), xn_ref[...]], axis=0)
    ms = jnp.mean(xt * xt, axis=-1, keepdims=True)
    h = xt * lax.rsqrt(ms + EPS) * nw_ref[...]
    h = h * (1.0 + sc_ref[...]) + sh_ref[...]
    pq = _dot(h.astype(BF16), wq_ref[...])
    ext_ref[0:HALO, :] = jnp.where(has_prev, pq[0:HALO], 0.0)
    ext_ref[HALO:HALO + TILE, :] = pq[HALO:HALO + TILE]
    ext_ref[HALO + TILE:, :] = jnp.where(has_next, pq[HALO + TILE:], 0.0)
    hb = h[HALO:HALO + TILE].astype(BF16)
    gate_ref[...] = _dot(hb, wg_ref[...])
    first = HALO - CONV_W // 2
    acc = ext_ref[first:first + TILE, :] * cw_ref[0:1, :]
    for k in range(1, CONV_W):
        acc = acc + ext_ref[first + k:first + k + TILE, :] * cw_ref[k:k + 1, :]
    naq_ref[...] = (_dot(hb, wnq_ref[...]) * (NA_HEAD_DIM ** -0.5)).astype(BF16)
    y = _silu(acc)
    nakv_ref[...] = _dot(hb, wnkv_ref[...]).astype(BF16)
    cos = cos_ref[...]
    sin = sin_ref[...]
    lane = lax.broadcasted_iota(jnp.int32, (TILE, LANES), 1)
    take_upper = (lane % (GDN_HEAD_DIM // 2)) < (GDN_HEAD_DIM // 4)
    for j in range(2 * GDN_HEADS):
        t = y[:, j * LANES:(j + 1) * LANES]
        t = t * lax.rsqrt(jnp.sum(t * t, axis=-1, keepdims=True) + EPS)
        partner = jnp.where(take_upper, pltpu.roll(t, LANES - GDN_HEAD_DIM // 4, 1),
                            pltpu.roll(t, GDN_HEAD_DIM // 4, 1))
        t = t * cos + partner * sin
        if j < GDN_HEADS:
            t = t * (GDN_HEAD_DIM ** -0.5)
        qkv_ref[:, j * LANES:(j + 1) * LANES] = t
    qkv_ref[:, 2 * GDN_WIDTH:] = y[:, 2 * GDN_WIDTH:]
    ab = _dot(hb, wab_ref[...])
    z = ab + gpar_ref[1:2, :]
    softplus = jnp.maximum(z, 0.0) + jnp.log1p(jnp.exp(-jnp.abs(z)))
    lane = lax.broadcasted_iota(jnp.int32, ab.shape, 1)
    gb_ref[...] = jnp.where(lane < 2 * GDN_HEADS, gpar_ref[0:1, :] * softplus, jax.nn.sigmoid(ab))


def _in_projection(x, ctx, sh, sc, norm_w, wq, wg, wab, wnq, wnkv, gpar, conv_w, cos_tab, sin_tab):
    B = x.shape[0]
    D = D_MODEL
    per = TILE // HALO
    tok = lambda n: pl.BlockSpec((None, TILE, n), lambda b, i: (b, i, 0))
    full = lambda a: pl.BlockSpec(a.shape, lambda b, i: (0,) * a.ndim)
    modrow = pl.BlockSpec((None, 1, D), lambda b, i: (jnp.where(i == 0, B, b), 0, 0))
    rope = pl.BlockSpec((TILE, LANES), lambda b, i: (i, 0))
    before = pl.BlockSpec((None, HALO, D), lambda b, i: (b, jnp.maximum((i - 1) * per - 1, 0), 0))
    after = pl.BlockSpec((None, HALO, D), lambda b, i: (b, jnp.minimum(jnp.maximum(i, 1) * per, SEQ // HALO - 1), 0))
    outs = [(3 * GDN_WIDTH, F32), (GDN_WIDTH, F32), (LANES, F32), (NA_WIDTH, BF16), (2 * NA_WIDTH, BF16)]
    return pl.pallas_call(
        _inproj_kernel,
        grid=(B, N_TILES),
        in_specs=[pl.BlockSpec((None, TILE, D), lambda b, i: (b, jnp.maximum(i - 1, 0), 0)),
                  pl.BlockSpec((None, TILE, D), lambda b, i: (b, 0, 0)), before, after,
                  modrow, modrow, full(norm_w), full(wq), full(wg), full(wab), full(wnq), full(wnkv), full(gpar),
                  full(conv_w), rope, rope],
        out_specs=[tok(n) for n, _ in outs],
        out_shape=[jax.ShapeDtypeStruct((B, T_ALL, n), dt) for n, dt in outs],
        scratch_shapes=[pltpu.VMEM((TILE + 2 * HALO, 3 * GDN_WIDTH), F32)],
        compiler_params=_params("parallel", "arbitrary"),
    )(x, ctx, x, x, sh, sc, norm_w, wq, wg, wab, wnq, wnkv, gpar, conv_w, cos_tab, sin_tab)


def _rope_tables():
    half = GDN_HEAD_DIM // 2
    pairs = half // 2
    t = np.arange(SEQ)
    inv_freq = jnp.asarray(ROPE_BASE, F32) ** (-jnp.arange(pairs, dtype=F32) / pairs)

    def tab(pos):
        ang = jnp.asarray(pos, F32)[:, None] * inv_freq[None, :]
        c, s = jnp.cos(ang), jnp.sin(ang)
        return jnp.concatenate([c, c], axis=-1), jnp.concatenate([-s, s], axis=-1)

    cr, sr = tab(t // GRID_W)
    cc, sc = tab(t % GRID_W)
    cos = jnp.concatenate([cr, cc], axis=-1)
    sin = jnp.concatenate([sr, sc], axis=-1)
    cos = jnp.concatenate([jnp.ones((CTX_LEN, LANES), F32), cos], axis=0)
    sin = jnp.concatenate([jnp.zeros((CTX_LEN, LANES), F32), sin], axis=0)
    return cos, sin


SCAN_SAMPLES = 8


def _gdn_scan_kernel(qf_ref, qb_ref, gf_ref, gb_ref, of_ref, ob_ref, s_ref):
    n = pl.program_id(1)

    @pl.when(n == 0)
    def _():
        s_ref[...] = jnp.zeros_like(s_ref)

    row = lax.broadcasted_iota(jnp.int32, (CHUNK, CHUNK), 0)
    col = lax.broadcasted_iota(jnp.int32, (CHUNK, CHUNK), 1)
    eye = jnp.where(row == col, 1.0, 0.0).astype(F32)
    chains = range(SCAN_SAMPLES * 2 * GDN_HEADS)
    bf = lambda t: t.astype(BF16)
    each = lambda f, *cols: [f(*args) for args in zip(*cols)]
    q, k, v, gcol, grow, gtot, bcol, incl, strict = [], [], [], [], [], [], [], [], []
    for s in range(SCAN_SAMPLES):
        for d in range(2):
            src = qf_ref if d == 0 else qb_ref
            gbt = (gf_ref if d == 0 else gb_ref)[s]
            inc = (row >= col) if d == 0 else (row <= col)
            cm = jnp.where(inc, 1.0, 0.0).astype(BF16)
            g1, g2, g3 = _split3(gbt)
            gc = _dot(cm, g1) + (_dot(cm, g2) + _dot(cm, g3))
            gct = gc.T
            tot = gc[CHUNK - 1:CHUNK, :] if d == 0 else gc[0:1, :]
            for h in range(GDN_HEADS):
                c = d * GDN_HEADS + h
                gcol.append(gc[:, c:c + 1])
                grow.append(gct[c:c + 1, :])
                gtot.append(tot[:, c:c + 1])
                bcol.append(gbt[:, 2 * GDN_HEADS + c:2 * GDN_HEADS + c + 1])
                q.append(src[s, :, h * LANES:(h + 1) * LANES])
                k.append(src[s, :, GDN_WIDTH + h * LANES:GDN_WIDTH + (h + 1) * LANES])
                v.append(src[s, :, 2 * GDN_WIDTH + h * LANES:2 * GDN_WIDTH + (h + 1) * LANES])
                incl.append(inc)
                strict.append((row > col) if d == 0 else (row < col))
    s_old = [s_ref[c] for c in chains]
    decay = each(lambda m, gc_, gr_: jnp.exp(jnp.where(m, gc_ - gr_, -jnp.inf)), incl, gcol, grow)
    eg = each(jnp.exp, gcol)
    kb = each(lambda k_, b_: k_ * b_, k, bcol)
    a = each(lambda kb_, q_, k_: _dot_nt(bf(jnp.concatenate([kb_, q_], axis=0)), bf(k_)), kb, q, k)
    lower = each(lambda m, a_, dc: jnp.where(m, a_[:CHUNK] * dc, 0.0), strict, a, decay)
    attn = each(lambda a_, dc: bf(a_[CHUNK:] * dc), a, decay)
    tinv = each(lambda l_: eye - l_, lower)
    m = lower
    for _ in range(int(math.log2(CHUNK)) - 1):
        m = each(lambda m_: _dot(bf(m_), bf(m_)), m)
        tinv = each(lambda t_, m_: t_ + _dot(bf(t_), bf(m_)), tinv, m)
    resid = each(lambda t_, l_: (eye - t_) - _dot3(l_, t_), tinv, lower)
    tinv = each(lambda t_, r_: t_ + _dot(bf(t_), bf(r_)), tinv, resid)
    uw = each(lambda t_, v_, b_, kb_, eg_: _dot(bf(t_), bf(jnp.concatenate([v_ * b_, kb_ * eg_], axis=1))),
              tinv, v, bcol, kb, eg)
    ws = each(lambda uw_, q_, eg_, s_: _dot(bf(jnp.concatenate([uw_[:, LANES:], q_ * eg_], axis=0)), bf(s_)),
              uw, q, eg, s_old)
    vb = each(lambda uw_, ws_: bf(uw_[:, :LANES] - ws_[:CHUNK]), uw, ws)
    o = each(lambda ws_, at_, vb_: ws_[CHUNK:] + _dot(at_, vb_), ws, attn, vb)
    s_new = each(lambda s_, gt_, k_, gc_, vb_: s_ * jnp.exp(gt_) + _dot_tn(bf(k_ * jnp.exp(gt_ - gc_)), vb_),
                 s_old, gtot, k, gcol, vb)
    for c in chains:
        s, d, h = c // (2 * GDN_HEADS), (c // GDN_HEADS) % 2, c % GDN_HEADS
        o_ref = of_ref if d == 0 else ob_ref
        o_ref[s, :, h * LANES:(h + 1) * LANES] = o[c]
        s_ref[c] = s_new[c]


def _gdn_scan(qkv, gb):
    B = qkv.shape[0]
    S = SCAN_SAMPLES
    assert B % S == 0
    fwd = lambda b, n: (b, n, 0)
    bwd = lambda b, n: (b, jnp.where(n < CTX_CHUNKS, CTX_CHUNKS - 1 - n, N_CHUNKS + CTX_CHUNKS - 1 - n), 0)
    out = jax.ShapeDtypeStruct((B, T_ALL, GDN_WIDTH), F32)
    return pl.pallas_call(
        _gdn_scan_kernel,
        grid=(B // S, N_CHUNKS),
        in_specs=[pl.BlockSpec((S, CHUNK, 3 * GDN_WIDTH), fwd),
                  pl.BlockSpec((S, CHUNK, 3 * GDN_WIDTH), bwd),
                  pl.BlockSpec((S, CHUNK, LANES), fwd),
                  pl.BlockSpec((S, CHUNK, LANES), bwd)],
        out_specs=[pl.BlockSpec((S, CHUNK, GDN_WIDTH), fwd),
                   pl.BlockSpec((S, CHUNK, GDN_WIDTH), bwd)],
        out_shape=[out, out],
        scratch_shapes=[pltpu.VMEM((S * 2 * GDN_HEADS, GDN_HEAD_DIM, GDN_HEAD_DIM), F32)],
        compiler_params=_params("parallel", "arbitrary"),
    )(qkv, qkv, gb, gb)


WIN_TOKENS = NA_ROWS * GRID_W
NA_STEP_ROWS = 4


def _window_start(r):
    return jnp.clip(r - NA_ROWS // 2, 0, GRID_ROWS - NA_ROWS)


def _na_kernel(q_ref, kv_ref, *rest):
    bias_refs, o_ref = rest[:NA_STEP_ROWS], rest[NA_STEP_ROWS]
    lane = lax.broadcasted_iota(jnp.int32, (GRID_W, LANES), 1)
    low = lane < NA_HEAD_DIM
    zero = jnp.zeros((GRID_W, LANES), BF16)
    each = lambda f, *cols: [f(*args) for args in zip(*cols)]
    rowmax = lambda t: jnp.max(t, axis=-1, keepdims=True)
    rowsum = lambda t: jnp.sum(t, axis=-1, keepdims=True)
    items =[(j, p) for j in range(NA_STEP_ROWS) for p in range(NA_HEADS // 2)]
    qrows = [slice(j * GRID_W, (j + 1) * GRID_W) for j, _ in items]
    klanes = [slice(p * LANES, (p + 1) * LANES) for _, p in items]
    vlanes = [slice(NA_WIDTH + s.start, NA_WIDTH + s.stop) for s in klanes]
    offs = [pl.multiple_of(CTX_LEN + _window_start(pl.program_id(1) * NA_STEP_ROWS + j) * GRID_W, GRID_W)
            for j in range(NA_STEP_ROWS)]
    off = [offs[j] for j, _ in items]
    q2 = each(lambda rw, s: jnp.concatenate([jnp.where(low, q_ref[rw, s], zero),
                                             jnp.where(low, zero, q_ref[rw, s])], axis=0), qrows, klanes)
    bias = [jnp.concatenate([bias_refs[j][2 * p], bias_refs[j][2 * p + 1]], axis=0) for j, p in items]
    s_lat = each(lambda q_, o_, s, b_: _dot_nt(q_, kv_ref[pl.ds(o_, WIN_TOKENS), s]) + b_, q2, off, klanes, bias)
    s_ctx = each(lambda q_, s: _dot_nt(q_, kv_ref[0:CTX_LEN, s]), q2, klanes)
    mx = each(lambda a, b: jnp.maximum(rowmax(a), rowmax(b)), s_lat, s_ctx)
    p_lat = each(lambda a, m: jnp.exp(a - m), s_lat, mx)
    p_ctx = each(lambda a, m: jnp.exp(a - m), s_ctx, mx)
    den = each(lambda a, b: rowsum(a) + rowsum(b), p_lat, p_ctx)
    o = each(lambda a, b, o_, s: _dot(a.astype(BF16), kv_ref[pl.ds(o_, WIN_TOKENS), s])
             + _dot(b.astype(BF16), kv_ref[0:CTX_LEN, s]), p_lat, p_ctx, off, vlanes)
    o = each(lambda o_, d_: o_ / d_, o, den)
    for rw, s, o_ in zip(qrows, klanes, o):
        o_ref[rw, s] = jnp.where(low, o_[:GRID_W], o_[GRID_W:]).astype(BF16)


def _neighbourhood_attention(naq, nakv, bias_tab):
    B = naq.shape[0]
    R = NA_STEP_ROWS
    step_tokens = R * GRID_W
    assert GRID_ROWS % R == 0 and CTX_LEN % step_tokens == 0
    q_blocks_before = CTX_LEN // step_tokens

    def bias_spec(j):
        return pl.BlockSpec((None, NA_HEADS, GRID_W, WIN_TOKENS),
                            lambda b, s: (s * R + j - _window_start(s * R + j), 0, 0, 0))

    return pl.pallas_call(
        _na_kernel,
        grid=(B, GRID_ROWS // R),
        in_specs=[pl.BlockSpec((None, step_tokens, NA_WIDTH), lambda b, s: (b, s + q_blocks_before, 0)),
                  pl.BlockSpec((None, T_ALL, 2 * NA_WIDTH), lambda b, s: (b, 0, 0))]
                 + [bias_spec(j) for j in range(R)],
        out_specs=pl.BlockSpec((None, step_tokens, NA_WIDTH), lambda b, s: (b, s, 0)),
        out_shape=jax.ShapeDtypeStruct((B, SEQ, NA_WIDTH), BF16),
        compiler_params=_params("parallel", "arbitrary"),
    )(naq, nakv, *([bias_tab] * R))


def _na_bias_table(rpb):
    col = np.arange(GRID_W)
    col_start = np.clip(col - NA_COLS // 2, 0, GRID_W - NA_COLS)
    col_in = (col[None, :] >= col_start[:, None]) & (col[None, :] < col_start[:, None] + NA_COLS)
    col_idx = np.clip(col[None, :] - col[:, None], -(NA_COLS - 1), NA_COLS - 1) + (NA_COLS - 1)
    row_idx = np.arange(NA_ROWS)[None, :] - np.arange(NA_ROWS)[:, None] + (NA_ROWS - 1)
    row_sel = (row_idx[..., None] == np.arange(2 * NA_ROWS - 1)).astype(np.float32)
    col_sel = (col_idx[..., None] == np.arange(2 * NA_COLS - 1)).astype(np.float32)
    t = jnp.einsum('hrc,vjr,qkc->vhqjk', rpb.astype(F32), row_sel, col_sel, precision=lax.Precision.HIGHEST)
    t = jnp.where(col_in[None, None, :, None, :], t, -jnp.inf)
    return t.reshape(NA_ROWS, NA_HEADS, GRID_W, WIN_TOKENS)


def _outproj_kernel(of_ref, ob_ref, gate_ref, yna_ref, x_ref, gt1_ref, sh2_ref, sc2_ref, gn_ref, nf_ref,
                    wog_ref, won_ref, wr_ref, x1_ref, h2_ref, aff_ref):
    o = of_ref[...] + ob_ref[...]
    ys = []
    for h in range(GDN_HEADS):
        lanes = slice(h * LANES, (h + 1) * LANES)
        oh = o[:, lanes]
        yh = oh * lax.rsqrt(jnp.mean(oh * oh, axis=-1, keepdims=True) + EPS) * gn_ref[...]
        ys.append(yh * _silu(gate_ref[:, lanes]))
    yg = jnp.concatenate(ys, axis=1).astype(BF16)
    y = _dot(yg, wog_ref[...]) + _dot(yna_ref[...], won_ref[...])
    x1 = x_ref[...] + gt1_ref[...] * y
    x1_ref[...] = x1
    h2 = x1 * lax.rsqrt(jnp.mean(x1 * x1, axis=-1, keepdims=True) + EPS) * nf_ref[...]
    h2 = h2 * (1.0 + sc2_ref[...]) + sh2_ref[...]
    h2_ref[...] = h2
    hb = h2.astype(BF16)
    h_lo = (h2 - hb.astype(F32)).astype(BF16)
    wr_hi, wr_lo = _split2(wr_ref[...])
    logits = _dot_nt(wr_hi, hb) + (_dot_nt(wr_hi, h_lo) + _dot_nt(wr_lo, hb))
    e = jnp.exp(logits - jnp.max(logits, axis=0, keepdims=True))
    aff_ref[...] = e / jnp.sum(e, axis=0, keepdims=True)


def _out_projection(o_f, o_b, gate, y_na, x, gt1, sh2, sc2, gdn_norm, norm_ffn, wog, won, wrt):
    B = x.shape[0]
    D = D_MODEL
    lat = lambda n: pl.BlockSpec((None, TILE, n), lambda b, i: (b, i + 1, 0))
    tok = lambda n: pl.BlockSpec((None, TILE, n), lambda b, i: (b, i, 0))
    full = lambda a: pl.BlockSpec(a.shape, lambda b, i: (0,) * a.ndim)
    modrow = pl.BlockSpec((None, 1, D), lambda b, i: (b, 0, 0))
    return pl.pallas_call(
        _outproj_kernel,
        grid=(B, SEQ // TILE),
        in_specs=[lat(GDN_WIDTH), lat(GDN_WIDTH), lat(GDN_WIDTH), tok(NA_WIDTH), tok(D),
                  modrow, modrow, modrow, full(gdn_norm), full(norm_ffn), full(wog), full(won), full(wrt)],
        out_specs=[tok(D), tok(D),
                   pl.BlockSpec((None, N_EXPERTS, TILE), lambda b, i: (b, 0, i))],
        out_shape=[jax.ShapeDtypeStruct((B, SEQ, D), F32),
                   jax.ShapeDtypeStruct((B, SEQ, D), F32),
                   jax.ShapeDtypeStruct((B, N_EXPERTS, SEQ), F32)],
        compiler_params=_params("parallel", "arbitrary"),
    )(o_f, o_b, gate, y_na, x, gt1, sh2, sc2, gdn_norm, norm_ffn, wog, won, wrt)


TOK_BLOCKS = SEQ // LANES
ROUTE_ROWS = N_EXPERTS * TOK_BLOCKS
F32_MAGNITUDE_BITS = 31


def _topk_kernel(aff_ref, idx_ref, gval_ref):
    a = aff_ref[...]
    bf = lambda t: t.astype(BF16)
    mask = lambda c: jnp.where(c, 1.0, 0.0).astype(BF16)
    li = lax.broadcasted_iota(jnp.int32, (LANES, LANES), 0)
    lj = lax.broadcasted_iota(jnp.int32, (LANES, LANES), 1)
    ones = jnp.ones((LANES, LANES), BF16)
    upper = mask(li <= lj)
    ri = lax.broadcasted_iota(jnp.int32, (ROUTE_ROWS, ROUTE_ROWS), 0)
    rj = lax.broadcasted_iota(jnp.int32, (ROUTE_ROWS, ROUTE_ROWS), 1)
    shift = TOK_BLOCKS.bit_length() - 1
    same = (ri >> shift) == (rj >> shift)
    expert_sum = mask(same)
    rows_before = mask(jnp.logical_and(same, rj < ri))
    per_expert = lambda m: _dot(expert_sum, bf(_dot(m, ones)))

    def bisect(it, prefix):
        cand = prefix | (jnp.int32(1) << (F32_MAGNITUDE_BITS - 1 - it))
        cnt = per_expert(mask(a >= pltpu.bitcast(cand, F32)))
        return jnp.where(cnt >= CAPACITY, cand, prefix)

    thr = lax.fori_loop(0, F32_MAGNITUDE_BITS, bisect, jnp.zeros((ROUTE_ROWS, LANES), jnp.int32))
    gt = a >= pltpu.bitcast(thr + 1, F32)
    eq = jnp.logical_and(a >= pltpu.bitcast(thr, F32), jnp.logical_not(gt))
    need = CAPACITY - per_expert(mask(gt))

    def prefix(m):
        return _dot(m, upper), _dot(rows_before, bf(_dot(m, ones)))

    eq_m = mask(eq)
    eq_in_row, eq_rows_before = prefix(eq_m)
    eq_before = eq_in_row + eq_rows_before - eq_m.astype(F32)
    sel = jnp.logical_or(gt, jnp.logical_and(eq, eq_before < need))
    sel_m = mask(sel)
    rank_in_row, start = prefix(sel_m)
    local = jnp.where(sel, rank_in_row, 0.0)
    row_total = _dot(sel_m, ones)
    lane_f = lax.broadcasted_iota(jnp.int32, (CAPACITY, LANES), 1).astype(F32)
    slot = lax.broadcasted_iota(jnp.int32, (CAPACITY, LANES), 0).astype(F32)
    tok0 = (lax.broadcasted_iota(jnp.int32, (TOK_BLOCKS, LANES), 0) * LANES).astype(F32)
    tok0_hi, tok0_lo = _split2(tok0)
    ones_cl = jnp.ones((CAPACITY, LANES), BF16)
    pad = lambda t: jnp.concatenate([t, jnp.zeros((LANES - TOK_BLOCKS, LANES), t.dtype)], axis=0)

    def as_row(v):
        hi, lo = _split2(jnp.where(li == lj, pad(v), 0.0))
        return _dot(ones_cl, hi) + _dot(ones_cl, lo)

    for e in range(N_EXPERTS):
        rows = slice(e * TOK_BLOCKS, (e + 1) * TOK_BLOCKS)
        st, tot = start[rows], row_total[rows]
        st_r, tot_r = as_row(st), as_row(tot)
        owner = mask(jnp.logical_and(st_r <= slot, slot < st_r + tot_r))
        pick = lambda t: _dot(owner, pad(t))
        st_hi, st_lo = _split2(st)
        a_hi, a_mid, a_lo = _split3(a[rows])
        want = slot - (pick(st_hi) + pick(st_lo)) + 1.0
        hit = pick(bf(local[rows])) == want
        tok = jnp.sum(jnp.where(hit, lane_f, 0.0), axis=-1, keepdims=True)
        base = pick(tok0_hi) + pick(tok0_lo)
        val = pick(a_hi) + (pick(a_mid) + pick(a_lo))
        idx_ref[e] = (tok + base[:, 0:1]).astype(jnp.int32)
        gval_ref[e] = jnp.sum(jnp.where(hit, val, 0.0), axis=-1, keepdims=True)


def _route(aff_t):
    B = aff_t.shape[0]
    out = lambda dt: jax.ShapeDtypeStruct((B, N_EXPERTS, CAPACITY, 1), dt)
    spec = pl.BlockSpec((None, N_EXPERTS, CAPACITY, 1), lambda b: (b, 0, 0, 0))
    return pl.pallas_call(
        _topk_kernel,
        grid=(B,),
        in_specs=[pl.BlockSpec((None, ROUTE_ROWS, LANES), lambda b: (b, 0, 0))],
        out_specs=[spec, spec],
        out_shape=[out(jnp.int32), out(F32)],
        compiler_params=_params("parallel"),
    )(aff_t.reshape(B, ROUTE_ROWS, LANES))


GATHER_UNROLL = 8


def _dispatch_kernel(idx_ref, h_ref, xe_ref, rows_ref):
    base = (pl.program_id(0) * N_EXPERTS + pl.program_id(1)) * CAPACITY

    def body(r, _):
        t = idx_ref[base + r]
        rows_ref[pl.ds(r, 1), :] = h_ref[pl.ds(t, 1), :]
        return 0

    lax.fori_loop(0, CAPACITY, body, 0, unroll=GATHER_UNROLL)
    xe_ref[...] = rows_ref[...].astype(BF16)


def _dispatch(idx_flat, h2):
    B = h2.shape[0]
    D = D_MODEL
    return pl.pallas_call(
        _dispatch_kernel,
        grid_spec=pltpu.PrefetchScalarGridSpec(
            num_scalar_prefetch=1,
            grid=(B, N_EXPERTS),
            in_specs=[pl.BlockSpec((None, SEQ, D), lambda b, e, idx: (b, 0, 0))],
            out_specs=pl.BlockSpec((None, None, CAPACITY, D), lambda b, e, idx: (b, e, 0, 0)),
            scratch_shapes=[pltpu.VMEM((CAPACITY, D), F32)]),
        out_shape=jax.ShapeDtypeStruct((B, N_EXPERTS, CAPACITY, D), BF16),
        compiler_params=_params("parallel", "arbitrary"),
    )(idx_flat, h2)


def _ffn_kernel(xe_ref, gv_ref, wg_ref, wu_ref, wd_ref, ye_ref, wgb_ref, wub_ref, wdb_ref):
    @pl.when(pl.program_id(1) == 0)
    def _():
        wgb_ref[...] = wg_ref[...].astype(BF16)
        wub_ref[...] = wu_ref[...].astype(BF16)
        wdb_ref[...] = wd_ref[...].astype(BF16)

    xe = xe_ref[...]
    hid = (_silu(_dot(xe, wgb_ref[...])) * _dot(xe, wub_ref[...])).astype(BF16)
    ye_ref[...] = _dot(hid, wdb_ref[...]) * gv_ref[...]


def _expert_ffn(xe, gval, w_gate, w_up, w_down):
    B = xe.shape[0]
    D = D_MODEL
    F = w_gate.shape[-1]
    return pl.pallas_call(
        _ffn_kernel,
        grid=(N_EXPERTS, B),
        in_specs=[pl.BlockSpec((None, None, CAPACITY, D), lambda e, b: (b, e, 0, 0)),
                  pl.BlockSpec((None, None, CAPACITY, 1), lambda e, b: (b, e, 0, 0)),
                  pl.BlockSpec((None, D, F), lambda e, b: (e, 0, 0)),
                  pl.BlockSpec((None, D, F), lambda e, b: (e, 0, 0)),
                  pl.BlockSpec((None, F, D), lambda e, b: (e, 0, 0))],
        out_specs=pl.BlockSpec((None, None, CAPACITY, D), lambda e, b: (b, e, 0, 0)),
        out_shape=jax.ShapeDtypeStruct((B, N_EXPERTS, CAPACITY, D), F32),
        scratch_shapes=[pltpu.VMEM((D, F), BF16), pltpu.VMEM((D, F), BF16), pltpu.VMEM((F, D), BF16)],
        compiler_params=_params("arbitrary", "arbitrary"),
    )(xe, gval, w_gate, w_up, w_down)


HALF_E = N_EXPERTS // 2
OUT_TILES = SEQ // TILE


def _combine_kernel(idx_ref, ye0_ref, ye1_ref, x1_ref, gt2_ref, fw_ref, o_ref, acc0_ref, acc1_ref):
    b = pl.program_id(0)
    s = pl.program_id(1)

    @pl.when(s == 0)
    def _():
        acc0_ref[...] = jnp.zeros_like(acc0_ref)
        acc1_ref[...] = jnp.zeros_like(acc1_ref)

    @pl.when(s < HALF_E)
    def _():
        base0 = (b * N_EXPERTS + s) * CAPACITY
        base1 = base0 + HALF_E * CAPACITY

        def body(r, _):
            t0 = idx_ref[base0 + r]
            t1 = idx_ref[base1 + r]
            acc0_ref[pl.ds(t0, 1), :] = acc0_ref[pl.ds(t0, 1), :] + ye0_ref[pl.ds(r, 1), :]
            acc1_ref[pl.ds(t1, 1), :] = acc1_ref[pl.ds(t1, 1), :] + ye1_ref[pl.ds(r, 1), :]
            return 0

        lax.fori_loop(0, CAPACITY, body, 0, unroll=GATHER_UNROLL)

    @pl.when(s >= HALF_E)
    def _():
        t = pl.multiple_of((s - HALF_E) * TILE, TILE)
        moe = acc0_ref[pl.ds(t, TILE), :] + acc1_ref[pl.ds(t, TILE), :]
        x2 = x1_ref[...] + gt2_ref[...] * moe
        o_ref[...] = x2 * lax.rsqrt(jnp.mean(x2 * x2, axis=-1, keepdims=True) + EPS) * fw_ref[...]


def _combine(idx_flat, ye, x1, gt2, final_norm):
    B = ye.shape[0]
    D = D_MODEL
    expert = lambda off: pl.BlockSpec((None, None, CAPACITY, D),
                                      lambda b, s, idx: (b, jnp.minimum(s, HALF_E - 1) + off, 0, 0))
    tile = pl.BlockSpec((None, TILE, D), lambda b, s, idx: (b, jnp.maximum(s - HALF_E, 0), 0))
    return pl.pallas_call(
        _combine_kernel,
        grid_spec=pltpu.PrefetchScalarGridSpec(
            num_scalar_prefetch=1,
            grid=(B, HALF_E + OUT_TILES),
            in_specs=[expert(0), expert(HALF_E), tile,
                      pl.BlockSpec((None, 1, D), lambda b, s, idx: (b, 0, 0)),
                      pl.BlockSpec((1, D), lambda b, s, idx: (0, 0))],
            out_specs=tile,
            scratch_shapes=[pltpu.VMEM((SEQ, D), F32), pltpu.VMEM((SEQ, D), F32)]),
        out_shape=jax.ShapeDtypeStruct((B, SEQ, D), F32),
        compiler_params=_params("parallel", "arbitrary"),
    )(idx_flat, ye, ye, x1, gt2, final_norm)


def kernel(x, c, ctx, c_ctx, w_mod, b_mod, norm_mix, norm_ffn, w_in, conv_qkv, a_log, dt_bias, gdn_norm, na_rpb,
           w_out, w_router, w_gate, w_up, w_down, final_norm):
    B, T, D = x.shape
    assert (T, D) == (SEQ, D_MODEL) and ctx.shape == (B, CTX_LEN, D) and w_mod.shape[0] == 1
    li = 0
    mod_rows = -(-(B + 1) // SUBLANES) * SUBLANES
    cc = jnp.zeros((mod_rows, D), F32).at[:B].set(c).at[B].set(c_ctx)
    mod = _modulation(cc, w_mod[li], b_mod[li])
    part = lambda j, rows: mod[:rows, j * D:(j + 1) * D].reshape(rows, 1, D)
    sh1, sc1 = part(0, B + 1), part(1, B + 1)
    gt1, sh2, sc2, gt2 = part(2, B), part(3, B), part(4, B), part(5, B)

    wi = w_in[li]
    q_end, g_end = 3 * GDN_WIDTH, 4 * GDN_WIDTH
    ab_end = g_end + 4 * GDN_HEADS
    wq = wi[:, :q_end].astype(BF16)
    wg = wi[:, q_end:g_end].astype(BF16)
    wab = jnp.zeros((D, LANES), F32).at[:, :4 * GDN_HEADS].set(wi[:, g_end:ab_end]).astype(BF16)
    wnq = wi[:, ab_end:ab_end + NA_WIDTH].astype(BF16)
    wnkv = wi[:, ab_end + NA_WIDTH:].astype(BF16)
    gpar = jnp.zeros((2, LANES), F32)
    gpar = gpar.at[0, :2 * GDN_HEADS].set(-jnp.exp(a_log[li].astype(F32)).reshape(-1))
    gpar = gpar.at[1, :2 * GDN_HEADS].set(dt_bias[li].astype(F32).reshape(-1))

    cos_tab, sin_tab = _rope_tables()
    qkv, gate, gb, naq, nakv = _in_projection(x, ctx, sh1, sc1, norm_mix[li].reshape(1, D),
                                              wq, wg, wab, wnq, wnkv, gpar, conv_qkv[li], cos_tab, sin_tab)
    o_f, o_b = _gdn_scan(qkv, gb)
    y_na = _neighbourhood_attention(naq, nakv, _na_bias_table(na_rpb[li]))

    wo = w_out[li].astype(BF16)
    x1, h2, aff_t = _out_projection(o_f, o_b, gate, y_na, x, gt1, sh2, sc2,
                                     gdn_norm[li].reshape(1, GDN_HEAD_DIM), norm_ffn[li].reshape(1, D),
                                     wo[:GDN_WIDTH], wo[GDN_WIDTH:], w_router[li].T)
    idx, gval = _route(aff_t)
    idx_flat = idx.reshape(-1)
    xe = _dispatch(idx_flat, h2)
    ye = _expert_ffn(xe, gval, w_gate[li], w_up[li], w_down[li])
    return _combine(idx_flat, ye, x1, gt2, final_norm.reshape(1, D))
```

```python
import math

import numpy as np
import jax
import jax.numpy as jnp
from jax import lax
from jax.experimental import pallas as pl
from jax.experimental.pallas import tpu as pltpu

F32 = jnp.float32
BF16 = jnp.bfloat16

D_MODEL = 1024
SEQ = 4096
CTX_LEN = 256
GRID_W = 64
GRID_ROWS = SEQ // GRID_W
GDN_HEADS = 4
GDN_HEAD_DIM = 128
GDN_WIDTH = GDN_HEADS * GDN_HEAD_DIM
CONV_W = 5
CHUNK = 64
ROPE_BASE = 10000.0
NA_HEADS = 8
NA_HEAD_DIM = 64
NA_WIDTH = NA_HEADS * NA_HEAD_DIM
NA_ROWS = 8
NA_COLS = 16
N_EXPERTS = 16
CAPACITY = 2 * SEQ // N_EXPERTS
EPS = 1e-6

TILE = 256
T_ALL = CTX_LEN + SEQ
N_TILES = T_ALL // TILE
N_CHUNKS = T_ALL // CHUNK
CTX_CHUNKS = CTX_LEN // CHUNK
LANES = 128
SUBLANES = 8
VMEM_LIMIT = 56 * 1024 * 1024


def _params(*sem):
    return pltpu.CompilerParams(dimension_semantics=sem, vmem_limit_bytes=VMEM_LIMIT)


def _dot(a, b):
    return jnp.dot(a, b, preferred_element_type=F32)


def _dot_nt(a, b):
    return lax.dot_general(a, b, (((1,), (1,)), ((), ())), preferred_element_type=F32)


def _dot_tn(a, b):
    return lax.dot_general(a, b, (((0,), (0,)), ((), ())), preferred_element_type=F32)


def _split2(a):
    hi = a.astype(BF16)
    lo = (a - hi.astype(F32)).astype(BF16)
    return hi, lo


def _split3(a):
    hi = a.astype(BF16)
    r = a - hi.astype(F32)
    mid = r.astype(BF16)
    lo = (r - mid.astype(F32)).astype(BF16)
    return hi, mid, lo


def _dot3(a, b):
    ah, al = _split2(a)
    bh, bl = _split2(b)
    return _dot(ah, bh) + (_dot(ah, bl) + _dot(al, bh))


def _silu(x):
    return x * jax.nn.sigmoid(x)


def _mod_kernel(c_ref, w_ref, b_ref, o_ref):
    s = _silu(c_ref[...])
    o_ref[...] = _dot3(s, w_ref[...]) + b_ref[...]


def _modulation(cc, w_mod, b_mod):
    rows, d = cc.shape
    n = w_mod.shape[1]
    bn = 1024
    return pl.pallas_call(
        _mod_kernel,
        grid=(n // bn,),
        in_specs=[pl.BlockSpec((rows, d), lambda j: (0, 0)),
                  pl.BlockSpec((d, bn), lambda j: (0, j)),
                  pl.BlockSpec((1, bn), lambda j: (0, j))],
        out_specs=pl.BlockSpec((rows, bn), lambda j: (0, j)),
        out_shape=jax.ShapeDtypeStruct((rows, n), F32),
        compiler_params=_params("parallel"),
    )(cc, w_mod, b_mod.reshape(1, n))


HALO = SUBLANES


def _inproj_kernel(x_ref, ctx_ref, xp_ref, xn_ref, sh_ref, sc_ref, nw_ref, wq_ref, wg_ref, wab_ref, wnq_ref,
                   wnkv_ref, gpar_ref, cw_ref, cos_ref, sin_ref, qkv_ref, gate_ref, gb_ref, naq_ref, nakv_ref, ext_ref):
    i = pl.program_id(1)
    has_prev = i >= 2
    has_next = jnp.logical_and(i >= 1, i < N_TILES - 1)
    xt = jnp.concatenate([xp_ref[...], jnp.where(i == 0, ctx_ref[...], x_ref[...]), xn_ref[...]], axis=0)
    ms = jnp.mean(xt * xt, axis=-1, keepdims=True)
    h = xt * lax.rsqrt(ms + EPS) * nw_ref[...]
    h = h * (1.0 + sc_ref[...]) + sh_ref[...]
    pq = _dot(h.astype(BF16), wq_ref[...])
    ext_ref[0:HALO, :] = jnp.where(has_prev, pq[0:HALO], 0.0)
    ext_ref[HALO:HALO + TILE, :] = pq[HALO:HALO + TILE]
    ext_ref[HALO + TILE:, :] = jnp.where(has_next, pq[HALO + TILE:], 0.0)
    hb = h[HALO:HALO + TILE].astype(BF16)
    gate_ref[...] = _dot(hb, wg_ref[...])
    first = HALO - CONV_W // 2
    acc = ext_ref[first:first + TILE, :] * cw_ref[0:1, :]
    for k in range(1, CONV_W):
        acc = acc + ext_ref[first + k:first + k + TILE, :] * cw_ref[k:k + 1, :]
    naq_ref[...] = (_dot(hb, wnq_ref[...]) * (NA_HEAD_DIM ** -0.5)).astype(BF16)
    y = _silu(acc)
    nakv_ref[...] = _dot(hb, wnkv_ref[...]).astype(BF16)
    cos = cos_ref[...]
    sin = sin_ref[...]
    lane = lax.broadcasted_iota(jnp.int32, (TILE, LANES), 1)
    take_upper = (lane % (GDN_HEAD_DIM // 2)) < (GDN_HEAD_DIM // 4)
    for j in range(2 * GDN_HEADS):
        t = y[:, j * LANES:(j + 1) * LANES]
        t = t * lax.rsqrt(jnp.sum(t * t, axis=-1, keepdims=True) + EPS)
        partner = jnp.where(take_upper, pltpu.roll(t, LANES - GDN_HEAD_DIM // 4, 1),
                            pltpu.roll(t, GDN_HEAD_DIM // 4, 1))
        t = t * cos + partner * sin
        if j < GDN_HEADS:
            t = t * (GDN_HEAD_DIM ** -0.5)
        qkv_ref[:, j * LANES:(j + 1) * LANES] = t
    qkv_ref[:, 2 * GDN_WIDTH:] = y[:, 2 * GDN_WIDTH:]
    ab = _dot(hb, wab_ref[...])
    z = ab + gpar_ref[1:2, :]
    softplus = jnp.maximum(z, 0.0) + jnp.log1p(jnp.exp(-jnp.abs(z)))
    lane = lax.broadcasted_iota(jnp.int32, ab.shape, 1)
    gb_ref[...] = jnp.where(lane < 2 * GDN_HEADS, gpar_ref[0:1, :] * softplus, jax.nn.sigmoid(ab))


def _in_projection(x, ctx, sh, sc, norm_w, wq, wg, wab, wnq, wnkv, gpar, conv_w, cos_tab, sin_tab):
    B = x.shape[0]
    D = D_MODEL
    per = TILE // HALO
    tok = lambda n: pl.BlockSpec((None, TILE, n), lambda b, i: (b, i, 0))
    full = lambda a: pl.BlockSpec(a.shape, lambda b, i: (0,) * a.ndim)
    modrow = pl.BlockSpec((None, 1, D), lambda b, i: (jnp.where(i == 0, B, b), 0, 0))
    rope = pl.BlockSpec((TILE, LANES), lambda b, i: (i, 0))
    before = pl.BlockSpec((None, HALO, D), lambda b, i: (b, jnp.maximum((i - 1) * per - 1, 0), 0))
    after = pl.BlockSpec((None, HALO, D), lambda b, i: (b, jnp.minimum(jnp.maximum(i, 1) * per, SEQ // HALO - 1), 0))
    outs = [(3 * GDN_WIDTH, F32), (GDN_WIDTH, F32), (LANES, F32), (NA_WIDTH, BF16), (2 * NA_WIDTH, BF16)]
    return pl.pallas_call(
        _inproj_kernel,
        grid=(B, N_TILES),
        in_specs=[pl.BlockSpec((None, TILE, D), lambda b, i: (b, jnp.maximum(i - 1, 0), 0)),
                  pl.BlockSpec((None, TILE, D), lambda b, i: (b, 0, 0)), before, after,
                  modrow, modrow, full(norm_w), full(wq), full(wg), full(wab), full(wnq), full(wnkv), full(gpar),
                  full(conv_w), rope, rope],
        out_specs=[tok(n) for n, _ in outs],
        out_shape=[jax.ShapeDtypeStruct((B, T_ALL, n), dt) for n, dt in outs],
        scratch_shapes=[pltpu.VMEM((TILE + 2 * HALO, 3 * GDN_WIDTH), F32)],
        compiler_params=_params("parallel", "arbitrary"),
    )(x, ctx, x, x, sh, sc, norm_w, wq, wg, wab, wnq, wnkv, gpar, conv_w, cos_tab, sin_tab)


def _rope_tables():
    half = GDN_HEAD_DIM // 2
    pairs = half // 2
    t = np.arange(SEQ)
    inv_freq = jnp.asarray(ROPE_BASE, F32) ** (-jnp.arange(pairs, dtype=F32) / pairs)

    def tab(pos):
        ang = jnp.asarray(pos, F32)[:, None] * inv_freq[None, :]
        c, s = jnp.cos(ang), jnp.sin(ang)
        return jnp.concatenate([c, c], axis=-1), jnp.concatenate([-s, s], axis=-1)

    cr, sr = tab(t // GRID_W)
    cc, sc = tab(t % GRID_W)
    cos = jnp.concatenate([cr, cc], axis=-1)
    sin = jnp.concatenate([sr, sc], axis=-1)
    cos = jnp.concatenate([jnp.ones((CTX_LEN, LANES), F32), cos], axis=0)
    sin = jnp.concatenate([jnp.zeros((CTX_LEN, LANES), F32), sin], axis=0)
    return cos, sin


SCAN_SAMPLES = 8


def _gdn_scan_kernel(qf_ref, qb_ref, gf_ref, gb_ref, of_ref, ob_ref, s_ref):
    n = pl.program_id(1)

    @pl.when(n == 0)
    def _():
        s_ref[...] = jnp.zeros_like(s_ref)

    row = lax.broadcasted_iota(jnp.int32, (CHUNK, CHUNK), 0)
    col = lax.broadcasted_iota(jnp.int32, (CHUNK, CHUNK), 1)
    eye = jnp.where(row == col, 1.0, 0.0).astype(F32)
    chains = range(SCAN_SAMPLES * 2 * GDN_HEADS)
    bf = lambda t: t.astype(BF16)
    each = lambda f, *cols: [f(*args) for args in zip(*cols)]
    q, k, v, gcol, grow, gtot, bcol, incl, strict = [], [], [], [], [], [], [], [], []
    for s in range(SCAN_SAMPLES):
        for d in range(2):
            src = qf_ref if d == 0 else qb_ref
            gbt = (gf_ref if d == 0 else gb_ref)[s]
            inc = (row >= col) if d == 0 else (row <= col)
            cm = jnp.where(inc, 1.0, 0.0).astype(BF16)
            g1, g2, g3 = _split3(gbt)
            gc = _dot(cm, g1) + (_dot(cm, g2) + _dot(cm, g3))
            gct = gc.T
            tot = gc[CHUNK - 1:CHUNK, :] if d == 0 else gc[0:1, :]
            for h in range(GDN_HEADS):
                c = d * GDN_HEADS + h
                gcol.append(gc[:, c:c + 1])
                grow.append(gct[c:c + 1, :])
                gtot.append(tot[:, c:c + 1])
                bcol.append(gbt[:, 2 * GDN_HEADS + c:2 * GDN_HEADS + c + 1])
                q.append(src[s, :, h * LANES:(h + 1) * LANES])
                k.append(src[s, :, GDN_WIDTH + h * LANES:GDN_WIDTH + (h + 1) * LANES])
                v.append(src[s, :, 2 * GDN_WIDTH + h * LANES:2 * GDN_WIDTH + (h + 1) * LANES])
                incl.append(inc)
                strict.append((row > col) if d == 0 else (row < col))
    s_old = [s_ref[c] for c in chains]
    decay = each(lambda m, gc_, gr_: jnp.exp(jnp.where(m, gc_ - gr_, -jnp.inf)), incl, gcol, grow)
    eg = each(jnp.exp, gcol)
    kb = each(lambda k_, b_: k_ * b_, k, bcol)
    a = each(lambda kb_, q_, k_: _dot_nt(bf(jnp.concatenate([kb_, q_], axis=0)), bf(k_)), kb, q, k)
    lower = each(lambda m, a_, dc: jnp.where(m, a_[:CHUNK] * dc, 0.0), strict, a, decay)
    attn = each(lambda a_, dc: bf(a_[CHUNK:] * dc), a, decay)
    tinv = each(lambda l_: eye - l_, lower)
    m = each(lambda l_: _dot(bf(l_), bf(l_)), lower)
    for _ in range(int(math.log2(CHUNK)) - 2):
        prod = each(lambda m_, t_: _dot(bf(jnp.concatenate([m_, t_], axis=0)), bf(m_)), m, tinv)
        tinv = each(lambda t_, p_: t_ + p_[CHUNK:], tinv, prod)
        m = each(lambda p_: p_[:CHUNK], prod)
    tinv = each(lambda t_, m_: t_ + _dot(bf(t_), bf(m_)), tinv, m)
    resid = each(lambda t_, l_: (eye - t_) - _dot3(l_, t_), tinv, lower)
    tinv = each(lambda t_, r_: t_ + _dot(bf(t_), bf(r_)), tinv, resid)
    uw = each(lambda t_, v_, b_, kb_, eg_: _dot(bf(t_), bf(jnp.concatenate([v_ * b_, kb_ * eg_], axis=1))),
              tinv, v, bcol, kb, eg)
    ws = each(lambda uw_, q_, eg_, s_: _dot(bf(jnp.concatenate([uw_[:, LANES:], q_ * eg_], axis=0)), bf(s_)),
              uw, q, eg, s_old)
    vb = each(lambda uw_, ws_: bf(uw_[:, :LANES] - ws_[:CHUNK]), uw, ws)
    o = each(lambda ws_, at_, vb_: ws_[CHUNK:] + _dot(at_, vb_), ws, attn, vb)
    s_new = each(lambda s_, gt_, k_, gc_, vb_: s_ * jnp.exp(gt_) + _dot_tn(bf(k_ * jnp.exp(gt_ - gc_)), vb_),
                 s_old, gtot, k, gcol, vb)
    for c in chains:
        s, d, h = c // (2 * GDN_HEADS), (c // GDN_HEADS) % 2, c % GDN_HEADS
        o_ref = of_ref if d == 0 else ob_ref
        o_ref[s, :, h * LANES:(h + 1) * LANES] = o[c]
        s_ref[c] = s_new[c]


def _gdn_scan(qkv, gb):
    B = qkv.shape[0]
    S = SCAN_SAMPLES
    assert B % S == 0
    fwd = lambda b, n: (b, n, 0)
    bwd = lambda b, n: (b, jnp.where(n < CTX_CHUNKS, CTX_CHUNKS - 1 - n, N_CHUNKS + CTX_CHUNKS - 1 - n), 0)
    out = jax.ShapeDtypeStruct((B, T_ALL, GDN_WIDTH), F32)
    return pl.pallas_call(
        _gdn_scan_kernel,
        grid=(B // S, N_CHUNKS),
        in_specs=[pl.BlockSpec((S, CHUNK, 3 * GDN_WIDTH), fwd),
                  pl.BlockSpec((S, CHUNK, 3 * GDN_WIDTH), bwd),
                  pl.BlockSpec((S, CHUNK, LANES), fwd),
                  pl.BlockSpec((S, CHUNK, LANES), bwd)],
        out_specs=[pl.BlockSpec((S, CHUNK, GDN_WIDTH), fwd),
                   pl.BlockSpec((S, CHUNK, GDN_WIDTH), bwd)],
        out_shape=[out, out],
        scratch_shapes=[pltpu.VMEM((S * 2 * GDN_HEADS, GDN_HEAD_DIM, GDN_HEAD_DIM), F32)],
        compiler_params=_params("parallel", "arbitrary"),
    )(qkv, qkv, gb, gb)


WIN_TOKENS = NA_ROWS * GRID_W
NA_STEP_ROWS = 4


def _window_start(r):
    return jnp.clip(r - NA_ROWS // 2, 0, GRID_ROWS - NA_ROWS)


def _na_kernel(q_ref, kv_ref, *rest):
    bias_refs, o_ref = rest[:NA_STEP_ROWS], rest[NA_STEP_ROWS]
    lane = lax.broadcasted_iota(jnp.int32, (GRID_W, LANES), 1)
    low = lane < NA_HEAD_DIM
    zero = jnp.zeros((GRID_W, LANES), BF16)
    each = lambda f, *cols: [f(*args) for args in zip(*cols)]
    rowmax = lambda t: jnp.max(t, axis=-1, keepdims=True)
    rowsum = lambda t: jnp.sum(t, axis=-1, keepdims=True)
    items =[(j, p) for j in range(NA_STEP_ROWS) for p in range(NA_HEADS // 2)]
    qrows = [slice(j * GRID_W, (j + 1) * GRID_W) for j, _ in items]
    klanes = [slice(p * LANES, (p + 1) * LANES) for _, p in items]
    vlanes = [slice(NA_WIDTH + s.start, NA_WIDTH + s.stop) for s in klanes]
    offs = [pl.multiple_of(CTX_LEN + _window_start(pl.program_id(1) * NA_STEP_ROWS + j) * GRID_W, GRID_W)
            for j in range(NA_STEP_ROWS)]
    off = [offs[j] for j, _ in items]
    q2 = each(lambda rw, s: jnp.concatenate([jnp.where(low, q_ref[rw, s], zero),
                                             jnp.where(low, zero, q_ref[rw, s])], axis=0), qrows, klanes)
    bias = [jnp.concatenate([bias_refs[j][2 * p], bias_refs[j][2 * p + 1]], axis=0) for j, p in items]
    s_lat = each(lambda q_, o_, s, b_: _dot_nt(q_, kv_ref[pl.ds(o_, WIN_TOKENS), s]) + b_, q2, off, klanes, bias)
    s_ctx = each(lambda q_, s: _dot_nt(q_, kv_ref[0:CTX_LEN, s]), q2, klanes)
    mx = each(lambda a, b: jnp.maximum(rowmax(a), rowmax(b)), s_lat, s_ctx)
    p_lat = each(lambda a, m: jnp.exp(a - m), s_lat, mx)
    p_ctx = each(lambda a, m: jnp.exp(a - m), s_ctx, mx)
    den = each(lambda a, b: rowsum(a) + rowsum(b), p_lat, p_ctx)
    o = each(lambda a, b, o_, s: _dot(a.astype(BF16), kv_ref[pl.ds(o_, WIN_TOKENS), s])
             + _dot(b.astype(BF16), kv_ref[0:CTX_LEN, s]), p_lat, p_ctx, off, vlanes)
    o = each(lambda o_, d_: o_ / d_, o, den)
    for rw, s, o_ in zip(qrows, klanes, o):
        o_ref[rw, s] = jnp.where(low, o_[:GRID_W], o_[GRID_W:]).astype(BF16)


def _neighbourhood_attention(naq, nakv, bias_tab):
    B = naq.shape[0]
    R = NA_STEP_ROWS
    step_tokens = R * GRID_W
    assert GRID_ROWS % R == 0 and CTX_LEN % step_tokens == 0
    q_blocks_before = CTX_LEN // step_tokens

    def bias_spec(j):
        return pl.BlockSpec((None, NA_HEADS, GRID_W, WIN_TOKENS),
                            lambda b, s: (s * R + j - _window_start(s * R + j), 0, 0, 0))

    return pl.pallas_call(
        _na_kernel,
        grid=(B, GRID_ROWS // R),
        in_specs=[pl.BlockSpec((None, step_tokens, NA_WIDTH), lambda b, s: (b, s + q_blocks_before, 0)),
                  pl.BlockSpec((None, T_ALL, 2 * NA_WIDTH), lambda b, s: (b, 0, 0))]
                 + [bias_spec(j) for j in range(R)],
        out_specs=pl.BlockSpec((None, step_tokens, NA_WIDTH), lambda b, s: (b, s, 0)),
        out_shape=jax.ShapeDtypeStruct((B, SEQ, NA_WIDTH), BF16),
        compiler_params=_params("parallel", "arbitrary"),
    )(naq, nakv, *([bias_tab] * R))


def _na_bias_table(rpb):
    col = np.arange(GRID_W)
    col_start = np.clip(col - NA_COLS // 2, 0, GRID_W - NA_COLS)
    col_in = (col[None, :] >= col_start[:, None]) & (col[None, :] < col_start[:, None] + NA_COLS)
    col_idx = np.clip(col[None, :] - col[:, None], -(NA_COLS - 1), NA_COLS - 1) + (NA_COLS - 1)
    row_idx = np.arange(NA_ROWS)[None, :] - np.arange(NA_ROWS)[:, None] + (NA_ROWS - 1)
    row_sel = (row_idx[..., None] == np.arange(2 * NA_ROWS - 1)).astype(np.float32)
    col_sel = (col_idx[..., None] == np.arange(2 * NA_COLS - 1)).astype(np.float32)
    t = jnp.einsum('hrc,vjr,qkc->vhqjk', rpb.astype(F32), row_sel, col_sel, precision=lax.Precision.HIGHEST)
    t = jnp.where(col_in[None, None, :, None, :], t, -jnp.inf)
    return t.reshape(NA_ROWS, NA_HEADS, GRID_W, WIN_TOKENS)


def _outproj_kernel(of_ref, ob_ref, gate_ref, yna_ref, x_ref, gt1_ref, sh2_ref, sc2_ref, gn_ref, nf_ref,
                    wog_ref, won_ref, wr_ref, x1_ref, h2_ref, aff_ref):
    o = of_ref[...] + ob_ref[...]
    ys = []
    for h in range(GDN_HEADS):
        lanes = slice(h * LANES, (h + 1) * LANES)
        oh = o[:, lanes]
        yh = oh * lax.rsqrt(jnp.mean(oh * oh, axis=-1, keepdims=True) + EPS) * gn_ref[...]
        ys.append(yh * _silu(gate_ref[:, lanes]))
    yg = jnp.concatenate(ys, axis=1).astype(BF16)
    y = _dot(yg, wog_ref[...]) + _dot(yna_ref[...], won_ref[...])
    x1 = x_ref[...] + gt1_ref[...] * y
    x1_ref[...] = x1
    h2 = x1 * lax.rsqrt(jnp.mean(x1 * x1, axis=-1, keepdims=True) + EPS) * nf_ref[...]
    h2 = h2 * (1.0 + sc2_ref[...]) + sh2_ref[...]
    h2_ref[...] = h2
    hb = h2.astype(BF16)
    h_lo = (h2 - hb.astype(F32)).astype(BF16)
    wr_hi, wr_lo = _split2(wr_ref[...])
    logits = _dot_nt(wr_hi, hb) + (_dot_nt(wr_hi, h_lo) + _dot_nt(wr_lo, hb))
    e = jnp.exp(logits - jnp.max(logits, axis=0, keepdims=True))
    aff_ref[...] = e / jnp.sum(e, axis=0, keepdims=True)


def _out_projection(o_f, o_b, gate, y_na, x, gt1, sh2, sc2, gdn_norm, norm_ffn, wog, won, wrt):
    B = x.shape[0]
    D = D_MODEL
    lat = lambda n: pl.BlockSpec((None, TILE, n), lambda b, i: (b, i + 1, 0))
    tok = lambda n: pl.BlockSpec((None, TILE, n), lambda b, i: (b, i, 0))
    full = lambda a: pl.BlockSpec(a.shape, lambda b, i: (0,) * a.ndim)
    modrow = pl.BlockSpec((None, 1, D), lambda b, i: (b, 0, 0))
    return pl.pallas_call(
        _outproj_kernel,
        grid=(B, SEQ // TILE),
        in_specs=[lat(GDN_WIDTH), lat(GDN_WIDTH), lat(GDN_WIDTH), tok(NA_WIDTH), tok(D),
                  modrow, modrow, modrow, full(gdn_norm), full(norm_ffn), full(wog), full(won), full(wrt)],
        out_specs=[tok(D), tok(D),
                   pl.BlockSpec((None, N_EXPERTS, TILE), lambda b, i: (b, 0, i))],
        out_shape=[jax.ShapeDtypeStruct((B, SEQ, D), F32),
                   jax.ShapeDtypeStruct((B, SEQ, D), F32),
                   jax.ShapeDtypeStruct((B, N_EXPERTS, SEQ), F32)],
        compiler_params=_params("parallel", "arbitrary"),
    )(o_f, o_b, gate, y_na, x, gt1, sh2, sc2, gdn_norm, norm_ffn, wog, won, wrt)


TOK_BLOCKS = SEQ // LANES
ROUTE_ROWS = N_EXPERTS * TOK_BLOCKS
F32_MAGNITUDE_BITS = 31


def _topk_kernel(aff_ref, idx_ref, gval_ref):
    a = aff_ref[...]
    bf = lambda t: t.astype(BF16)
    mask = lambda c: jnp.where(c, 1.0, 0.0).astype(BF16)
    li = lax.broadcasted_iota(jnp.int32, (LANES, LANES), 0)
    lj = lax.broadcasted_iota(jnp.int32, (LANES, LANES), 1)
    ones = jnp.ones((LANES, LANES), BF16)
    upper = mask(li <= lj)
    ri = lax.broadcasted_iota(jnp.int32, (ROUTE_ROWS, ROUTE_ROWS), 0)
    rj = lax.broadcasted_iota(jnp.int32, (ROUTE_ROWS, ROUTE_ROWS), 1)
    shift = TOK_BLOCKS.bit_length() - 1
    same = (ri >> shift) == (rj >> shift)
    expert_sum = mask(same)
    rows_before = mask(jnp.logical_and(same, rj < ri))
    per_expert = lambda m: _dot(expert_sum, bf(_dot(m, ones)))

    def bisect(it, prefix):
        cand = prefix | (jnp.int32(1) << (F32_MAGNITUDE_BITS - 1 - it))
        cnt = per_expert(mask(a >= pltpu.bitcast(cand, F32)))
        return jnp.where(cnt >= CAPACITY, cand, prefix)

    thr = lax.fori_loop(0, F32_MAGNITUDE_BITS, bisect, jnp.zeros((ROUTE_ROWS, LANES), jnp.int32))
    gt = a >= pltpu.bitcast(thr + 1, F32)
    eq = jnp.logical_and(a >= pltpu.bitcast(thr, F32), jnp.logical_not(gt))
    need = CAPACITY - per_expert(mask(gt))

    def prefix(m):
        return _dot(m, upper), _dot(rows_before, bf(_dot(m, ones)))

    eq_m = mask(eq)
    eq_in_row, eq_rows_before = prefix(eq_m)
    eq_before = eq_in_row + eq_rows_before - eq_m.astype(F32)
    sel = jnp.logical_or(gt, jnp.logical_and(eq, eq_before < need))
    sel_m = mask(sel)
    rank_in_row, start = prefix(sel_m)
    local = jnp.where(sel, rank_in_row, 0.0)
    row_total = _dot(sel_m, ones)
    lane_f = lax.broadcasted_iota(jnp.int32, (CAPACITY, LANES), 1).astype(F32)
    slot = lax.broadcasted_iota(jnp.int32, (CAPACITY, LANES), 0).astype(F32)
    tok0 = (lax.broadcasted_iota(jnp.int32, (TOK_BLOCKS, LANES), 0) * LANES).astype(F32)
    tok0_hi, tok0_lo = _split2(tok0)
    ones_cl = jnp.ones((CAPACITY, LANES), BF16)
    pad = lambda t: jnp.concatenate([t, jnp.zeros((LANES - TOK_BLOCKS, LANES), t.dtype)], axis=0)

    def as_row(v):
        hi, lo = _split2(jnp.where(li == lj, pad(v), 0.0))
        return _dot(ones_cl, hi) + _dot(ones_cl, lo)

    for e in range(N_EXPERTS):
        rows = slice(e * TOK_BLOCKS, (e + 1) * TOK_BLOCKS)
        st, tot = start[rows], row_total[rows]
        st_r, tot_r = as_row(st), as_row(tot)
        owner = mask(jnp.logical_and(st_r <= slot, slot < st_r + tot_r))
        pick = lambda t: _dot(owner, pad(t))
        st_hi, st_lo = _split2(st)
        a_hi, a_mid, a_lo = _split3(a[rows])
        want = slot - (pick(st_hi) + pick(st_lo)) + 1.0
        hit = pick(bf(local[rows])) == want
        tok = jnp.sum(jnp.where(hit, lane_f, 0.0), axis=-1, keepdims=True)
        base = pick(tok0_hi) + pick(tok0_lo)
        val = pick(a_hi) + (pick(a_mid) + pick(a_lo))
        idx_ref[e] = (tok + base[:, 0:1]).astype(jnp.int32)
        gval_ref[e] = jnp.sum(jnp.where(hit, val, 0.0), axis=-1, keepdims=True)


def _route(aff_t):
    B = aff_t.shape[0]
    out = lambda dt: jax.ShapeDtypeStruct((B, N_EXPERTS, CAPACITY, 1), dt)
    spec = pl.BlockSpec((None, N_EXPERTS, CAPACITY, 1), lambda b: (b, 0, 0, 0))
    return pl.pallas_call(
        _topk_kernel,
        grid=(B,),
        in_specs=[pl.BlockSpec((None, ROUTE_ROWS, LANES), lambda b: (b, 0, 0))],
        out_specs=[spec, spec],
        out_shape=[out(jnp.int32), out(F32)],
        compiler_params=_params("parallel"),
    )(aff_t.reshape(B, ROUTE_ROWS, LANES))


GATHER_UNROLL = 8


def _dispatch_kernel(idx_ref, h_ref, xe_ref, rows_ref):
    base = (pl.program_id(0) * N_EXPERTS + pl.program_id(1)) * CAPACITY

    def body(r, _):
        t = idx_ref[base + r]
        rows_ref[pl.ds(r, 1), :] = h_ref[pl.ds(t, 1), :]
        return 0

    lax.fori_loop(0, CAPACITY, body, 0, unroll=GATHER_UNROLL)
    xe_ref[...] = rows_ref[...].astype(BF16)


def _dispatch(idx_flat, h2):
    B = h2.shape[0]
    D = D_MODEL
    return pl.pallas_call(
        _dispatch_kernel,
        grid_spec=pltpu.PrefetchScalarGridSpec(
            num_scalar_prefetch=1,
            grid=(B, N_EXPERTS),
            in_specs=[pl.BlockSpec((None, SEQ, D), lambda b, e, idx: (b, 0, 0))],
            out_specs=pl.BlockSpec((None, None, CAPACITY, D), lambda b, e, idx: (b, e, 0, 0)),
            scratch_shapes=[pltpu.VMEM((CAPACITY, D), F32)]),
        out_shape=jax.ShapeDtypeStruct((B, N_EXPERTS, CAPACITY, D), BF16),
        compiler_params=_params("parallel", "arbitrary"),
    )(idx_flat, h2)


def _ffn_kernel(xe_ref, gv_ref, wg_ref, wu_ref, wd_ref, ye_ref, wgb_ref, wub_ref, wdb_ref):
    @pl.when(pl.program_id(1) == 0)
    def _():
        wgb_ref[...] = wg_ref[...].astype(BF16)
        wub_ref[...] = wu_ref[...].astype(BF16)
        wdb_ref[...] = wd_ref[...].astype(BF16)

    xe = xe_ref[...]
    hid = (_silu(_dot(xe, wgb_ref[...])) * _dot(xe, wub_ref[...])).astype(BF16)
    ye_ref[...] = _dot(hid, wdb_ref[...]) * gv_ref[...]


def _expert_ffn(xe, gval, w_gate, w_up, w_down):
    B = xe.shape[0]
    D = D_MODEL
    F = w_gate.shape[-1]
    return pl.pallas_call(
        _ffn_kernel,
        grid=(N_EXPERTS, B),
        in_specs=[pl.BlockSpec((None, None, CAPACITY, D), lambda e, b: (b, e, 0, 0)),
                  pl.BlockSpec((None, None, CAPACITY, 1), lambda e, b: (b, e, 0, 0)),
                  pl.BlockSpec((None, D, F), lambda e, b: (e, 0, 0)),
                  pl.BlockSpec((None, D, F), lambda e, b: (e, 0, 0)),
                  pl.BlockSpec((None, F, D), lambda e, b: (e, 0, 0))],
        out_specs=pl.BlockSpec((None, None, CAPACITY, D), lambda e, b: (b, e, 0, 0)),
        out_shape=jax.ShapeDtypeStruct((B, N_EXPERTS, CAPACITY, D), F32),
        scratch_shapes=[pltpu.VMEM((D, F), BF16), pltpu.VMEM((D, F), BF16), pltpu.VMEM((F, D), BF16)],
        compiler_params=_params("arbitrary", "arbitrary"),
    )(xe, gval, w_gate, w_up, w_down)


HALF_E = N_EXPERTS // 2
OUT_TILES = SEQ // TILE


def _combine_kernel(idx_ref, ye0_ref, ye1_ref, x1_ref, gt2_ref, fw_ref, o_ref, acc0_ref, acc1_ref):
    b = pl.program_id(0)
    s = pl.program_id(1)

    @pl.when(s == 0)
    def _():
        acc0_ref[...] = jnp.zeros_like(acc0_ref)
        acc1_ref[...] = jnp.zeros_like(acc1_ref)

    @pl.when(s < HALF_E)
    def _():
        base0 = (b * N_EXPERTS + s) * CAPACITY
        base1 = base0 + HALF_E * CAPACITY

        def body(r, _):
            t0 = idx_ref[base0 + r]
            t1 = idx_ref[base1 + r]
            acc0_ref[pl.ds(t0, 1), :] = acc0_ref[pl.ds(t0, 1), :] + ye0_ref[pl.ds(r, 1), :]
            acc1_ref[pl.ds(t1, 1), :] = acc1_ref[pl.ds(t1, 1), :] + ye1_ref[pl.ds(r, 1), :]
            return 0

        lax.fori_loop(0, CAPACITY, body, 0, unroll=GATHER_UNROLL)

    @pl.when(s >= HALF_E)
    def _():
        t = pl.multiple_of((s - HALF_E) * TILE, TILE)
        moe = acc0_ref[pl.ds(t, TILE), :] + acc1_ref[pl.ds(t, TILE), :]
        x2 = x1_ref[...] + gt2_ref[...] * moe
        o_ref[...] = x2 * lax.rsqrt(jnp.mean(x2 * x2, axis=-1, keepdims=True) + EPS) * fw_ref[...]


def _combine(idx_flat, ye, x1, gt2, final_norm):
    B = ye.shape[0]
    D = D_MODEL
    expert = lambda off: pl.BlockSpec((None, None, CAPACITY, D),
                                      lambda b, s, idx: (b, jnp.minimum(s, HALF_E - 1) + off, 0, 0))
    tile = pl.BlockSpec((None, TILE, D), lambda b, s, idx: (b, jnp.maximum(s - HALF_E, 0), 0))
    return pl.pallas_call(
        _combine_kernel,
        grid_spec=pltpu.PrefetchScalarGridSpec(
            num_scalar_prefetch=1,
            grid=(B, HALF_E + OUT_TILES),
            in_specs=[expert(0), expert(HALF_E), tile,
                      pl.BlockSpec((None, 1, D), lambda b, s, idx: (b, 0, 0)),
                      pl.BlockSpec((1, D), lambda b, s, idx: (0, 0))],
            out_specs=tile,
            scratch_shapes=[pltpu.VMEM((SEQ, D), F32), pltpu.VMEM((SEQ, D), F32)]),
        out_shape=jax.ShapeDtypeStruct((B, SEQ, D), F32),
        compiler_params=_params("parallel", "arbitrary"),
    )(idx_flat, ye, ye, x1, gt2, final_norm)


def kernel(x, c, ctx, c_ctx, w_mod, b_mod, norm_mix, norm_ffn, w_in, conv_qkv, a_log, dt_bias, gdn_norm, na_rpb,
           w_out, w_router, w_gate, w_up, w_down, final_norm):
    B, T, D = x.shape
    assert (T, D) == (SEQ, D_MODEL) and ctx.shape == (B, CTX_LEN, D) and w_mod.shape[0] == 1
    li = 0
    mod_rows = -(-(B + 1) // SUBLANES) * SUBLANES
    cc = jnp.zeros((mod_rows, D), F32).at[:B].set(c).at[B].set(c_ctx)
    mod = _modulation(cc, w_mod[li], b_mod[li])
    part = lambda j, rows: mod[:rows, j * D:(j + 1) * D].reshape(rows, 1, D)
    sh1, sc1 = part(0, B + 1), part(1, B + 1)
    gt1, sh2, sc2, gt2 = part(2, B), part(3, B), part(4, B), part(5, B)

    wi = w_in[li]
    q_end, g_end = 3 * GDN_WIDTH, 4 * GDN_WIDTH
    ab_end = g_end + 4 * GDN_HEADS
    wq = wi[:, :q_end].astype(BF16)
    wg = wi[:, q_end:g_end].astype(BF16)
    wab = jnp.zeros((D, LANES), F32).at[:, :4 * GDN_HEADS].set(wi[:, g_end:ab_end]).astype(BF16)
    wnq = wi[:, ab_end:ab_end + NA_WIDTH].astype(BF16)
    wnkv = wi[:, ab_end + NA_WIDTH:].astype(BF16)
    gpar = jnp.zeros((2, LANES), F32)
    gpar = gpar.at[0, :2 * GDN_HEADS].set(-jnp.exp(a_log[li].astype(F32)).reshape(-1))
    gpar = gpar.at[1, :2 * GDN_HEADS].set(dt_bias[li].astype(F32).reshape(-1))

    cos_tab, sin_tab = _rope_tables()
    qkv, gate, gb, naq, nakv = _in_projection(x, ctx, sh1, sc1, norm_mix[li].reshape(1, D),
                                              wq, wg, wab, wnq, wnkv, gpar, conv_qkv[li], cos_tab, sin_tab)
    o_f, o_b = _gdn_scan(qkv, gb)
    y_na = _neighbourhood_attention(naq, nakv, _na_bias_table(na_rpb[li]))

    wo = w_out[li].astype(BF16)
    x1, h2, aff_t = _out_projection(o_f, o_b, gate, y_na, x, gt1, sh2, sc2,
                                     gdn_norm[li].reshape(1, GDN_HEAD_DIM), norm_ffn[li].reshape(1, D),
                                     wo[:GDN_WIDTH], wo[GDN_WIDTH:], w_router[li].T)
    idx, gval = _route(aff_t)
    idx_flat = idx.reshape(-1)
    xe = _dispatch(idx_flat, h2)
    ye = _expert_ffn(xe, gval, w_gate[li], w_up[li], w_down[li])
    return _combine(idx_flat, ye, x1, gt2, final_norm.reshape(1, D))
```

```python
import math

import numpy as np
import jax
import jax.numpy as jnp
from jax import lax
from jax.experimental import pallas as pl
from jax.experimental.pallas import tpu as pltpu

F32 = jnp.float32
BF16 = jnp.bfloat16

D_MODEL = 1024
SEQ = 4096
CTX_LEN = 256
GRID_W = 64
GRID_ROWS = SEQ // GRID_W
GDN_HEADS = 4
GDN_HEAD_DIM = 128
GDN_WIDTH = GDN_HEADS * GDN_HEAD_DIM
CONV_W = 5
CHUNK = 64
ROPE_BASE = 10000.0
NA_HEADS = 8
NA_HEAD_DIM = 64
NA_WIDTH = NA_HEADS * NA_HEAD_DIM
NA_ROWS = 8
NA_COLS = 16
N_EXPERTS = 16
CAPACITY = 2 * SEQ // N_EXPERTS
EPS = 1e-6

TILE = 256
T_ALL = CTX_LEN + SEQ
N_TILES = T_ALL // TILE
N_CHUNKS = T_ALL // CHUNK
CTX_CHUNKS = CTX_LEN // CHUNK
LANES = 128
SUBLANES = 8
VMEM_LIMIT = 56 * 1024 * 1024


def _params(*sem):
    return pltpu.CompilerParams(dimension_semantics=sem, vmem_limit_bytes=VMEM_LIMIT)


def _dot(a, b):
    return jnp.dot(a, b, preferred_element_type=F32)


def _dot_nt(a, b):
    return lax.dot_general(a, b, (((1,), (1,)), ((), ())), preferred_element_type=F32)


def _dot_tn(a, b):
    return lax.dot_general(a, b, (((0,), (0,)), ((), ())), preferred_element_type=F32)


def _split2(a):
    hi = a.astype(BF16)
    lo = (a - hi.astype(F32)).astype(BF16)
    return hi, lo


def _split3(a):
    hi = a.astype(BF16)
    r = a - hi.astype(F32)
    mid = r.astype(BF16)
    lo = (r - mid.astype(F32)).astype(BF16)
    return hi, mid, lo


def _dot3(a, b):
    ah, al = _split2(a)
    bh, bl = _split2(b)
    return _dot(ah, bh) + (_dot(ah, bl) + _dot(al, bh))


def _silu(x):
    return x * jax.nn.sigmoid(x)


def _mod_kernel(c_ref, w_ref, b_ref, o_ref):
    s = _silu(c_ref[...])
    o_ref[...] = _dot3(s, w_ref[...]) + b_ref[...]


def _modulation(cc, w_mod, b_mod):
    rows, d = cc.shape
    n = w_mod.shape[1]
    bn = 1024
    return pl.pallas_call(
        _mod_kernel,
        grid=(n // bn,),
        in_specs=[pl.BlockSpec((rows, d), lambda j: (0, 0)),
                  pl.BlockSpec((d, bn), lambda j: (0, j)),
                  pl.BlockSpec((1, bn), lambda j: (0, j))],
        out_specs=pl.BlockSpec((rows, bn), lambda j: (0, j)),
        out_shape=jax.ShapeDtypeStruct((rows, n), F32),
        compiler_params=_params("parallel"),
    )(cc, w_mod, b_mod.reshape(1, n))


HALO = SUBLANES


def _inproj_kernel(x_ref, ctx_ref, xp_ref, xn_ref, sh_ref, sc_ref, nw_ref, wq_ref, wg_ref, wab_ref, wnq_ref,
                   wnkv_ref, gpar_ref, cw_ref, cos_ref, sin_ref, qkv_ref, gate_ref, gb_ref, naq_ref, nakv_ref, ext_ref):
    i = pl.program_id(1)
    has_prev = i >= 2
    has_next = jnp.logical_and(i >= 1, i < N_TILES - 1)
    xt = jnp.concatenate([xp_ref[...], jnp.where(i == 0, ctx_ref[...], x_ref[...]), xn_ref[...]], axis=0)
    ms = jnp.mean(xt * xt, axis=-1, keepdims=True)
    h = xt * lax.rsqrt(ms + EPS) * nw_ref[...]
    h = h * (1.0 + sc_ref[...]) + sh_ref[...]
    pq = _dot(h.astype(BF16), wq_ref[...])
    ext_ref[0:HALO, :] = jnp.where(has_prev, pq[0:HALO], 0.0)
    ext_ref[HALO:HALO + TILE, :] = pq[HALO:HALO + TILE]
    ext_ref[HALO + TILE:, :] = jnp.where(has_next, pq[HALO + TILE:], 0.0)
    hb = h[HALO:HALO + TILE].astype(BF16)
    gate_ref[...] = _dot(hb, wg_ref[...])
    first = HALO - CONV_W // 2
    acc = ext_ref[first:first + TILE, :] * cw_ref[0:1, :]
    for k in range(1, CONV_W):
        acc = acc + ext_ref[first + k:first + k + TILE, :] * cw_ref[k:k + 1, :]
    naq_ref[...] = (_dot(hb, wnq_ref[...]) * (NA_HEAD_DIM ** -0.5)).astype(BF16)
    y = _silu(acc)
    nakv_ref[...] = _dot(hb, wnkv_ref[...]).astype(BF16)
    cos = cos_ref[...]
    sin = sin_ref[...]
    lane = lax.broadcasted_iota(jnp.int32, (TILE, LANES), 1)
    take_upper = (lane % (GDN_HEAD_DIM // 2)) < (GDN_HEAD_DIM // 4)
    for j in range(2 * GDN_HEADS):
        t = y[:, j * LANES:(j + 1) * LANES]
        t = t * lax.rsqrt(jnp.sum(t * t, axis=-1, keepdims=True) + EPS)
        partner = jnp.where(take_upper, pltpu.roll(t, LANES - GDN_HEAD_DIM // 4, 1),
                            pltpu.roll(t, GDN_HEAD_DIM // 4, 1))
        t = t * cos + partner * sin
        if j < GDN_HEADS:
            t = t * (GDN_HEAD_DIM ** -0.5)
        qkv_ref[:, j * LANES:(j + 1) * LANES] = t
    qkv_ref[:, 2 * GDN_WIDTH:] = y[:, 2 * GDN_WIDTH:]
    ab = _dot(hb, wab_ref[...])
    z = ab + gpar_ref[1:2, :]
    softplus = jnp.maximum(z, 0.0) + jnp.log1p(jnp.exp(-jnp.abs(z)))
    lane = lax.broadcasted_iota(jnp.int32, ab.shape, 1)
    gb_ref[...] = jnp.where(lane < 2 * GDN_HEADS, gpar_ref[0:1, :] * softplus, jax.nn.sigmoid(ab))


def _in_projection(x, ctx, sh, sc, norm_w, wq, wg, wab, wnq, wnkv, gpar, conv_w, cos_tab, sin_tab):
    B = x.shape[0]
    D = D_MODEL
    per = TILE // HALO
    tok = lambda n: pl.BlockSpec((None, TILE, n), lambda b, i: (b, i, 0))
    full = lambda a: pl.BlockSpec(a.shape, lambda b, i: (0,) * a.ndim)
    modrow = pl.BlockSpec((None, 1, D), lambda b, i: (jnp.where(i == 0, B, b), 0, 0))
    rope = pl.BlockSpec((TILE, LANES), lambda b, i: (i, 0))
    before = pl.BlockSpec((None, HALO, D), lambda b, i: (b, jnp.maximum((i - 1) * per - 1, 0), 0))
    after = pl.BlockSpec((None, HALO, D), lambda b, i: (b, jnp.minimum(jnp.maximum(i, 1) * per, SEQ // HALO - 1), 0))
    outs = [(3 * GDN_WIDTH, F32), (GDN_WIDTH, F32), (LANES, F32), (NA_WIDTH, BF16), (2 * NA_WIDTH, BF16)]
    return pl.pallas_call(
        _inproj_kernel,
        grid=(B, N_TILES),
        in_specs=[pl.BlockSpec((None, TILE, D), lambda b, i: (b, jnp.maximum(i - 1, 0), 0)),
                  pl.BlockSpec((None, TILE, D), lambda b, i: (b, 0, 0)), before, after,
                  modrow, modrow, full(norm_w), full(wq), full(wg), full(wab), full(wnq), full(wnkv), full(gpar),
                  full(conv_w), rope, rope],
        out_specs=[tok(n) for n, _ in outs],
        out_shape=[jax.ShapeDtypeStruct((B, T_ALL, n), dt) for n, dt in outs],
        scratch_shapes=[pltpu.VMEM((TILE + 2 * HALO, 3 * GDN_WIDTH), F32)],
        compiler_params=_params("parallel", "arbitrary"),
    )(x, ctx, x, x, sh, sc, norm_w, wq, wg, wab, wnq, wnkv, gpar, conv_w, cos_tab, sin_tab)


def _rope_tables():
    half = GDN_HEAD_DIM // 2
    pairs = half // 2
    t = np.arange(SEQ)
    inv_freq = jnp.asarray(ROPE_BASE, F32) ** (-jnp.arange(pairs, dtype=F32) / pairs)

    def tab(pos):
        ang = jnp.asarray(pos, F32)[:, None] * inv_freq[None, :]
        c, s = jnp.cos(ang), jnp.sin(ang)
        return jnp.concatenate([c, c], axis=-1), jnp.concatenate([-s, s], axis=-1)

    cr, sr = tab(t // GRID_W)
    cc, sc = tab(t % GRID_W)
    cos = jnp.concatenate([cr, cc], axis=-1)
    sin = jnp.concatenate([sr, sc], axis=-1)
    cos = jnp.concatenate([jnp.ones((CTX_LEN, LANES), F32), cos], axis=0)
    sin = jnp.concatenate([jnp.zeros((CTX_LEN, LANES), F32), sin], axis=0)
    return cos, sin


SCAN_SAMPLES = 8


def _gdn_scan_kernel(qf_ref, qb_ref, gf_ref, gb_ref, of_ref, ob_ref, s_ref):
    n = pl.program_id(1)

    @pl.when(n == 0)
    def _():
        s_ref[...] = jnp.zeros_like(s_ref)

    row = lax.broadcasted_iota(jnp.int32, (CHUNK, CHUNK), 0)
    col = lax.broadcasted_iota(jnp.int32, (CHUNK, CHUNK), 1)
    eye = jnp.where(row == col, 1.0, 0.0).astype(F32)
    chains = range(SCAN_SAMPLES * 2 * GDN_HEADS)
    bf = lambda t: t.astype(BF16)
    each = lambda f, *cols: [f(*args) for args in zip(*cols)]
    q, k, v, gcol, grow, gtot, bcol, incl, strict = [], [], [], [], [], [], [], [], []
    for s in range(SCAN_SAMPLES):
        for d in range(2):
            src = qf_ref if d == 0 else qb_ref
            gbt = (gf_ref if d == 0 else gb_ref)[s]
            inc = (row >= col) if d == 0 else (row <= col)
            cm = jnp.where(inc, 1.0, 0.0).astype(BF16)
            g1, g2, g3 = _split3(gbt)
            gc = _dot(cm, g1) + (_dot(cm, g2) + _dot(cm, g3))
            gct = gc.T
            tot = gc[CHUNK - 1:CHUNK, :] if d == 0 else gc[0:1, :]
            for h in range(GDN_HEADS):
                c = d * GDN_HEADS + h
                gcol.append(gc[:, c:c + 1])
                grow.append(gct[c:c + 1, :])
                gtot.append(tot[:, c:c + 1])
                bcol.append(gbt[:, 2 * GDN_HEADS + c:2 * GDN_HEADS + c + 1])
                q.append(src[s, :, h * LANES:(h + 1) * LANES])
                k.append(src[s, :, GDN_WIDTH + h * LANES:GDN_WIDTH + (h + 1) * LANES])
                v.append(src[s, :, 2 * GDN_WIDTH + h * LANES:2 * GDN_WIDTH + (h + 1) * LANES])
                incl.append(inc)
                strict.append((row > col) if d == 0 else (row < col))
    s_old = [s_ref[c] for c in chains]
    decay = each(lambda m, gc_, gr_: jnp.exp(jnp.where(m, gc_ - gr_, -jnp.inf)), incl, gcol, grow)
    eg = each(jnp.exp, gcol)
    kb = each(lambda k_, b_: k_ * b_, k, bcol)
    a = each(lambda kb_, q_, k_: _dot_nt(bf(jnp.concatenate([kb_, q_], axis=0)), bf(k_)), kb, q, k)
    lower = each(lambda m, a_, dc: jnp.where(m, a_[:CHUNK] * dc, 0.0), strict, a, decay)
    attn = each(lambda a_, dc: bf(a_[CHUNK:] * dc), a, decay)
    tinv = each(lambda l_: eye - l_, lower)
    m = each(lambda l_: _dot(bf(l_), bf(l_)), lower)
    for _ in range(int(math.log2(CHUNK)) - 2):
        prod = each(lambda m_, t_: _dot(bf(jnp.concatenate([m_, t_], axis=0)), bf(m_)), m, tinv)
        tinv = each(lambda t_, p_: t_ + p_[CHUNK:], tinv, prod)
        m = each(lambda p_: p_[:CHUNK], prod)
    tinv = each(lambda t_, m_: t_ + _dot(bf(t_), bf(m_)), tinv, m)
    resid = each(lambda t_, l_: (eye - t_) - _dot3(l_, t_), tinv, lower)
    tinv = each(lambda t_, r_: t_ + _dot(bf(t_), bf(r_)), tinv, resid)
    uw = each(lambda t_, v_, b_, kb_, eg_: _dot(bf(t_), bf(jnp.concatenate([v_ * b_, kb_ * eg_], axis=1))),
              tinv, v, bcol, kb, eg)
    ws = each(lambda uw_, q_, eg_, s_: _dot(bf(jnp.concatenate([uw_[:, LANES:], q_ * eg_], axis=0)), bf(s_)),
              uw, q, eg, s_old)
    vb = each(lambda uw_, ws_: bf(uw_[:, :LANES] - ws_[:CHUNK]), uw, ws)
    o = each(lambda ws_, at_, vb_: ws_[CHUNK:] + _dot(at_, vb_), ws, attn, vb)
    s_new = each(lambda s_, gt_, k_, gc_, vb_: s_ * jnp.exp(gt_) + _dot_tn(bf(k_ * jnp.exp(gt_ - gc_)), vb_),
                 s_old, gtot, k, gcol, vb)
    for c in chains:
        s, d, h = c // (2 * GDN_HEADS), (c // GDN_HEADS) % 2, c % GDN_HEADS
        o_ref = of_ref if d == 0 else ob_ref
        o_ref[s, :, h * LANES:(h + 1) * LANES] = o[c]
        s_ref[c] = s_new[c]


def _gdn_scan(qkv, gb):
    B = qkv.shape[0]
    S = SCAN_SAMPLES
    assert B % S == 0
    fwd = lambda b, n: (b, n, 0)
    bwd = lambda b, n: (b, jnp.where(n < CTX_CHUNKS, CTX_CHUNKS - 1 - n, N_CHUNKS + CTX_CHUNKS - 1 - n), 0)
    out = jax.ShapeDtypeStruct((B, T_ALL, GDN_WIDTH), F32)
    return pl.pallas_call(
        _gdn_scan_kernel,
        grid=(B // S, N_CHUNKS),
        in_specs=[pl.BlockSpec((S, CHUNK, 3 * GDN_WIDTH), fwd),
                  pl.BlockSpec((S, CHUNK, 3 * GDN_WIDTH), bwd),
                  pl.BlockSpec((S, CHUNK, LANES), fwd),
                  pl.BlockSpec((S, CHUNK, LANES), bwd)],
        out_specs=[pl.BlockSpec((S, CHUNK, GDN_WIDTH), fwd),
                   pl.BlockSpec((S, CHUNK, GDN_WIDTH), bwd)],
        out_shape=[out, out],
        scratch_shapes=[pltpu.VMEM((S * 2 * GDN_HEADS, GDN_HEAD_DIM, GDN_HEAD_DIM), F32)],
        compiler_params=_params("parallel", "arbitrary"),
    )(qkv, qkv, gb, gb)


WIN_TOKENS = NA_ROWS * GRID_W
NA_STEP_ROWS = 4


def _window_start(r):
    return jnp.clip(r - NA_ROWS // 2, 0, GRID_ROWS - NA_ROWS)


def _na_kernel(q_ref, kv_ref, *rest):
    bias_refs, o_ref = rest[:NA_STEP_ROWS], rest[NA_STEP_ROWS]
    lane = lax.broadcasted_iota(jnp.int32, (GRID_W, LANES), 1)
    low = lane < NA_HEAD_DIM
    zero = jnp.zeros((GRID_W, LANES), BF16)
    each = lambda f, *cols: [f(*args) for args in zip(*cols)]
    rowmax = lambda t: jnp.max(t, axis=-1, keepdims=True)
    rowsum = lambda t: jnp.sum(t, axis=-1, keepdims=True)
    items =[(j, p) for j in range(NA_STEP_ROWS) for p in range(NA_HEADS // 2)]
    qrows = [slice(j * GRID_W, (j + 1) * GRID_W) for j, _ in items]
    klanes = [slice(p * LANES, (p + 1) * LANES) for _, p in items]
    vlanes = [slice(NA_WIDTH + s.start, NA_WIDTH + s.stop) for s in klanes]
    offs = [pl.multiple_of(CTX_LEN + _window_start(pl.program_id(1) * NA_STEP_ROWS + j) * GRID_W, GRID_W)
            for j in range(NA_STEP_ROWS)]
    off = [offs[j] for j, _ in items]
    q2 = each(lambda rw, s: jnp.concatenate([jnp.where(low, q_ref[rw, s], zero),
                                             jnp.where(low, zero, q_ref[rw, s])], axis=0), qrows, klanes)
    s2 = each(lambda q_, o_, s: _dot_nt(q_, kv_ref[pl.ds(o_, WIN_TOKENS), s]), q2, off, klanes)
    c2 = each(lambda q_, s: _dot_nt(q_, kv_ref[0:CTX_LEN, s]), q2, klanes)
    halves = [(i, t) for i in range(len(items)) for t in range(2)]
    half = lambda t: slice(t * GRID_W, (t + 1) * GRID_W)
    s_lat = [s2[i][half(t)] + bias_refs[items[i][0]][2 * items[i][1] + t] for i, t in halves]
    s_ctx = [c2[i][half(t)] for i, t in halves]
    mx = each(lambda a, b: jnp.maximum(rowmax(a), rowmax(b)), s_lat, s_ctx)
    p_lat = each(lambda a, m: jnp.exp(a - m), s_lat, mx)
    p_ctx = each(lambda a, m: jnp.exp(a - m), s_ctx, mx)
    den = each(lambda a, b: rowsum(a) + rowsum(b), p_lat, p_ctx)
    stack = lambda parts: [jnp.concatenate(parts[2 * i:2 * i + 2], axis=0) for i in range(len(items))]
    to_bf = lambda parts: [t.astype(BF16) for t in parts]
    o = each(lambda a, b, o_, s: _dot(a, kv_ref[pl.ds(o_, WIN_TOKENS), s]) + _dot(b, kv_ref[0:CTX_LEN, s]),
             stack(to_bf(p_lat)), stack(to_bf(p_ctx)), off, vlanes)
    o = each(lambda o_, d_: o_ / d_, o, stack(den))
    for rw, s, o_ in zip(qrows, klanes, o):
        o_ref[rw, s] = jnp.where(low, o_[:GRID_W], o_[GRID_W:]).astype(BF16)


def _neighbourhood_attention(naq, nakv, bias_tab):
    B = naq.shape[0]
    R = NA_STEP_ROWS
    step_tokens = R * GRID_W
    assert GRID_ROWS % R == 0 and CTX_LEN % step_tokens == 0
    q_blocks_before = CTX_LEN // step_tokens

    def bias_spec(j):
        return pl.BlockSpec((None, NA_HEADS, GRID_W, WIN_TOKENS),
                            lambda b, s: (s * R + j - _window_start(s * R + j), 0, 0, 0))

    return pl.pallas_call(
        _na_kernel,
        grid=(B, GRID_ROWS // R),
        in_specs=[pl.BlockSpec((None, step_tokens, NA_WIDTH), lambda b, s: (b, s + q_blocks_before, 0)),
                  pl.BlockSpec((None, T_ALL, 2 * NA_WIDTH), lambda b, s: (b, 0, 0))]
                 + [bias_spec(j) for j in range(R)],
        out_specs=pl.BlockSpec((None, step_tokens, NA_WIDTH), lambda b, s: (b, s, 0)),
        out_shape=jax.ShapeDtypeStruct((B, SEQ, NA_WIDTH), BF16),
        compiler_params=_params("parallel", "arbitrary"),
    )(naq, nakv, *([bias_tab] * R))


def _na_bias_table(rpb):
    col = np.arange(GRID_W)
    col_start = np.clip(col - NA_COLS // 2, 0, GRID_W - NA_COLS)
    col_in = (col[None, :] >= col_start[:, None]) & (col[None, :] < col_start[:, None] + NA_COLS)
    col_idx = np.clip(col[None, :] - col[:, None], -(NA_COLS - 1), NA_COLS - 1) + (NA_COLS - 1)
    row_idx = np.arange(NA_ROWS)[None, :] - np.arange(NA_ROWS)[:, None] + (NA_ROWS - 1)
    row_sel = (row_idx[..., None] == np.arange(2 * NA_ROWS - 1)).astype(np.float32)
    col_sel = (col_idx[..., None] == np.arange(2 * NA_COLS - 1)).astype(np.float32)
    t = jnp.einsum('hrc,vjr,qkc->vhqjk', rpb.astype(F32), row_sel, col_sel, precision=lax.Precision.HIGHEST)
    t = jnp.where(col_in[None, None, :, None, :], t, -jnp.inf)
    return t.reshape(NA_ROWS, NA_HEADS, GRID_W, WIN_TOKENS)


def _outproj_kernel(of_ref, ob_ref, gate_ref, yna_ref, x_ref, gt1_ref, sh2_ref, sc2_ref, gn_ref, nf_ref,
                    wog_ref, won_ref, wr_ref, x1_ref, h2_ref, aff_ref):
    o = of_ref[...] + ob_ref[...]
    ys = []
    for h in range(GDN_HEADS):
        lanes = slice(h * LANES, (h + 1) * LANES)
        oh = o[:, lanes]
        yh = oh * lax.rsqrt(jnp.mean(oh * oh, axis=-1, keepdims=True) + EPS) * gn_ref[...]
        ys.append(yh * _silu(gate_ref[:, lanes]))
    yg = jnp.concatenate(ys, axis=1).astype(BF16)
    y = _dot(yg, wog_ref[...]) + _dot(yna_ref[...], won_ref[...])
    x1 = x_ref[...] + gt1_ref[...] * y
    x1_ref[...] = x1
    h2 = x1 * lax.rsqrt(jnp.mean(x1 * x1, axis=-1, keepdims=True) + EPS) * nf_ref[...]
    h2 = h2 * (1.0 + sc2_ref[...]) + sh2_ref[...]
    h2_ref[...] = h2
    hb = h2.astype(BF16)
    h_lo = (h2 - hb.astype(F32)).astype(BF16)
    wr_hi, wr_lo = _split2(wr_ref[...])
    logits = _dot_nt(wr_hi, hb) + (_dot_nt(wr_hi, h_lo) + _dot_nt(wr_lo, hb))
    e = jnp.exp(logits - jnp.max(logits, axis=0, keepdims=True))
    aff_ref[...] = e / jnp.sum(e, axis=0, keepdims=True)


def _out_projection(o_f, o_b, gate, y_na, x, gt1, sh2, sc2, gdn_norm, norm_ffn, wog, won, wrt):
    B = x.shape[0]
    D = D_MODEL
    lat = lambda n: pl.BlockSpec((None, TILE, n), lambda b, i: (b, i + 1, 0))
    tok = lambda n: pl.BlockSpec((None, TILE, n), lambda b, i: (b, i, 0))
    full = lambda a: pl.BlockSpec(a.shape, lambda b, i: (0,) * a.ndim)
    modrow = pl.BlockSpec((None, 1, D), lambda b, i: (b, 0, 0))
    return pl.pallas_call(
        _outproj_kernel,
        grid=(B, SEQ // TILE),
        in_specs=[lat(GDN_WIDTH), lat(GDN_WIDTH), lat(GDN_WIDTH), tok(NA_WIDTH), tok(D),
                  modrow, modrow, modrow, full(gdn_norm), full(norm_ffn), full(wog), full(won), full(wrt)],
        out_specs=[tok(D), tok(D),
                   pl.BlockSpec((None, N_EXPERTS, TILE), lambda b, i: (b, 0, i))],
        out_shape=[jax.ShapeDtypeStruct((B, SEQ, D), F32),
                   jax.ShapeDtypeStruct((B, SEQ, D), F32),
                   jax.ShapeDtypeStruct((B, N_EXPERTS, SEQ), F32)],
        compiler_params=_params("parallel", "arbitrary"),
    )(o_f, o_b, gate, y_na, x, gt1, sh2, sc2, gdn_norm, norm_ffn, wog, won, wrt)


TOK_BLOCKS = SEQ // LANES
ROUTE_ROWS = N_EXPERTS * TOK_BLOCKS
F32_MAGNITUDE_BITS = 31


def _topk_kernel(aff_ref, idx_ref, gval_ref):
    a = aff_ref[...]
    bf = lambda t: t.astype(BF16)
    mask = lambda c: jnp.where(c, 1.0, 0.0).astype(BF16)
    li = lax.broadcasted_iota(jnp.int32, (LANES, LANES), 0)
    lj = lax.broadcasted_iota(jnp.int32, (LANES, LANES), 1)
    ones = jnp.ones((LANES, LANES), BF16)
    upper = mask(li <= lj)
    ri = lax.broadcasted_iota(jnp.int32, (ROUTE_ROWS, ROUTE_ROWS), 0)
    rj = lax.broadcasted_iota(jnp.int32, (ROUTE_ROWS, ROUTE_ROWS), 1)
    shift = TOK_BLOCKS.bit_length() - 1
    same = (ri >> shift) == (rj >> shift)
    expert_sum = mask(same)
    rows_before = mask(jnp.logical_and(same, rj < ri))
    per_expert = lambda m: _dot(expert_sum, bf(_dot(m, ones)))

    def bisect(it, prefix):
        cand = prefix | (jnp.int32(1) << (F32_MAGNITUDE_BITS - 1 - it))
        cnt = per_expert(mask(a >= pltpu.bitcast(cand, F32)))
        return jnp.where(cnt >= CAPACITY, cand, prefix)

    thr = lax.fori_loop(0, F32_MAGNITUDE_BITS, bisect, jnp.zeros((ROUTE_ROWS, LANES), jnp.int32))
    gt = a >= pltpu.bitcast(thr + 1, F32)
    eq = jnp.logical_and(a >= pltpu.bitcast(thr, F32), jnp.logical_not(gt))
    need = CAPACITY - per_expert(mask(gt))

    def prefix(m):
        return _dot(m, upper), _dot(rows_before, bf(_dot(m, ones)))

    eq_m = mask(eq)
    eq_in_row, eq_rows_before = prefix(eq_m)
    eq_before = eq_in_row + eq_rows_before - eq_m.astype(F32)
    sel = jnp.logical_or(gt, jnp.logical_and(eq, eq_before < need))
    sel_m = mask(sel)
    rank_in_row, start = prefix(sel_m)
    local = jnp.where(sel, rank_in_row, 0.0)
    row_total = _dot(sel_m, ones)
    lane_f = lax.broadcasted_iota(jnp.int32, (CAPACITY, LANES), 1).astype(F32)
    slot = lax.broadcasted_iota(jnp.int32, (CAPACITY, LANES), 0).astype(F32)
    tok0 = (lax.broadcasted_iota(jnp.int32, (TOK_BLOCKS, LANES), 0) * LANES).astype(F32)
    tok0_hi, tok0_lo = _split2(tok0)
    ones_cl = jnp.ones((CAPACITY, LANES), BF16)
    pad = lambda t: jnp.concatenate([t, jnp.zeros((LANES - TOK_BLOCKS, LANES), t.dtype)], axis=0)

    def as_row(v):
        hi, lo = _split2(jnp.where(li == lj, pad(v), 0.0))
        return _dot(ones_cl, hi) + _dot(ones_cl, lo)

    for e in range(N_EXPERTS):
        rows = slice(e * TOK_BLOCKS, (e + 1) * TOK_BLOCKS)
        st, tot = start[rows], row_total[rows]
        st_r, tot_r = as_row(st), as_row(tot)
        owner = mask(jnp.logical_and(st_r <= slot, slot < st_r + tot_r))
        pick = lambda t: _dot(owner, pad(t))
        st_hi, st_lo = _split2(st)
        a_hi, a_mid, a_lo = _split3(a[rows])
        want = slot - (pick(st_hi) + pick(st_lo)) + 1.0
        hit = pick(bf(local[rows])) == want
        tok = jnp.sum(jnp.where(hit, lane_f, 0.0), axis=-1, keepdims=True)
        base = pick(tok0_hi) + pick(tok0_lo)
        val = pick(a_hi) + (pick(a_mid) + pick(a_lo))
        idx_ref[e] = (tok + base[:, 0:1]).astype(jnp.int32)
        gval_ref[e] = jnp.sum(jnp.where(hit, val, 0.0), axis=-1, keepdims=True)


def _route(aff_t):
    B = aff_t.shape[0]
    out = lambda dt: jax.ShapeDtypeStruct((B, N_EXPERTS, CAPACITY, 1), dt)
    spec = pl.BlockSpec((None, N_EXPERTS, CAPACITY, 1), lambda b: (b, 0, 0, 0))
    return pl.pallas_call(
        _topk_kernel,
        grid=(B,),
        in_specs=[pl.BlockSpec((None, ROUTE_ROWS, LANES), lambda b: (b, 0, 0))],
        out_specs=[spec, spec],
        out_shape=[out(jnp.int32), out(F32)],
        compiler_params=_params("parallel"),
    )(aff_t.reshape(B, ROUTE_ROWS, LANES))


GATHER_UNROLL = 8


def _dispatch_kernel(idx_ref, h_ref, xe_ref, rows_ref):
    base = (pl.program_id(0) * N_EXPERTS + pl.program_id(1)) * CAPACITY

    def body(r, _):
        t = idx_ref[base + r]
        rows_ref[pl.ds(r, 1), :] = h_ref[pl.ds(t, 1), :]
        return 0

    lax.fori_loop(0, CAPACITY, body, 0, unroll=GATHER_UNROLL)
    xe_ref[...] = rows_ref[...].astype(BF16)


def _dispatch(idx_flat, h2):
    B = h2.shape[0]
    D = D_MODEL
    return pl.pallas_call(
        _dispatch_kernel,
        grid_spec=pltpu.PrefetchScalarGridSpec(
            num_scalar_prefetch=1,
            grid=(B, N_EXPERTS),
            in_specs=[pl.BlockSpec((None, SEQ, D), lambda b, e, idx: (b, 0, 0))],
            out_specs=pl.BlockSpec((None, None, CAPACITY, D), lambda b, e, idx: (b, e, 0, 0)),
            scratch_shapes=[pltpu.VMEM((CAPACITY, D), F32)]),
        out_shape=jax.ShapeDtypeStruct((B, N_EXPERTS, CAPACITY, D), BF16),
        compiler_params=_params("parallel", "arbitrary"),
    )(idx_flat, h2)


def _ffn_kernel(xe_ref, gv_ref, wg_ref, wu_ref, wd_ref, ye_ref, wgb_ref, wub_ref, wdb_ref):
    @pl.when(pl.program_id(1) == 0)
    def _():
        wgb_ref[...] = wg_ref[...].astype(BF16)
        wub_ref[...] = wu_ref[...].astype(BF16)
        wdb_ref[...] = wd_ref[...].astype(BF16)

    xe = xe_ref[...]
    hid = (_silu(_dot(xe, wgb_ref[...])) * _dot(xe, wub_ref[...])).astype(BF16)
    ye_ref[...] = _dot(hid, wdb_ref[...]) * gv_ref[...]


def _expert_ffn(xe, gval, w_gate, w_up, w_down):
    B = xe.shape[0]
    D = D_MODEL
    F = w_gate.shape[-1]
    return pl.pallas_call(
        _ffn_kernel,
        grid=(N_EXPERTS, B),
        in_specs=[pl.BlockSpec((None, None, CAPACITY, D), lambda e, b: (b, e, 0, 0)),
                  pl.BlockSpec((None, None, CAPACITY, 1), lambda e, b: (b, e, 0, 0)),
                  pl.BlockSpec((None, D, F), lambda e, b: (e, 0, 0)),
                  pl.BlockSpec((None, D, F), lambda e, b: (e, 0, 0)),
                  pl.BlockSpec((None, F, D), lambda e, b: (e, 0, 0))],
        out_specs=pl.BlockSpec((None, None, CAPACITY, D), lambda e, b: (b, e, 0, 0)),
        out_shape=jax.ShapeDtypeStruct((B, N_EXPERTS, CAPACITY, D), F32),
        scratch_shapes=[pltpu.VMEM((D, F), BF16), pltpu.VMEM((D, F), BF16), pltpu.VMEM((F, D), BF16)],
        compiler_params=_params("arbitrary", "arbitrary"),
    )(xe, gval, w_gate, w_up, w_down)


HALF_E = N_EXPERTS // 2
OUT_TILES = SEQ // TILE


def _combine_kernel(idx_ref, ye0_ref, ye1_ref, x1_ref, gt2_ref, fw_ref, o_ref, acc0_ref, acc1_ref):
    b = pl.program_id(0)
    s = pl.program_id(1)

    @pl.when(s == 0)
    def _():
        acc0_ref[...] = jnp.zeros_like(acc0_ref)
        acc1_ref[...] = jnp.zeros_like(acc1_ref)

    @pl.when(s < HALF_E)
    def _():
        base0 = (b * N_EXPERTS + s) * CAPACITY
        base1 = base0 + HALF_E * CAPACITY

        def body(r, _):
            t0 = idx_ref[base0 + r]
            t1 = idx_ref[base1 + r]
            acc0_ref[pl.ds(t0, 1), :] = acc0_ref[pl.ds(t0, 1), :] + ye0_ref[pl.ds(r, 1), :]
            acc1_ref[pl.ds(t1, 1), :] = acc1_ref[pl.ds(t1, 1), :] + ye1_ref[pl.ds(r, 1), :]
            return 0

        lax.fori_loop(0, CAPACITY, body, 0, unroll=GATHER_UNROLL)

    @pl.when(s >= HALF_E)
    def _():
        t = pl.multiple_of((s - HALF_E) * TILE, TILE)
        moe = acc0_ref[pl.ds(t, TILE), :] + acc1_ref[pl.ds(t, TILE), :]
        x2 = x1_ref[...] + gt2_ref[...] * moe
        o_ref[...] = x2 * lax.rsqrt(jnp.mean(x2 * x2, axis=-1, keepdims=True) + EPS) * fw_ref[...]


def _combine(idx_flat, ye, x1, gt2, final_norm):
    B = ye.shape[0]
    D = D_MODEL
    expert = lambda off: pl.BlockSpec((None, None, CAPACITY, D),
                                      lambda b, s, idx: (b, jnp.minimum(s, HALF_E - 1) + off, 0, 0))
    tile = pl.BlockSpec((None, TILE, D), lambda b, s, idx: (b, jnp.maximum(s - HALF_E, 0), 0))
    return pl.pallas_call(
        _combine_kernel,
        grid_spec=pltpu.PrefetchScalarGridSpec(
            num_scalar_prefetch=1,
            grid=(B, HALF_E + OUT_TILES),
            in_specs=[expert(0), expert(HALF_E), tile,
                      pl.BlockSpec((None, 1, D), lambda b, s, idx: (b, 0, 0)),
                      pl.BlockSpec((1, D), lambda b, s, idx: (0, 0))],
            out_specs=tile,
            scratch_shapes=[pltpu.VMEM((SEQ, D), F32), pltpu.VMEM((SEQ, D), F32)]),
        out_shape=jax.ShapeDtypeStruct((B, SEQ, D), F32),
        compiler_params=_params("parallel", "arbitrary"),
    )(idx_flat, ye, ye, x1, gt2, final_norm)


def kernel(x, c, ctx, c_ctx, w_mod, b_mod, norm_mix, norm_ffn, w_in, conv_qkv, a_log, dt_bias, gdn_norm, na_rpb,
           w_out, w_router, w_gate, w_up, w_down, final_norm):
    B, T, D = x.shape
    assert (T, D) == (SEQ, D_MODEL) and ctx.shape == (B, CTX_LEN, D) and w_mod.shape[0] == 1
    li = 0
    mod_rows = -(-(B + 1) // SUBLANES) * SUBLANES
    cc = jnp.zeros((mod_rows, D), F32).at[:B].set(c).at[B].set(c_ctx)
    mod = _modulation(cc, w_mod[li], b_mod[li])
    part = lambda j, rows: mod[:rows, j * D:(j + 1) * D].reshape(rows, 1, D)
    sh1, sc1 = part(0, B + 1), part(1, B + 1)
    gt1, sh2, sc2, gt2 = part(2, B), part(3, B), part(4, B), part(5, B)

    wi = w_in[li]
    q_end, g_end = 3 * GDN_WIDTH, 4 * GDN_WIDTH
    ab_end = g_end + 4 * GDN_HEADS
    wq = wi[:, :q_end].astype(BF16)
    wg = wi[:, q_end:g_end].astype(BF16)
    wab = jnp.zeros((D, LANES), F32).at[:, :4 * GDN_HEADS].set(wi[:, g_end:ab_end]).astype(BF16)
    wnq = wi[:, ab_end:ab_end + NA_WIDTH].astype(BF16)
    wnkv = wi[:, ab_end + NA_WIDTH:].astype(BF16)
    gpar = jnp.zeros((2, LANES), F32)
    gpar = gpar.at[0, :2 * GDN_HEADS].set(-jnp.exp(a_log[li].astype(F32)).reshape(-1))
    gpar = gpar.at[1, :2 * GDN_HEADS].set(dt_bias[li].astype(F32).reshape(-1))

    cos_tab, sin_tab = _rope_tables()
    qkv, gate, gb, naq, nakv = _in_projection(x, ctx, sh1, sc1, norm_mix[li].reshape(1, D),
                                              wq, wg, wab, wnq, wnkv, gpar, conv_qkv[li], cos_tab, sin_tab)
    o_f, o_b = _gdn_scan(qkv, gb)
    y_na = _neighbourhood_attention(naq, nakv, _na_bias_table(na_rpb[li]))

    wo = w_out[li].astype(BF16)
    x1, h2, aff_t = _out_projection(o_f, o_b, gate, y_na, x, gt1, sh2, sc2,
                                     gdn_norm[li].reshape(1, GDN_HEAD_DIM), norm_ffn[li].reshape(1, D),
                                     wo[:GDN_WIDTH], wo[GDN_WIDTH:], w_router[li].T)
    idx, gval = _route(aff_t)
    idx_flat = idx.reshape(-1)
    xe = _dispatch(idx_flat, h2)
    ye = _expert_ffn(xe, gval, w_gate[li], w_up[li], w_down[li])
    return _combine(idx_flat, ye, x1, gt2, final_norm.reshape(1, D))
```
